```python
import jax, jax.numpy as jnp
from jax import lax
import numpy as np

D_MODEL = 1024
BATCH = 2
SEQ = 8192
DEPTH = 1

HEAD_DIM = 64
N_HEADS_DSA = 8
N_HEADS_FOX = 8
N_HEADS_IDX = 8
IDX_DIM = 64
TOPK_MAX = 256
Q_BLOCK = 128
D_FF = 2816
N_MOD = 9
EPS = 1e-6
FORGET_BIAS_CENTER = 3.0

WIDTH_DSA = N_HEADS_DSA * HEAD_DIM
WIDTH_FOX = N_HEADS_FOX * HEAD_DIM
IN_SIZES = (WIDTH_DSA, WIDTH_DSA, WIDTH_DSA, WIDTH_FOX, WIDTH_FOX, WIDTH_FOX,
            N_HEADS_IDX * IDX_DIM, IDX_DIM, N_HEADS_IDX, N_HEADS_FOX, D_MODEL, D_MODEL)
IN_WIDTH = 3 * WIDTH_DSA + 3 * WIDTH_FOX + N_HEADS_IDX * IDX_DIM + IDX_DIM + N_HEADS_IDX + N_HEADS_FOX + 2 * D_MODEL

kernel_name = "hybrid_dsa_fox_macaron_adaln_layer"


def rms_norm(x, g):
    xf = x.astype(jnp.float32)
    y = xf * lax.rsqrt(jnp.mean(xf * xf, axis=-1, keepdims=True) + EPS)
    return (y * g.astype(jnp.float32)).astype(x.dtype)


def swiglu(h, wg, wu, wd):
    return (jax.nn.silu(h @ wg) * (h @ wu)) @ wd


def split_columns(z):
    parts, start = [], 0
    for width in IN_SIZES:
        parts.append(z[..., start:start + width])
        start += width
    return parts


def alibi_slopes(n):
    return jnp.exp2(-8.0 * jnp.arange(1, n + 1, dtype=jnp.float32) / n)


def to_blocks(a):
    b, s = a.shape[0], a.shape[1]
    return jnp.moveaxis(a.reshape(b, s // Q_BLOCK, Q_BLOCK, *a.shape[2:]), 1, 0)


def from_blocks(a):
    a = jnp.moveaxis(a, 0, 1)
    return a.reshape(a.shape[0], a.shape[1] * a.shape[2], *a.shape[3:])


def dsa_attention(q, k, v, q_idx, k_idx, w_idx):
    b, s, h, dh = q.shape
    n_keys = s
    topk = min(TOPK_MAX, n_keys // 4)
    slopes = alibi_slopes(h)
    s_pos = jnp.arange(s)
    k_idx32 = k_idx.astype(jnp.float32)

    def block_fn(args):
        blk, qb, qib, wib = args
        t_pos = blk * Q_BLOCK + jnp.arange(Q_BLOCK)
        rel = jax.nn.relu(jnp.einsum('btid,bsd->btis', qib.astype(jnp.float32), k_idx32))
        score = jnp.einsum('btis,bti->bts', rel, wib.astype(jnp.float32))
        causal = s_pos[None, :] <= t_pos[:, None]
        score = jnp.where(causal[None], score, -jnp.inf)
        _, sel = lax.top_k(score, topk)
        flat = sel.reshape(b, Q_BLOCK * topk)
        k_sel = jax.vmap(lambda kk, ii: kk[ii])(k, flat).reshape(b, Q_BLOCK, topk, h, dh)
        v_sel = jax.vmap(lambda vv, ii: vv[ii])(v, flat).reshape(b, Q_BLOCK, topk, h, dh)
        logits = jnp.einsum('bthd,btkhd->bthk', qb, k_sel).astype(jnp.float32)
        dist = (t_pos[None, :, None] - sel).astype(jnp.float32)
        logits = logits - slopes[None, None, :, None] * dist[:, :, None, :]
        valid = sel <= t_pos[None, :, None]
        logits = jnp.where(valid[:, :, None, :], logits, -jnp.inf)
        p = jax.nn.softmax(logits, axis=-1).astype(v.dtype)
        return jnp.einsum('bthk,btkhd->bthd', p, v_sel)

    nb = s // Q_BLOCK
    out = lax.map(block_fn, (jnp.arange(nb), to_blocks(q), to_blocks(q_idx), to_blocks(w_idx)))
    return from_blocks(out)


def fox_attention(q, k, v, log_f):
    b, s, h, dh = q.shape
    F = jnp.cumsum(log_f, axis=1)
    F_keys = jnp.swapaxes(F, 1, 2)
    s_pos = jnp.arange(s)

    def block_fn(args):
        blk, qb, Fb = args
        t_pos = blk * Q_BLOCK + jnp.arange(Q_BLOCK)
        logits = jnp.einsum('bthd,bshd->bhts', qb, k).astype(jnp.float32)
        logits = logits + jnp.swapaxes(Fb, 1, 2)[..., None] - F_keys[:, :, None, :]
        causal = s_pos[None, :] <= t_pos[:, None]
        logits = jnp.where(causal[None, None], logits, -jnp.inf)
        p = jax.nn.softmax(logits, axis=-1).astype(v.dtype)
        return jnp.einsum('bhts,bshd->bthd', p, v)

    nb = s // Q_BLOCK
    out = lax.map(block_fn, (jnp.arange(nb), to_blocks(q), to_blocks(F)))
    return from_blocks(out)


def token_mix(h, w_in, b_forget, qn_dsa, kn_dsa, qn_fox, kn_fox, w_br_dsa, w_br_fox, w_out):
    b, s, _ = h.shape
    z = h @ w_in
    (qa, ka, va, qf, kf, vf, qi, ki, wi, fpre, ga, gb) = split_columns(z)
    scale = HEAD_DIM ** -0.5
    qa = rms_norm(qa.reshape(b, s, N_HEADS_DSA, HEAD_DIM), qn_dsa) * scale
    ka = rms_norm(ka.reshape(b, s, N_HEADS_DSA, HEAD_DIM), kn_dsa)
    va = va.reshape(b, s, N_HEADS_DSA, HEAD_DIM)
    qf = rms_norm(qf.reshape(b, s, N_HEADS_FOX, HEAD_DIM), qn_fox) * scale
    kf = rms_norm(kf.reshape(b, s, N_HEADS_FOX, HEAD_DIM), kn_fox)
    vf = vf.reshape(b, s, N_HEADS_FOX, HEAD_DIM)
    qi = qi.reshape(b, s, N_HEADS_IDX, IDX_DIM) * (IDX_DIM ** -0.5)
    wi = wi * (N_HEADS_IDX ** -0.5)
    log_f = jax.nn.log_sigmoid(fpre.astype(jnp.float32) + b_forget.astype(jnp.float32))

    y_dsa = dsa_attention(qa, ka, va, qi, ki, wi).reshape(b, s, WIDTH_DSA) @ w_br_dsa
    y_fox = fox_attention(qf, kf, vf, log_f).reshape(b, s, WIDTH_FOX) @ w_br_fox
    merged = jax.nn.sigmoid(ga) * y_dsa + jax.nn.sigmoid(gb) * y_fox
    return merged @ w_out


def setup_inputs(seed: int = 0) -> dict:
    key = jax.random.key(seed)
    ks = jax.random.split(key, 24)
    f32 = jnp.float32

    def w(k, shape, fan_in, scale=1.0):
        return scale * (fan_in ** -0.5) * jax.random.normal(k, shape, f32)

    def gain(k, shape):
        return 1.0 + 0.02 * jax.random.normal(k, shape, f32)

    L = DEPTH
    return {
        "x": jax.random.normal(ks[0], (BATCH, SEQ, D_MODEL), f32),
        "c": jax.random.normal(ks[1], (BATCH, D_MODEL), f32),
        "ada_w": w(ks[2], (L, D_MODEL, N_MOD * D_MODEL), D_MODEL, 0.5),
        "ada_b": 0.02 * jax.random.normal(ks[3], (L, N_MOD * D_MODEL), f32),
        "norm1_g": gain(ks[4], (L, D_MODEL)),
        "ffn1_wg": w(ks[5], (L, D_MODEL, D_FF), D_MODEL),
        "ffn1_wu": w(ks[6], (L, D_MODEL, D_FF), D_MODEL),
        "ffn1_wd": w(ks[7], (L, D_FF, D_MODEL), D_FF),
        "norm2_g": gain(ks[8], (L, D_MODEL)),
        "w_in": w(ks[9], (L, D_MODEL, IN_WIDTH), D_MODEL),
        "b_forget": FORGET_BIAS_CENTER + 0.5 * jax.random.normal(ks[10], (L, N_HEADS_FOX), f32),
        "qn_dsa": gain(ks[11], (L, HEAD_DIM)),
        "kn_dsa": gain(ks[12], (L, HEAD_DIM)),
        "qn_fox": gain(ks[13], (L, HEAD_DIM)),
        "kn_fox": gain(ks[14], (L, HEAD_DIM)),
        "w_br_dsa": w(ks[15], (L, WIDTH_DSA, D_MODEL), WIDTH_DSA),
        "w_br_fox": w(ks[16], (L, WIDTH_FOX, D_MODEL), WIDTH_FOX),
        "w_out": w(ks[17], (L, D_MODEL, D_MODEL), D_MODEL),
        "norm3_g": gain(ks[18], (L, D_MODEL)),
        "ffn2_wg": w(ks[19], (L, D_MODEL, D_FF), D_MODEL),
        "ffn2_wu": w(ks[20], (L, D_MODEL, D_FF), D_MODEL),
        "ffn2_wd": w(ks[21], (L, D_FF, D_MODEL), D_FF),
    }


def reference(x, c, ada_w, ada_b, norm1_g, ffn1_wg, ffn1_wu, ffn1_wd, norm2_g, w_in, b_forget,
              qn_dsa, kn_dsa, qn_fox, kn_fox, w_br_dsa, w_br_fox, w_out, norm3_g,
              ffn2_wg, ffn2_wu, ffn2_wd):
    b = x.shape[0]
    for l in range(DEPTH):
        mod = (jax.nn.silu(c) @ ada_w[l] + ada_b[l]).reshape(b, N_MOD, 1, D_MODEL)
        sh1, sc1, g1 = mod[:, 0], mod[:, 1], mod[:, 2]
        sh2, sc2, g2 = mod[:, 3], mod[:, 4], mod[:, 5]
        sh3, sc3, g3 = mod[:, 6], mod[:, 7], mod[:, 8]
        h = rms_norm(x, norm1_g[l]) * (1.0 + sc1) + sh1
        x = x + 0.5 * g1 * swiglu(h, ffn1_wg[l], ffn1_wu[l], ffn1_wd[l])
        h = rms_norm(x, norm2_g[l]) * (1.0 + sc2) + sh2
        x = x + g2 * token_mix(h, w_in[l], b_forget[l], qn_dsa[l], kn_dsa[l], qn_fox[l], kn_fox[l],
                               w_br_dsa[l], w_br_fox[l], w_out[l])
        h = rms_norm(x, norm3_g[l]) * (1.0 + sc3) + sh3
        x = x + 0.5 * g3 * swiglu(h, ffn2_wg[l], ffn2_wu[l], ffn2_wd[l])
    return x
```

```python
import functools

import jax
import jax.numpy as jnp
from jax import lax
from jax.experimental import pallas as pl
from jax.experimental.pallas import tpu as pltpu

F32 = jnp.float32
BF16 = jnp.bfloat16
I32 = jnp.int32

HEAD_DIM = 64
N_HEADS = 8
WIDTH = N_HEADS * HEAD_DIM
TOPK = 256
EPS = 1e-6
N_MOD = 9
LANES = 128
NEG = -1e30
INT_MIN = -(2 ** 31)
VMEM_LIMIT = 56 * 1024 * 1024

SM_WI = 64
SM_F = 72


def _cparams(n_axes):
    return pltpu.CompilerParams(
        dimension_semantics=("arbitrary",) * n_axes, vmem_limit_bytes=VMEM_LIMIT)


def _const_spec(shape):
    nd = len(shape)
    return pl.BlockSpec(shape, lambda *_: (0,) * nd)


def _dot(a, b):
    return jnp.dot(a, b, preferred_element_type=F32)


def _rms_adaln(x, gain, scale, shift):
    y = x * lax.rsqrt(jnp.mean(x * x, axis=-1, keepdims=True) + EPS)
    return (y * gain) * (1.0 + scale) + shift


def _mod_kernel(c_ref, w_ref, b_ref, o_ref):
    c = c_ref[...]
    a = c * jax.nn.sigmoid(c)
    o_ref[...] = jnp.dot(a, w_ref[...], preferred_element_type=F32,
                         precision=lax.Precision.HIGHEST) + b_ref[...]


def _adaln_mod(c, ada_w, ada_b):
    b, d = c.shape
    n = ada_w.shape[1]
    rows = 8
    tn = n // 8
    c_pad = jnp.zeros((rows, d), F32).at[:b].set(c)
    out = pl.pallas_call(
        _mod_kernel,
        grid=(n // tn,),
        in_specs=[pl.BlockSpec((rows, d), lambda j: (0, 0)),
                  pl.BlockSpec((d, tn), lambda j: (0, j)),
                  pl.BlockSpec((1, tn), lambda j: (0, j))],
        out_specs=pl.BlockSpec((rows, tn), lambda j: (0, j)),
        out_shape=jax.ShapeDtypeStruct((rows, n), F32),
        compiler_params=_cparams(1),
        name="adaln_mod",
    )(c_pad, ada_w, ada_b.reshape(1, n))
    return out[:b].reshape(b, N_MOD, d)


def _ffn_kernel(x_ref, mod_ref, g_ref, wg_ref, wu_ref, wd_ref, o_ref, *, mod_base, n_chunks):
    x = x_ref[0]
    shift = mod_ref[0, mod_base:mod_base + 1, :]
    scale = mod_ref[0, mod_base + 1:mod_base + 2, :]
    gate = mod_ref[0, mod_base + 2:mod_base + 3, :]
    h = _rms_adaln(x, g_ref[...], scale, shift).astype(BF16)
    dff = wg_ref.shape[1]
    ck = dff // n_chunks
    acc = jnp.zeros(x.shape, F32)
    for i in range(n_chunks):
        g = _dot(h, wg_ref[:, i * ck:(i + 1) * ck])
        u = _dot(h, wu_ref[:, i * ck:(i + 1) * ck])
        a = (g * jax.nn.sigmoid(g) * u).astype(BF16)
        acc = acc + _dot(a, wd_ref[i * ck:(i + 1) * ck, :])
    o_ref[0] = x + (0.5 * gate) * acc


def _ffn(x, mod, gain, wg, wu, wd, *, mod_base, tm=512):
    b, s, d = x.shape
    dff = wg.shape[1]
    kern = functools.partial(_ffn_kernel, mod_base=mod_base, n_chunks=2)
    return pl.pallas_call(
        kern,
        grid=(b, s // tm),
        in_specs=[pl.BlockSpec((1, tm, d), lambda i, j: (i, j, 0)),
                  pl.BlockSpec((1, N_MOD, d), lambda i, j: (i, 0, 0)),
                  _const_spec((1, d)),
                  _const_spec((d, dff)), _const_spec((d, dff)), _const_spec((dff, d))],
        out_specs=pl.BlockSpec((1, tm, d), lambda i, j: (i, j, 0)),
        out_shape=jax.ShapeDtypeStruct((b, s, d), F32),
        compiler_params=_cparams(2),
        name="ffn",
    )(x, mod, gain.reshape(1, d), wg, wu, wd)


def _split3_bf16(v):
    p1 = v.astype(BF16)
    r1 = v - p1.astype(F32)
    p2 = r1.astype(BF16)
    r2 = r1 - p2.astype(F32)
    return p1, p2, r2.astype(BF16)


def _proj_kernel(x_ref, mod_ref, g_ref, wm_ref, ws_ref, bsm_ref, qg_ref, kg_ref,
                 qa_ref, kat_ref, va_ref, qf_ref, kft_ref, vf_ref, qi_ref,
                 small_ref, smallt_ref, kit_ref, carry_ref):
    tm = x_ref.shape[1]
    x = x_ref[0]
    shift = mod_ref[0, 3:4, :]
    scale = mod_ref[0, 4:5, :]
    h = _rms_adaln(x, g_ref[...], scale, shift).astype(BF16)

    def z(i):
        return _dot(h, wm_ref[:, i * WIDTH:(i + 1) * WIDTH])

    r = lax.broadcasted_iota(I32, (WIDTH, WIDTH), 0) // HEAD_DIM
    c = lax.broadcasted_iota(I32, (WIDTH, WIDTH), 1) // HEAD_DIM
    avg = jnp.where(r == c, 1.0 / HEAD_DIM, 0.0).astype(BF16)

    def norm_q(q, gain_row):
        ms = _dot((q * q).astype(BF16), avg)
        return (q * lax.rsqrt(ms + EPS) * gain_row).astype(BF16)

    def norm_kt(k, gain_col):
        kt = k.T.reshape(N_HEADS, HEAD_DIM, tm)
        ms = jnp.mean(kt * kt, axis=1, keepdims=True)
        kt = kt * lax.rsqrt(ms + EPS) * gain_col[None]
        return kt.reshape(WIDTH, tm).astype(BF16)

    qa_ref[0] = norm_q(z(0), qg_ref[0:1, :])
    kat_ref[0] = norm_kt(z(1), kg_ref[:, 0:1])
    va_ref[0] = z(2).astype(BF16)
    qf_ref[0] = norm_q(z(3), qg_ref[1:2, :])
    kft_ref[0] = norm_kt(z(4), kg_ref[:, 1:2])
    vf_ref[0] = z(5).astype(BF16)
    qi_ref[0] = (z(6) * (HEAD_DIM ** -0.5)).astype(BF16)

    zs = _dot(h, ws_ref[...])
    pre = zs + bsm_ref[...]
    logf = jnp.minimum(pre, 0.0) - jnp.log(1.0 + jnp.exp(-jnp.abs(pre)))

    @pl.when(pl.program_id(1) == 0)
    def _():
        carry_ref[...] = jnp.zeros_like(carry_ref)

    ri = lax.broadcasted_iota(I32, (tm, tm), 0)
    ci = lax.broadcasted_iota(I32, (tm, tm), 1)
    tri = jnp.where(ci <= ri, 1.0, 0.0).astype(BF16)
    p1, p2, p3 = _split3_bf16(logf)
    cum = (_dot(tri, p1) + _dot(tri, p2)) + _dot(tri, p3) + carry_ref[...]
    carry_ref[...] = cum[tm - 1:tm, :]

    lane = lax.broadcasted_iota(I32, (tm, LANES), 1)
    small = jnp.where(lane < SM_WI, zs,
                      jnp.where(lane < SM_F, zs * (N_HEADS ** -0.5), cum))
    small_ref[0] = small
    st = small.T
    smallt_ref[0] = st
    kit_ref[0] = st[0:HEAD_DIM, :].astype(BF16)


def _in_proj(x, mod, gain, w_main, w_small, b_small, q_gains, k_gains, *, tm=512):
    b, s, d = x.shape
    row = lambda i, j: (i, j, 0)
    col = lambda i, j: (i, 0, j)
    sd = jax.ShapeDtypeStruct
    out_shape = [sd((b, s, WIDTH), BF16), sd((b, WIDTH, s), BF16), sd((b, s, WIDTH), BF16),
                 sd((b, s, WIDTH), BF16), sd((b, WIDTH, s), BF16), sd((b, s, WIDTH), BF16),
                 sd((b, s, WIDTH), BF16),
                 sd((b, s, LANES), F32), sd((b, LANES, s), F32), sd((b, HEAD_DIM, s), BF16)]
    rspec = pl.BlockSpec((1, tm, WIDTH), row)
    cspec = pl.BlockSpec((1, WIDTH, tm), col)
    out_specs = [rspec, cspec, rspec, rspec, cspec, rspec, rspec,
                 pl.BlockSpec((1, tm, LANES), row), pl.BlockSpec((1, LANES, tm), col),
                 pl.BlockSpec((1, HEAD_DIM, tm), col)]
    return pl.pallas_call(
        _proj_kernel,
        grid=(b, s // tm),
        in_specs=[pl.BlockSpec((1, tm, d), row),
                  pl.BlockSpec((1, N_MOD, d), lambda i, j: (i, 0, 0)),
                  _const_spec((1, d)),
                  _const_spec(w_main.shape), _const_spec(w_small.shape),
                  _const_spec((1, LANES)), _const_spec((2, WIDTH)), _const_spec((HEAD_DIM, 2))],
        out_specs=out_specs,
        out_shape=out_shape,
        scratch_shapes=[pltpu.VMEM((1, LANES), F32)],
        compiler_params=_cparams(2),
        name="in_proj",
    )(x, mod, gain.reshape(1, d), w_main, w_small, b_small, q_gains, k_gains)


def _flash_head(q_h, kt_ref, v_ref, h, tk, kb_lo, kb_hi, logit_bias, carry):
    pair = h // 2

    def body(kb, carry):
        m, l, acc = carry
        off = pl.multiple_of(kb * tk, tk)
        kt = kt_ref[0, h * HEAD_DIM:(h + 1) * HEAD_DIM, pl.ds(off, tk)]
        s = _dot(q_h, kt) + logit_bias(off)
        m_new = jnp.maximum(m, jnp.max(s, axis=1, keepdims=True))
        alpha = jnp.exp(m - m_new)
        p = jnp.exp(s - m_new)
        l = alpha * l + jnp.sum(p, axis=1, keepdims=True)
        v = v_ref[0, pl.ds(off, tk), pair * LANES:(pair + 1) * LANES]
        acc = alpha * acc + _dot(p.astype(BF16), v)
        return m_new, l, acc

    return lax.fori_loop(kb_lo, kb_hi, body, carry)


def _flash_init(tq):
    return (jnp.full((tq, 1), NEG, F32), jnp.zeros((tq, 1), F32), jnp.zeros((tq, LANES), F32))


def _merge_pair(even, odd):
    (_, l0, a0), (_, l1, a1) = even, odd
    lane = lax.broadcasted_iota(I32, a0.shape, 1)
    return jnp.where(lane < HEAD_DIM, a0 / l0, a1 / l1)


def _dsa_kernel(qa_ref, qi_ref, small_ref, kat_ref, va_ref, kit_ref, o_ref,
                key_ref, bias_ref, icut_ref, *, tq, tk):
    s_len = kat_ref.shape[2]
    t0 = pl.program_id(1) * tq
    nkb = (t0 + tq + tk - 1) // tk
    row = lax.broadcasted_iota(I32, (tq, tk), 0) + t0
    col = lax.broadcasted_iota(I32, (tq, tk), 1)
    n_sub = tk // LANES

    wi = small_ref[0][:, SM_WI:SM_WI + N_HEADS]
    qi = qi_ref[0]

    def score_body(kb, _):
        off = pl.multiple_of(kb * tk, tk)
        kib = kit_ref[0, :, pl.ds(off, tk)]
        acc = jnp.zeros((tq, tk), F32)
        for h in range(N_HEADS):
            r = _dot(qi[:, h * HEAD_DIM:(h + 1) * HEAD_DIM], kib)
            acc = acc + jnp.maximum(r, 0.0) * wi[:, h:h + 1]
        acc = jnp.where(col + off <= row, acc, -jnp.inf)
        bits = pltpu.bitcast(acc, I32)
        key_ref[:, pl.ds(off, tk)] = bits ^ ((bits >> 31) & 0x7FFFFFFF)
        return 0

    lax.fori_loop(0, nkb, score_body, 0)

    def count(pred):
        def body(kb, c):
            off = pl.multiple_of(kb * tk, tk)
            hit = jnp.where(pred(key_ref[:, pl.ds(off, tk)], off), 1.0, 0.0)
            part = hit[:, 0:LANES]
            for i in range(1, n_sub):
                part = part + hit[:, i * LANES:(i + 1) * LANES]
            return c + part
        c = lax.fori_loop(0, nkb, body, jnp.zeros((tq, LANES), F32))
        return jnp.sum(c, axis=1, keepdims=True)

    def bit_body(i, t_u):
        cand_u = t_u | jnp.left_shift(jnp.int32(1), 31 - i)
        cand = cand_u ^ INT_MIN
        cnt = count(lambda k, off: k >= cand)
        return jnp.where(cnt >= TOPK, cand_u, t_u)

    thr = lax.fori_loop(0, 32, bit_body, jnp.zeros((tq, 1), I32)) ^ INT_MIN

    c_gt = count(lambda k, off: k > thr)
    c_eq = count(lambda k, off: k == thr)
    need = TOPK - c_gt
    icut_ref[...] = jnp.full(icut_ref.shape, s_len, I32)

    @pl.when(jnp.max(c_eq - need) > 0.0)
    def _():
        n_bits = max(1, (s_len - 1).bit_length())

        def idx_body(i, cut):
            cand = cut | jnp.left_shift(jnp.int32(1), n_bits - 1 - i)
            cnt = count(lambda k, off: (k == thr) & (col + off < cand))
            return jnp.where(cnt < need, cand, cut)

        cut = lax.fori_loop(0, n_bits, idx_body, jnp.zeros((tq, 1), I32))
        icut_ref[...] = jnp.broadcast_to(cut, icut_ref.shape)

    icut = icut_ref[:, 0:1]

    def bias_body(kb, _):
        off = pl.multiple_of(kb * tk, tk)
        k = key_ref[:, pl.ds(off, tk)]
        spos = col + off
        sel = (k > thr) | ((k == thr) & (spos <= icut))
        bias_ref[:, pl.ds(off, tk)] = jnp.where(sel & (spos <= row), 0.0, NEG)
        return 0

    lax.fori_loop(0, nkb, bias_body, 0)

    qa = qa_ref[0]
    kpos = lax.broadcasted_iota(I32, (1, tk), 1)
    for pair in range(N_HEADS // 2):
        res = []
        for h in (2 * pair, 2 * pair + 1):
            slope = 2.0 ** (-8.0 * (h + 1) / N_HEADS)

            def logit_bias(off, slope=slope):
                return bias_ref[:, pl.ds(off, tk)] + slope * (kpos + (off - t0)).astype(F32)

            res.append(_flash_head(qa[:, h * HEAD_DIM:(h + 1) * HEAD_DIM], kat_ref, va_ref, h, tk,
                                   0, nkb, logit_bias, _flash_init(tq)))
        o_ref[0, :, pair * LANES:(pair + 1) * LANES] = _merge_pair(*res).astype(BF16)


def _dsa_attn(qa, qi, small, kat, va, kit, *, tq=128, tk=512):
    b, s, _ = qa.shape
    row = lambda i, j: (i, j, 0)
    whole = lambda i, j: (i, 0, 0)
    kern = functools.partial(_dsa_kernel, tq=tq, tk=tk)
    return pl.pallas_call(
        kern,
        grid=(b, s // tq),
        in_specs=[pl.BlockSpec((1, tq, WIDTH), row), pl.BlockSpec((1, tq, WIDTH), row),
                  pl.BlockSpec((1, tq, LANES), row),
                  pl.BlockSpec((1, WIDTH, s), whole), pl.BlockSpec((1, s, WIDTH), whole),
                  pl.BlockSpec((1, HEAD_DIM, s), whole)],
        out_specs=pl.BlockSpec((1, tq, WIDTH), row),
        out_shape=jax.ShapeDtypeStruct((b, s, WIDTH), BF16),
        scratch_shapes=[pltpu.VMEM((tq, s), I32), pltpu.VMEM((tq, s), F32),
                        pltpu.VMEM((tq, LANES), I32)],
        compiler_params=_cparams(2),
        name="dsa_attn",
    )(qa, qi, small, kat, va, kit)


def _fox_kernel(qf_ref, smallt_ref, kft_ref, vf_ref, o_ref, *, tq, tk):
    t0 = pl.program_id(1) * tq
    n_full = t0 // tk
    nkb = (t0 + tq + tk - 1) // tk
    row = lax.broadcasted_iota(I32, (tq, tk), 0) + t0
    col = lax.broadcasted_iota(I32, (tq, tk), 1)
    qf = qf_ref[0]
    f_t0 = smallt_ref[0, SM_F:SM_F + N_HEADS, pl.ds(pl.multiple_of(t0, LANES), LANES)][:, 0:1]
    for pair in range(N_HEADS // 2):
        res = []
        for h in (2 * pair, 2 * pair + 1):
            f0 = f_t0[h:h + 1, :]

            def decay(off, h=h, f0=f0):
                return f0 - smallt_ref[0, SM_F + h:SM_F + h + 1, pl.ds(off, tk)]

            def decay_causal(off, h=h, f0=f0):
                return jnp.where(col + off <= row, decay(off, h, f0), NEG)

            q_h = qf[:, h * HEAD_DIM:(h + 1) * HEAD_DIM]
            carry = _flash_head(q_h, kft_ref, vf_ref, h, tk, 0, n_full, decay, _flash_init(tq))
            res.append(_flash_head(q_h, kft_ref, vf_ref, h, tk, n_full, nkb, decay_causal, carry))
        o_ref[0, :, pair * LANES:(pair + 1) * LANES] = _merge_pair(*res).astype(BF16)


def _fox_attn(qf, smallt, kft, vf, *, tq=128, tk=512):
    b, s, _ = qf.shape
    row = lambda i, j: (i, j, 0)
    whole = lambda i, j: (i, 0, 0)
    kern = functools.partial(_fox_kernel, tq=tq, tk=tk)
    return pl.pallas_call(
        kern,
        grid=(b, s // tq),
        in_specs=[pl.BlockSpec((1, tq, WIDTH), row), pl.BlockSpec((1, LANES, s), whole),
                  pl.BlockSpec((1, WIDTH, s), whole), pl.BlockSpec((1, s, WIDTH), whole)],
        out_specs=pl.BlockSpec((1, tq, WIDTH), row),
        out_shape=jax.ShapeDtypeStruct((b, s, WIDTH), BF16),
        compiler_params=_cparams(2),
        name="fox_attn",
    )(qf, smallt, kft, vf)


def _mix_kernel(x_ref, mod_ref, g_ref, ad_ref, af_ref, wgate_ref, wbd_ref, wbf_ref, wo_ref, o_ref):
    d = x_ref.shape[2]
    x = x_ref[0]
    shift = mod_ref[0, 3:4, :]
    scale = mod_ref[0, 4:5, :]
    gate = mod_ref[0, 5:6, :]
    h = _rms_adaln(x, g_ref[...], scale, shift).astype(BF16)
    ga = _dot(h, wgate_ref[:, 0:d])
    gb = _dot(h, wgate_ref[:, d:2 * d])
    y_dsa = _dot(ad_ref[0], wbd_ref[...])
    y_fox = _dot(af_ref[0], wbf_ref[...])
    merged = jax.nn.sigmoid(ga) * y_dsa + jax.nn.sigmoid(gb) * y_fox
    o_ref[0] = x + gate * _dot(merged.astype(BF16), wo_ref[...])


def _mix_out(x, mod, gain, a_dsa, a_fox, w_gate, w_br_dsa, w_br_fox, w_out, *, tm=512):
    b, s, d = x.shape
    row = lambda i, j: (i, j, 0)
    return pl.pallas_call(
        _mix_kernel,
        grid=(b, s // tm),
        in_specs=[pl.BlockSpec((1, tm, d), row),
                  pl.BlockSpec((1, N_MOD, d), lambda i, j: (i, 0, 0)),
                  _const_spec((1, d)),
                  pl.BlockSpec((1, tm, WIDTH), row), pl.BlockSpec((1, tm, WIDTH), row),
                  _const_spec(w_gate.shape), _const_spec(w_br_dsa.shape),
                  _const_spec(w_br_fox.shape), _const_spec(w_out.shape)],
        out_specs=pl.BlockSpec((1, tm, d), row),
        out_shape=jax.ShapeDtypeStruct((b, s, d), F32),
        compiler_params=_cparams(2),
        name="mix_out",
    )(x, mod, gain.reshape(1, d), a_dsa, a_fox, w_gate, w_br_dsa, w_br_fox, w_out)


def _layer(x, mod, norm1_g, ffn1_wg, ffn1_wu, ffn1_wd, norm2_g, w_in, b_forget,
           qn_dsa, kn_dsa, qn_fox, kn_fox, w_br_dsa, w_br_fox, w_out, norm3_g,
           ffn2_wg, ffn2_wu, ffn2_wd):
    d = x.shape[2]
    bf = lambda w: w.astype(BF16)
    x = _ffn(x, mod, norm1_g, bf(ffn1_wg), bf(ffn1_wu), bf(ffn1_wd), mod_base=0)

    n_main = 7 * WIDTH
    n_small = HEAD_DIM + 2 * N_HEADS
    w_main = bf(w_in[:, :n_main])
    w_small = jnp.zeros((d, LANES), BF16).at[:, :n_small].set(bf(w_in[:, n_main:n_main + n_small]))
    w_gate = bf(w_in[:, n_main + n_small:])
    b_small = jnp.zeros((1, LANES), F32).at[0, SM_F:SM_F + N_HEADS].set(b_forget)
    q_scale = HEAD_DIM ** -0.5
    q_gains = jnp.stack([jnp.tile(qn_dsa, N_HEADS), jnp.tile(qn_fox, N_HEADS)]) * q_scale
    k_gains = jnp.stack([kn_dsa, kn_fox], axis=1)

    qa, kat, va, qf, kft, vf, qi, small, smallt, kit = _in_proj(
        x, mod, norm2_g, w_main, w_small, b_small, q_gains, k_gains)
    a_dsa = _dsa_attn(qa, qi, small, kat, va, kit)
    a_fox = _fox_attn(qf, smallt, kft, vf)
    x = _mix_out(x, mod, norm2_g, a_dsa, a_fox, w_gate, bf(w_br_dsa), bf(w_br_fox), bf(w_out))
    return _ffn(x, mod, norm3_g, bf(ffn2_wg), bf(ffn2_wu), bf(ffn2_wd), mod_base=6)


def kernel(x, c, ada_w, ada_b, norm1_g, ffn1_wg, ffn1_wu, ffn1_wd, norm2_g, w_in, b_forget,
           qn_dsa, kn_dsa, qn_fox, kn_fox, w_br_dsa, w_br_fox, w_out, norm3_g,
           ffn2_wg, ffn2_wu, ffn2_wd):
    per_layer = (norm1_g, ffn1_wg, ffn1_wu, ffn1_wd, norm2_g, w_in, b_forget,
                 qn_dsa, kn_dsa, qn_fox, kn_fox, w_br_dsa, w_br_fox, w_out, norm3_g,
                 ffn2_wg, ffn2_wu, ffn2_wd)
    for l in range(ada_w.shape[0]):
        mod = _adaln_mod(c, ada_w[l], ada_b[l])
        x = _layer(x, mod, *(p[l] for p in per_layer))
    return x
```

```python
import functools

import jax
import jax.numpy as jnp
from jax import lax
from jax.experimental import pallas as pl
from jax.experimental.pallas import tpu as pltpu

F32 = jnp.float32
BF16 = jnp.bfloat16
I32 = jnp.int32

HEAD_DIM = 64
N_HEADS = 8
WIDTH = N_HEADS * HEAD_DIM
TOPK = 256
EPS = 1e-6
N_MOD = 9
LANES = 128
NEG = -1e30
BISECT_STEPS = 24
VMEM_LIMIT = 56 * 1024 * 1024

SM_WI = 64
SM_F = 72


def _cparams(n_axes):
    return pltpu.CompilerParams(
        dimension_semantics=("arbitrary",) * n_axes, vmem_limit_bytes=VMEM_LIMIT)


def _const_spec(shape):
    nd = len(shape)
    return pl.BlockSpec(shape, lambda *_: (0,) * nd)


def _dot(a, b):
    return jnp.dot(a, b, preferred_element_type=F32)


def _rms_adaln(x, gain, scale, shift):
    y = x * lax.rsqrt(jnp.mean(x * x, axis=-1, keepdims=True) + EPS)
    return (y * gain) * (1.0 + scale) + shift


def _mod_kernel(c_ref, w_ref, b_ref, o_ref):
    c = c_ref[...]
    a = c * jax.nn.sigmoid(c)
    o_ref[...] = jnp.dot(a, w_ref[...], preferred_element_type=F32,
                         precision=lax.Precision.HIGHEST) + b_ref[...]


def _adaln_mod(c, ada_w, ada_b):
    b, d = c.shape
    n = ada_w.shape[1]
    rows = 8
    tn = n // 8
    c_pad = jnp.zeros((rows, d), F32).at[:b].set(c)
    out = pl.pallas_call(
        _mod_kernel,
        grid=(n // tn,),
        in_specs=[pl.BlockSpec((rows, d), lambda j: (0, 0)),
                  pl.BlockSpec((d, tn), lambda j: (0, j)),
                  pl.BlockSpec((1, tn), lambda j: (0, j))],
        out_specs=pl.BlockSpec((rows, tn), lambda j: (0, j)),
        out_shape=jax.ShapeDtypeStruct((rows, n), F32),
        compiler_params=_cparams(1),
        name="adaln_mod",
    )(c_pad, ada_w, ada_b.reshape(1, n))
    return out[:b].reshape(b, N_MOD, d)


def _ffn_kernel(x_ref, mod_ref, g_ref, wg_ref, wu_ref, wd_ref, o_ref, *, mod_base, n_chunks):
    x = x_ref[0]
    shift = mod_ref[0, mod_base:mod_base + 1, :]
    scale = mod_ref[0, mod_base + 1:mod_base + 2, :]
    gate = mod_ref[0, mod_base + 2:mod_base + 3, :]
    h = _rms_adaln(x, g_ref[...], scale, shift).astype(BF16)
    dff = wg_ref.shape[1]
    ck = dff // n_chunks
    acc = jnp.zeros(x.shape, F32)
    for i in range(n_chunks):
        g = _dot(h, wg_ref[:, i * ck:(i + 1) * ck])
        u = _dot(h, wu_ref[:, i * ck:(i + 1) * ck])
        a = (g * jax.nn.sigmoid(g) * u).astype(BF16)
        acc = acc + _dot(a, wd_ref[i * ck:(i + 1) * ck, :])
    o_ref[0] = x + (0.5 * gate) * acc


def _ffn(x, mod, gain, wg, wu, wd, *, mod_base, tm=512):
    b, s, d = x.shape
    dff = wg.shape[1]
    kern = functools.partial(_ffn_kernel, mod_base=mod_base, n_chunks=2)
    return pl.pallas_call(
        kern,
        grid=(b, s // tm),
        in_specs=[pl.BlockSpec((1, tm, d), lambda i, j: (i, j, 0)),
                  pl.BlockSpec((1, N_MOD, d), lambda i, j: (i, 0, 0)),
                  _const_spec((1, d)),
                  _const_spec((d, dff)), _const_spec((d, dff)), _const_spec((dff, d))],
        out_specs=pl.BlockSpec((1, tm, d), lambda i, j: (i, j, 0)),
        out_shape=jax.ShapeDtypeStruct((b, s, d), F32),
        compiler_params=_cparams(2),
        name="ffn",
    )(x, mod, gain.reshape(1, d), wg, wu, wd)


def _split3_bf16(v):
    p1 = v.astype(BF16)
    r1 = v - p1.astype(F32)
    p2 = r1.astype(BF16)
    r2 = r1 - p2.astype(F32)
    return p1, p2, r2.astype(BF16)


def _proj_kernel(x_ref, mod_ref, g_ref, wm_ref, ws_ref, bsm_ref, qg_ref, kg_ref,
                 qa_ref, kat_ref, va_ref, qf_ref, kft_ref, vf_ref, qi_ref,
                 small_ref, smallt_ref, kit_ref, carry_ref):
    tm = x_ref.shape[1]
    x = x_ref[0]
    shift = mod_ref[0, 3:4, :]
    scale = mod_ref[0, 4:5, :]
    h = _rms_adaln(x, g_ref[...], scale, shift).astype(BF16)

    def z(i):
        return _dot(h, wm_ref[:, i * WIDTH:(i + 1) * WIDTH])

    r = lax.broadcasted_iota(I32, (WIDTH, WIDTH), 0) // HEAD_DIM
    c = lax.broadcasted_iota(I32, (WIDTH, WIDTH), 1) // HEAD_DIM
    avg = jnp.where(r == c, 1.0 / HEAD_DIM, 0.0).astype(BF16)

    def norm_q(q, gain_row):
        ms = _dot((q * q).astype(BF16), avg)
        return (q * lax.rsqrt(ms + EPS) * gain_row).astype(BF16)

    def norm_kt(k, gain_col):
        kt = k.T.reshape(N_HEADS, HEAD_DIM, tm)
        ms = jnp.mean(kt * kt, axis=1, keepdims=True)
        kt = kt * lax.rsqrt(ms + EPS) * gain_col[None]
        return kt.reshape(WIDTH, tm).astype(BF16)

    qa_ref[0] = norm_q(z(0), qg_ref[0:1, :])
    kat_ref[0] = norm_kt(z(1), kg_ref[:, 0:1])
    va_ref[0] = z(2).astype(BF16)
    qf_ref[0] = norm_q(z(3), qg_ref[1:2, :])
    kft_ref[0] = norm_kt(z(4), kg_ref[:, 1:2])
    vf_ref[0] = z(5).astype(BF16)
    qi_ref[0] = z(6) * (HEAD_DIM ** -0.5)

    zs = _dot(h, ws_ref[...])
    pre = zs + bsm_ref[...]
    logf = jnp.minimum(pre, 0.0) - jnp.log(1.0 + jnp.exp(-jnp.abs(pre)))

    @pl.when(pl.program_id(1) == 0)
    def _():
        carry_ref[...] = jnp.zeros_like(carry_ref)

    ri = lax.broadcasted_iota(I32, (tm, tm), 0)
    ci = lax.broadcasted_iota(I32, (tm, tm), 1)
    tri = jnp.where(ci <= ri, 1.0, 0.0).astype(BF16)
    p1, p2, p3 = _split3_bf16(logf)
    cum = (_dot(tri, p1) + _dot(tri, p2)) + _dot(tri, p3) + carry_ref[...]
    carry_ref[...] = cum[tm - 1:tm, :]

    lane = lax.broadcasted_iota(I32, (tm, LANES), 1)
    small = jnp.where(lane < SM_WI, zs,
                      jnp.where(lane < SM_F, zs * (N_HEADS ** -0.5), cum))
    small_ref[0] = small
    st = small.T
    smallt_ref[0] = st
    ki = st[0:HEAD_DIM, :]
    ki_hi = ki.astype(BF16)
    ki_lo = (ki - ki_hi.astype(F32)).astype(BF16)
    kit_ref[0] = jnp.concatenate([ki_hi, ki_hi, ki_lo], axis=0)


def _in_proj(x, mod, gain, w_main, w_small, b_small, q_gains, k_gains, *, tm=512):
    b, s, d = x.shape
    row = lambda i, j: (i, j, 0)
    col = lambda i, j: (i, 0, j)
    sd = jax.ShapeDtypeStruct
    out_shape = [sd((b, s, WIDTH), BF16), sd((b, WIDTH, s), BF16), sd((b, s, WIDTH), BF16),
                 sd((b, s, WIDTH), BF16), sd((b, WIDTH, s), BF16), sd((b, s, WIDTH), BF16),
                 sd((b, s, WIDTH), F32),
                 sd((b, s, LANES), F32), sd((b, LANES, s), F32), sd((b, 3 * HEAD_DIM, s), BF16)]
    rspec = pl.BlockSpec((1, tm, WIDTH), row)
    cspec = pl.BlockSpec((1, WIDTH, tm), col)
    out_specs = [rspec, cspec, rspec, rspec, cspec, rspec, rspec,
                 pl.BlockSpec((1, tm, LANES), row), pl.BlockSpec((1, LANES, tm), col),
                 pl.BlockSpec((1, 3 * HEAD_DIM, tm), col)]
    return pl.pallas_call(
        _proj_kernel,
        grid=(b, s // tm),
        in_specs=[pl.BlockSpec((1, tm, d), row),
                  pl.BlockSpec((1, N_MOD, d), lambda i, j: (i, 0, 0)),
                  _const_spec((1, d)),
                  _const_spec(w_main.shape), _const_spec(w_small.shape),
                  _const_spec((1, LANES)), _const_spec((2, WIDTH)), _const_spec((HEAD_DIM, 2))],
        out_specs=out_specs,
        out_shape=out_shape,
        scratch_shapes=[pltpu.VMEM((1, LANES), F32)],
        compiler_params=_cparams(2),
        name="in_proj",
    )(x, mod, gain.reshape(1, d), w_main, w_small, b_small, q_gains, k_gains)


def _flash_pair(q_pair, kt_ref, v_ref, pair, tk, kb_lo, kb_hi, logit_bias, carry):
    def body(kb, carry):
        off = pl.multiple_of(kb * tk, tk)
        v = v_ref[0, pl.ds(off, tk), pair * LANES:(pair + 1) * LANES]
        out = []
        for i, (m, l, acc) in enumerate(carry):
            h = 2 * pair + i
            kt = kt_ref[0, h * HEAD_DIM:(h + 1) * HEAD_DIM, pl.ds(off, tk)]
            s = _dot(q_pair[i], kt) + logit_bias(off, i)
            m_new = jnp.maximum(m, jnp.max(s, axis=1, keepdims=True))
            alpha = jnp.exp(m - m_new)
            p = jnp.exp(s - m_new)
            l = alpha * l + jnp.sum(p, axis=1, keepdims=True)
            acc = alpha * acc + _dot(p.astype(BF16), v)
            out.append((m_new, l, acc))
        return tuple(out)

    return lax.fori_loop(kb_lo, kb_hi, body, carry)


def _flash_init(tq):
    one = (jnp.full((tq, 1), NEG, F32), jnp.zeros((tq, 1), F32), jnp.zeros((tq, LANES), F32))
    return (one, one)


def _merge_pair(carry):
    (_, l0, a0), (_, l1, a1) = carry
    lane = lax.broadcasted_iota(I32, a0.shape, 1)
    return jnp.where(lane < HEAD_DIM, a0 / l0, a1 / l1)


def _q_pair(q, pair):
    return [q[:, (2 * pair + i) * HEAD_DIM:(2 * pair + i + 1) * HEAD_DIM] for i in range(2)]


def _dsa_kernel(qa_ref, qi_ref, small_ref, kat_ref, va_ref, kit_ref, o_ref, sc_ref, icut_ref,
                *, tq, tk, topk):
    s_len = kat_ref.shape[2]
    t0 = pl.program_id(1) * tq
    nkb = (t0 + tq + tk - 1) // tk
    n_sub = tk // LANES
    row = lax.broadcasted_iota(I32, (tq, LANES), 0) + t0
    lane = lax.broadcasted_iota(I32, (tq, LANES), 1)
    rep = lambda col: jnp.broadcast_to(col, (tq, LANES))

    wi = small_ref[0][:, SM_WI:SM_WI + N_HEADS]
    qi = qi_ref[0]
    lhs = []
    for h in range(N_HEADS):
        qh = qi[:, h * HEAD_DIM:(h + 1) * HEAD_DIM]
        hi = qh.astype(BF16)
        lo = (qh - hi.astype(F32)).astype(BF16)
        lhs.append(jnp.concatenate([hi, lo, hi], axis=1))

    def score_body(kb, carry):
        mn, mx = carry
        off = pl.multiple_of(kb * tk, tk)
        kib = kit_ref[0, :, pl.ds(off, tk)]
        acc = jnp.zeros((tq, tk), F32)
        for h in range(N_HEADS):
            acc = acc + jnp.maximum(_dot(lhs[h], kib), 0.0) * wi[:, h:h + 1]
        for i in range(n_sub):
            a = acc[:, i * LANES:(i + 1) * LANES]
            causal = lane + (off + i * LANES) <= row
            lowered = jnp.where(causal, a, -jnp.inf)
            sc_ref[:, pl.ds(off + i * LANES, LANES)] = lowered
            mn = jnp.minimum(mn, jnp.where(causal, a, jnp.inf))
            mx = jnp.maximum(mx, lowered)
        return mn, mx

    mn, mx = lax.fori_loop(0, nkb, score_body, (jnp.full((tq, LANES), jnp.inf, F32),
                                                jnp.full((tq, LANES), -jnp.inf, F32)))

    def count(pred):
        def body(kb, c):
            off = pl.multiple_of(kb * tk, tk)
            for i in range(n_sub):
                blk = sc_ref[:, pl.ds(off + i * LANES, LANES)]
                c = c + jnp.where(pred(blk, lane + (off + i * LANES)), 1.0, 0.0)
            return c
        c = lax.fori_loop(0, nkb, body, jnp.zeros((tq, LANES), F32))
        return rep(jnp.sum(c, axis=1, keepdims=True))

    def below_max(bound):
        def body(kb, m):
            off = pl.multiple_of(kb * tk, tk)
            for i in range(n_sub):
                blk = sc_ref[:, pl.ds(off + i * LANES, LANES)]
                m = jnp.maximum(m, jnp.where(blk < bound, blk, -jnp.inf))
            return m
        m = lax.fori_loop(0, nkb, body, jnp.full((tq, LANES), -jnp.inf, F32))
        return rep(jnp.max(m, axis=1, keepdims=True))

    kf = float(topk)
    few = row < topk
    rmin = rep(jnp.min(mn, axis=1, keepdims=True))
    rmax = rep(jnp.max(mx, axis=1, keepdims=True))
    above = jnp.where(rmax > 0.0, 2.0 * rmax, 0.5 * rmax) + 1.0
    lo0 = jnp.where(few, 0.0, rmin)
    hi0 = jnp.where(few, 0.0, above)
    state0 = jnp.where(few, 1.0, 0.0)

    def search_cond(st):
        it, pending = st[0], st[1]
        return (it < BISECT_STEPS) & (pending < 0.5)

    def search_body(st):
        it, _, lo, hi, state = st
        mid = 0.5 * lo + 0.5 * hi
        cnt = count(lambda blk, pos: blk >= mid)
        active = state == 0.0
        ge = cnt >= kf
        lo = jnp.where(active & ge, mid, lo)
        hi = jnp.where(active & jnp.logical_not(ge), mid, hi)
        state = jnp.where(active & (cnt == kf), 1.0, state)
        return it + 1, jnp.min(state), lo, hi, state

    _, pending, lo, hi, state = lax.while_loop(
        search_cond, search_body, (jnp.int32(0), jnp.min(state0), lo0, hi0, state0))
    thr = jnp.where(few, -jnp.inf, lo)

    def snap_body(st):
        _, hi, thr, state = st
        cand = below_max(hi)
        cnt = count(lambda blk, pos: blk >= cand)
        active = state == 0.0
        found = active & (cnt >= kf)
        thr = jnp.where(found, cand, thr)
        hi = jnp.where(active & jnp.logical_not(found), cand, hi)
        state = jnp.where(found, 2.0, state)
        return jnp.min(state), hi, thr, state

    _, _, thr, state = lax.while_loop(lambda st: st[0] < 0.5, snap_body, (pending, hi, thr, state))

    icut_ref[...] = jnp.full(icut_ref.shape, s_len, I32)
    tied = state == 2.0

    @pl.when(jnp.max(state) > 1.5)
    def _():
        need = kf - count(lambda blk, pos: blk > thr)
        n_bits = max(1, (s_len - 1).bit_length())

        def idx_body(i, cut):
            cand = cut | jnp.left_shift(jnp.int32(1), n_bits - 1 - i)
            cnt = count(lambda blk, pos: (blk == thr) & (pos < cand))
            return jnp.where(cnt < need, cand, cut)

        cut = lax.fori_loop(0, n_bits, idx_body, jnp.zeros((tq, LANES), I32))
        icut_ref[...] = jnp.where(tied, cut, s_len)

    icut = icut_ref[...]

    def bias_body(kb, _):
        off = pl.multiple_of(kb * tk, tk)
        for i in range(n_sub):
            sl = pl.ds(off + i * LANES, LANES)
            blk = sc_ref[:, sl]
            pos = lane + (off + i * LANES)
            sel = (blk > thr) | ((blk == thr) & (pos <= icut))
            sc_ref[:, sl] = jnp.where(sel & (pos <= row), 0.0, NEG)
        return 0

    lax.fori_loop(0, nkb, bias_body, 0)

    qa = qa_ref[0]
    kpos = lax.broadcasted_iota(I32, (1, tk), 1)
    for pair in range(N_HEADS // 2):
        slopes = [2.0 ** (-8.0 * (2 * pair + i + 1) / N_HEADS) for i in range(2)]

        def logit_bias(off, i, slopes=slopes):
            return sc_ref[:, pl.ds(off, tk)] + slopes[i] * (kpos + (off - t0)).astype(F32)

        carry = _flash_pair(_q_pair(qa, pair), kat_ref, va_ref, pair, tk, 0, nkb, logit_bias,
                            _flash_init(tq))
        o_ref[0, :, pair * LANES:(pair + 1) * LANES] = _merge_pair(carry).astype(BF16)


def _dsa_attn(qa, qi, small, kat, va, kit, *, tq=256, tk=512):
    b, s, _ = qa.shape
    row = lambda i, j: (i, j, 0)
    whole = lambda i, j: (i, 0, 0)
    once = pl.Buffered(1)
    kern = functools.partial(_dsa_kernel, tq=tq, tk=tk, topk=min(TOPK, s // 4))
    return pl.pallas_call(
        kern,
        grid=(b, s // tq),
        in_specs=[pl.BlockSpec((1, tq, WIDTH), row), pl.BlockSpec((1, tq, WIDTH), row),
                  pl.BlockSpec((1, tq, LANES), row),
                  pl.BlockSpec((1, WIDTH, s), whole, pipeline_mode=once),
                  pl.BlockSpec((1, s, WIDTH), whole, pipeline_mode=once),
                  pl.BlockSpec((1, 3 * HEAD_DIM, s), whole, pipeline_mode=once)],
        out_specs=pl.BlockSpec((1, tq, WIDTH), row),
        out_shape=jax.ShapeDtypeStruct((b, s, WIDTH), BF16),
        scratch_shapes=[pltpu.VMEM((tq, s), F32), pltpu.VMEM((tq, LANES), I32)],
        compiler_params=_cparams(2),
        name="dsa_attn",
    )(qa, qi, small, kat, va, kit)


def _fox_kernel(qf_ref, smallt_ref, kft_ref, vf_ref, o_ref, *, tq, tk):
    t0 = pl.program_id(1) * tq
    n_full = t0 // tk
    nkb = (t0 + tq + tk - 1) // tk
    row = lax.broadcasted_iota(I32, (tq, tk), 0) + t0
    col = lax.broadcasted_iota(I32, (tq, tk), 1)
    qf = qf_ref[0]
    f_t0 = smallt_ref[0, SM_F:SM_F + N_HEADS, pl.ds(pl.multiple_of(t0, LANES), LANES)][:, 0:1]
    for pair in range(N_HEADS // 2):
        def decay(off, i, pair=pair):
            h = 2 * pair + i
            return f_t0[h:h + 1, :] - smallt_ref[0, SM_F + h:SM_F + h + 1, pl.ds(off, tk)]

        def decay_causal(off, i, decay=decay):
            return jnp.where(col + off <= row, decay(off, i), NEG)

        q_pair = _q_pair(qf, pair)
        carry = _flash_pair(q_pair, kft_ref, vf_ref, pair, tk, 0, n_full, decay, _flash_init(tq))
        carry = _flash_pair(q_pair, kft_ref, vf_ref, pair, tk, n_full, nkb, decay_causal, carry)
        o_ref[0, :, pair * LANES:(pair + 1) * LANES] = _merge_pair(carry).astype(BF16)


def _fox_attn(qf, smallt, kft, vf, *, tq=512, tk=512):
    b, s, _ = qf.shape
    row = lambda i, j: (i, j, 0)
    whole = lambda i, j: (i, 0, 0)
    once = pl.Buffered(1)
    kern = functools.partial(_fox_kernel, tq=tq, tk=tk)
    return pl.pallas_call(
        kern,
        grid=(b, s // tq),
        in_specs=[pl.BlockSpec((1, tq, WIDTH), row),
                  pl.BlockSpec((1, LANES, s), whole, pipeline_mode=once),
                  pl.BlockSpec((1, WIDTH, s), whole, pipeline_mode=once),
                  pl.BlockSpec((1, s, WIDTH), whole, pipeline_mode=once)],
        out_specs=pl.BlockSpec((1, tq, WIDTH), row),
        out_shape=jax.ShapeDtypeStruct((b, s, WIDTH), BF16),
        compiler_params=_cparams(2),
        name="fox_attn",
    )(qf, smallt, kft, vf)


def _mix_kernel(x_ref, mod_ref, g_ref, ad_ref, af_ref, wgate_ref, wbd_ref, wbf_ref, wo_ref, o_ref):
    d = x_ref.shape[2]
    x = x_ref[0]
    shift = mod_ref[0, 3:4, :]
    scale = mod_ref[0, 4:5, :]
    gate = mod_ref[0, 5:6, :]
    h = _rms_adaln(x, g_ref[...], scale, shift).astype(BF16)
    ga = _dot(h, wgate_ref[:, 0:d])
    gb = _dot(h, wgate_ref[:, d:2 * d])
    y_dsa = _dot(ad_ref[0], wbd_ref[...])
    y_fox = _dot(af_ref[0], wbf_ref[...])
    merged = jax.nn.sigmoid(ga) * y_dsa + jax.nn.sigmoid(gb) * y_fox
    o_ref[0] = x + gate * _dot(merged.astype(BF16), wo_ref[...])


def _mix_out(x, mod, gain, a_dsa, a_fox, w_gate, w_br_dsa, w_br_fox, w_out, *, tm=512):
    b, s, d = x.shape
    row = lambda i, j: (i, j, 0)
    return pl.pallas_call(
        _mix_kernel,
        grid=(b, s // tm),
        in_specs=[pl.BlockSpec((1, tm, d), row),
                  pl.BlockSpec((1, N_MOD, d), lambda i, j: (i, 0, 0)),
                  _const_spec((1, d)),
                  pl.BlockSpec((1, tm, WIDTH), row), pl.BlockSpec((1, tm, WIDTH), row),
                  _const_spec(w_gate.shape), _const_spec(w_br_dsa.shape),
                  _const_spec(w_br_fox.shape), _const_spec(w_out.shape)],
        out_specs=pl.BlockSpec((1, tm, d), row),
        out_shape=jax.ShapeDtypeStruct((b, s, d), F32),
        compiler_params=_cparams(2),
        name="mix_out",
    )(x, mod, gain.reshape(1, d), a_dsa, a_fox, w_gate, w_br_dsa, w_br_fox, w_out)


def _layer(x, mod, norm1_g, ffn1_wg, ffn1_wu, ffn1_wd, norm2_g, w_in, b_forget,
           qn_dsa, kn_dsa, qn_fox, kn_fox, w_br_dsa, w_br_fox, w_out, norm3_g,
           ffn2_wg, ffn2_wu, ffn2_wd):
    d = x.shape[2]
    bf = lambda w: w.astype(BF16)
    x = _ffn(x, mod, norm1_g, bf(ffn1_wg), bf(ffn1_wu), bf(ffn1_wd), mod_base=0)

    n_main = 7 * WIDTH
    n_small = HEAD_DIM + 2 * N_HEADS
    w_main = bf(w_in[:, :n_main])
    w_small = jnp.zeros((d, LANES), BF16).at[:, :n_small].set(bf(w_in[:, n_main:n_main + n_small]))
    w_gate = bf(w_in[:, n_main + n_small:])
    b_small = jnp.zeros((1, LANES), F32).at[0, SM_F:SM_F + N_HEADS].set(b_forget)
    q_scale = HEAD_DIM ** -0.5
    q_gains = jnp.stack([jnp.tile(qn_dsa, N_HEADS), jnp.tile(qn_fox, N_HEADS)]) * q_scale
    k_gains = jnp.stack([kn_dsa, kn_fox], axis=1)

    qa, kat, va, qf, kft, vf, qi, small, smallt, kit = _in_proj(
        x, mod, norm2_g, w_main, w_small, b_small, q_gains, k_gains)
    a_dsa = _dsa_attn(qa, qi, small, kat, va, kit)
    a_fox = _fox_attn(qf, smallt, kft, vf)
    x = _mix_out(x, mod, norm2_g, a_dsa, a_fox, w_gate, bf(w_br_dsa), bf(w_br_fox), bf(w_out))
    return _ffn(x, mod, norm3_g, bf(ffn2_wg), bf(ffn2_wu), bf(ffn2_wd), mod_base=6)


def kernel(x, c, ada_w, ada_b, norm1_g, ffn1_wg, ffn1_wu, ffn1_wd, norm2_g, w_in, b_forget,
           qn_dsa, kn_dsa, qn_fox, kn_fox, w_br_dsa, w_br_fox, w_out, norm3_g,
           ffn2_wg, ffn2_wu, ffn2_wd):
    per_layer = (norm1_g, ffn1_wg, ffn1_wu, ffn1_wd, norm2_g, w_in, b_forget,
                 qn_dsa, kn_dsa, qn_fox, kn_fox, w_br_dsa, w_br_fox, w_out, norm3_g,
                 ffn2_wg, ffn2_wu, ffn2_wd)
    for l in range(ada_w.shape[0]):
        mod = _adaln_mod(c, ada_w[l], ada_b[l])
        x = _layer(x, mod, *(p[l] for p in per_layer))
    return x
```

```python
import functools

import jax
import jax.numpy as jnp
from jax import lax
from jax.experimental import pallas as pl
from jax.experimental.pallas import tpu as pltpu

F32 = jnp.float32
BF16 = jnp.bfloat16
I32 = jnp.int32

HEAD_DIM = 64
N_HEADS = 8
WIDTH = N_HEADS * HEAD_DIM
TOPK = 256
EPS = 1e-6
N_MOD = 9
LANES = 128
NEG = -1e30
BISECT_STEPS = 24
VMEM_LIMIT = 56 * 1024 * 1024

SM_WI = 64
SM_F = 72


def _cparams(n_axes):
    return pltpu.CompilerParams(
        dimension_semantics=("arbitrary",) * n_axes, vmem_limit_bytes=VMEM_LIMIT)


def _const_spec(shape):
    nd = len(shape)
    return pl.BlockSpec(shape, lambda *_: (0,) * nd)


def _dot(a, b):
    return jnp.dot(a, b, preferred_element_type=F32)


def _rms_adaln(x, gain, scale, shift):
    y = x * lax.rsqrt(jnp.mean(x * x, axis=-1, keepdims=True) + EPS)
    return (y * gain) * (1.0 + scale) + shift


def _mod_kernel(c_ref, w_ref, b_ref, o_ref):
    c = c_ref[...]
    a = c * jax.nn.sigmoid(c)
    o_ref[...] = jnp.dot(a, w_ref[...], preferred_element_type=F32,
                         precision=lax.Precision.HIGHEST) + b_ref[...]


def _adaln_mod(c, ada_w, ada_b):
    b, d = c.shape
    n = ada_w.shape[1]
    rows = 8
    tn = n // 8
    c_pad = jnp.zeros((rows, d), F32).at[:b].set(c)
    out = pl.pallas_call(
        _mod_kernel,
        grid=(n // tn,),
        in_specs=[pl.BlockSpec((rows, d), lambda j: (0, 0)),
                  pl.BlockSpec((d, tn), lambda j: (0, j)),
                  pl.BlockSpec((1, tn), lambda j: (0, j))],
        out_specs=pl.BlockSpec((rows, tn), lambda j: (0, j)),
        out_shape=jax.ShapeDtypeStruct((rows, n), F32),
        compiler_params=_cparams(1),
        name="adaln_mod",
    )(c_pad, ada_w, ada_b.reshape(1, n))
    return out[:b].reshape(b, N_MOD, d)


def _ffn_kernel(x_ref, mod_ref, g_ref, wg_ref, wu_ref, wd_ref, o_ref, *, mod_base, n_chunks):
    x = x_ref[0]
    shift = mod_ref[0, mod_base:mod_base + 1, :]
    scale = mod_ref[0, mod_base + 1:mod_base + 2, :]
    gate = mod_ref[0, mod_base + 2:mod_base + 3, :]
    h = _rms_adaln(x, g_ref[...], scale, shift).astype(BF16)
    dff = wg_ref.shape[1]
    ck = dff // n_chunks
    acc = jnp.zeros(x.shape, F32)
    for i in range(n_chunks):
        g = _dot(h, wg_ref[:, i * ck:(i + 1) * ck])
        u = _dot(h, wu_ref[:, i * ck:(i + 1) * ck])
        a = (g * jax.nn.sigmoid(g) * u).astype(BF16)
        acc = acc + _dot(a, wd_ref[i * ck:(i + 1) * ck, :])
    o_ref[0] = x + (0.5 * gate) * acc


def _ffn(x, mod, gain, wg, wu, wd, *, mod_base, tm=512):
    b, s, d = x.shape
    dff = wg.shape[1]
    kern = functools.partial(_ffn_kernel, mod_base=mod_base, n_chunks=2)
    return pl.pallas_call(
        kern,
        grid=(b, s // tm),
        in_specs=[pl.BlockSpec((1, tm, d), lambda i, j: (i, j, 0)),
                  pl.BlockSpec((1, N_MOD, d), lambda i, j: (i, 0, 0)),
                  _const_spec((1, d)),
                  _const_spec((d, dff)), _const_spec((d, dff)), _const_spec((dff, d))],
        out_specs=pl.BlockSpec((1, tm, d), lambda i, j: (i, j, 0)),
        out_shape=jax.ShapeDtypeStruct((b, s, d), F32),
        compiler_params=_cparams(2),
        name="ffn",
    )(x, mod, gain.reshape(1, d), wg, wu, wd)


def _split3_bf16(v):
    p1 = v.astype(BF16)
    r1 = v - p1.astype(F32)
    p2 = r1.astype(BF16)
    r2 = r1 - p2.astype(F32)
    return p1, p2, r2.astype(BF16)


def _proj_kernel(x_ref, mod_ref, g_ref, wm_ref, ws_ref, bsm_ref, qg_ref, kg_ref,
                 qa_ref, kat_ref, va_ref, qf_ref, kft_ref, vf_ref, qi_ref,
                 small_ref, smallt_ref, kit_ref, carry_ref):
    tm = x_ref.shape[1]
    x = x_ref[0]
    shift = mod_ref[0, 3:4, :]
    scale = mod_ref[0, 4:5, :]
    h = _rms_adaln(x, g_ref[...], scale, shift).astype(BF16)

    def z(i):
        return _dot(h, wm_ref[:, i * WIDTH:(i + 1) * WIDTH])

    r = lax.broadcasted_iota(I32, (WIDTH, WIDTH), 0) // HEAD_DIM
    c = lax.broadcasted_iota(I32, (WIDTH, WIDTH), 1) // HEAD_DIM
    avg = jnp.where(r == c, 1.0 / HEAD_DIM, 0.0).astype(BF16)

    def norm_q(q, gain_row):
        ms = _dot((q * q).astype(BF16), avg)
        return (q * lax.rsqrt(ms + EPS) * gain_row).astype(BF16)

    def norm_kt(k, gain_col):
        kt = k.T.reshape(N_HEADS, HEAD_DIM, tm)
        ms = jnp.mean(kt * kt, axis=1, keepdims=True)
        kt = kt * lax.rsqrt(ms + EPS) * gain_col[None]
        return kt.reshape(WIDTH, tm).astype(BF16)

    qa_ref[0] = norm_q(z(0), qg_ref[0:1, :])
    kat_ref[0] = norm_kt(z(1), kg_ref[:, 0:1])
    va_ref[0] = z(2).astype(BF16)
    qf_ref[0] = norm_q(z(3), qg_ref[1:2, :])
    kft_ref[0] = norm_kt(z(4), kg_ref[:, 1:2])
    vf_ref[0] = z(5).astype(BF16)
    qi_ref[0] = z(6) * (HEAD_DIM ** -0.5)

    zs = _dot(h, ws_ref[...])
    pre = zs + bsm_ref[...]
    logf = jnp.minimum(pre, 0.0) - jnp.log(1.0 + jnp.exp(-jnp.abs(pre)))

    @pl.when(pl.program_id(1) == 0)
    def _():
        carry_ref[...] = jnp.zeros_like(carry_ref)

    ri = lax.broadcasted_iota(I32, (tm, tm), 0)
    ci = lax.broadcasted_iota(I32, (tm, tm), 1)
    tri = jnp.where(ci <= ri, 1.0, 0.0).astype(BF16)
    p1, p2, p3 = _split3_bf16(logf)
    cum = (_dot(tri, p1) + _dot(tri, p2)) + _dot(tri, p3) + carry_ref[...]
    carry_ref[...] = cum[tm - 1:tm, :]

    lane = lax.broadcasted_iota(I32, (tm, LANES), 1)
    small = jnp.where(lane < SM_WI, zs,
                      jnp.where(lane < SM_F, zs * (N_HEADS ** -0.5), cum))
    small_ref[0] = small
    st = small.T
    smallt_ref[0] = st
    ki = st[0:HEAD_DIM, :]
    ki_hi = ki.astype(BF16)
    ki_lo = (ki - ki_hi.astype(F32)).astype(BF16)
    kit_ref[0] = jnp.concatenate([ki_hi, ki_hi, ki_lo], axis=0)


def _in_proj(x, mod, gain, w_main, w_small, b_small, q_gains, k_gains, *, tm=512):
    b, s, d = x.shape
    row = lambda i, j: (i, j, 0)
    col = lambda i, j: (i, 0, j)
    sd = jax.ShapeDtypeStruct
    out_shape = [sd((b, s, WIDTH), BF16), sd((b, WIDTH, s), BF16), sd((b, s, WIDTH), BF16),
                 sd((b, s, WIDTH), BF16), sd((b, WIDTH, s), BF16), sd((b, s, WIDTH), BF16),
                 sd((b, s, WIDTH), F32),
                 sd((b, s, LANES), F32), sd((b, LANES, s), F32), sd((b, 3 * HEAD_DIM, s), BF16)]
    rspec = pl.BlockSpec((1, tm, WIDTH), row)
    cspec = pl.BlockSpec((1, WIDTH, tm), col)
    out_specs = [rspec, cspec, rspec, rspec, cspec, rspec, rspec,
                 pl.BlockSpec((1, tm, LANES), row), pl.BlockSpec((1, LANES, tm), col),
                 pl.BlockSpec((1, 3 * HEAD_DIM, tm), col)]
    return pl.pallas_call(
        _proj_kernel,
        grid=(b, s // tm),
        in_specs=[pl.BlockSpec((1, tm, d), row),
                  pl.BlockSpec((1, N_MOD, d), lambda i, j: (i, 0, 0)),
                  _const_spec((1, d)),
                  _const_spec(w_main.shape), _const_spec(w_small.shape),
                  _const_spec((1, LANES)), _const_spec((2, WIDTH)), _const_spec((HEAD_DIM, 2))],
        out_specs=out_specs,
        out_shape=out_shape,
        scratch_shapes=[pltpu.VMEM((1, LANES), F32)],
        compiler_params=_cparams(2),
        name="in_proj",
    )(x, mod, gain.reshape(1, d), w_main, w_small, b_small, q_gains, k_gains)


FLASH_ROWS = 64
COUNT_ROWS = 64
LOG2E = 1.4426950408889634


def _flash_scratch(tq, tk):
    return [pltpu.VMEM((2, tq, LANES), F32), pltpu.VMEM((2, tq, LANES), F32),
            pltpu.VMEM((2, tq, LANES), F32), pltpu.VMEM((2, tq, tk), BF16)]


def _flash_reset(m_ref, l_ref, acc_ref):
    m_ref[...] = jnp.full(m_ref.shape, NEG, F32)
    l_ref[...] = jnp.zeros(l_ref.shape, F32)
    acc_ref[...] = jnp.zeros(acc_ref.shape, F32)


def _flash_pair_step(q_pair, kt_ref, v_ref, pair, off, chunk_logits, m_ref, l_ref, acc_ref, p_ref):
    tq, tk = p_ref.shape[1:]
    n_sub = tk // LANES
    v = v_ref[0, pl.ds(off, tk), pair * LANES:(pair + 1) * LANES]
    for i in range(2):
        h = 2 * pair + i
        kt = kt_ref[0, h * HEAD_DIM:(h + 1) * HEAD_DIM, pl.ds(off, tk)]
        for r0 in range(0, tq, FLASH_ROWS):
            rows = pl.ds(r0, FLASH_ROWS)
            s = _dot(q_pair[i][r0:r0 + FLASH_ROWS], kt)
            sc = [chunk_logits(i, r0, c, s[:, c * LANES:(c + 1) * LANES]) for c in range(n_sub)]
            mx = sc[0]
            for x in sc[1:]:
                mx = jnp.maximum(mx, x)
            m_old = m_ref[i, rows, :]
            m_new = jnp.maximum(m_old, jnp.max(mx, axis=1, keepdims=True))
            alpha = jnp.exp2(m_old - m_new)
            ps = [jnp.exp2(x - m_new) for x in sc]
            lsum = ps[0]
            for x in ps[1:]:
                lsum = lsum + x
            m_ref[i, rows, :] = m_new
            l_ref[i, rows, :] = alpha * l_ref[i, rows, :] + lsum
            acc_ref[i, rows, :] = alpha * acc_ref[i, rows, :]
            p_ref[i, rows, :] = jnp.concatenate(ps, axis=1).astype(BF16)
        acc_ref[i] = acc_ref[i] + _dot(p_ref[i], v)


def _flash_finish(l_ref, acc_ref):
    out = [acc_ref[i] / jnp.sum(l_ref[i], axis=1, keepdims=True) for i in range(2)]
    lane = lax.broadcasted_iota(I32, out[0].shape, 1)
    return jnp.where(lane < HEAD_DIM, out[0], out[1])


def _q_pair(q, pair):
    return [q[:, (2 * pair + i) * HEAD_DIM:(2 * pair + i + 1) * HEAD_DIM] for i in range(2)]


def _dsa_kernel(qa_ref, qi_ref, small_ref, kat_ref, va_ref, kit_ref, o_ref, sc_ref, icut_ref,
                m_ref, l_ref, acc_ref, p_ref, *, tq, tk, topk):
    s_len = kat_ref.shape[2]
    t0 = pl.program_id(1) * tq
    nkb = (t0 + tq + tk - 1) // tk
    n_sub = tk // LANES
    row = lax.broadcasted_iota(I32, (tq, LANES), 0) + t0
    lane = lax.broadcasted_iota(I32, (COUNT_ROWS, LANES), 1)
    rep = lambda col: jnp.broadcast_to(col, (col.shape[0], LANES))

    wi = small_ref[0][:, SM_WI:SM_WI + N_HEADS]
    qi = qi_ref[0]
    lhs = []
    for h in range(N_HEADS):
        qh = qi[:, h * HEAD_DIM:(h + 1) * HEAD_DIM]
        hi = qh.astype(BF16)
        lo = (qh - hi.astype(F32)).astype(BF16)
        lhs.append(jnp.concatenate([hi, lo, hi], axis=1))
    lane_q = lax.broadcasted_iota(I32, (tq, LANES), 1)

    def score_body(kb, carry):
        mn, mx = carry
        off = pl.multiple_of(kb * tk, tk)
        kib = kit_ref[0, :, pl.ds(off, tk)]
        acc = jnp.zeros((tq, tk), F32)
        for h in range(N_HEADS):
            acc = acc + jnp.maximum(_dot(lhs[h], kib), 0.0) * wi[:, h:h + 1]
        for i in range(n_sub):
            a = acc[:, i * LANES:(i + 1) * LANES]
            causal = lane_q + (off + i * LANES) <= row
            lowered = jnp.where(causal, a, -jnp.inf)
            sc_ref[:, pl.ds(off + i * LANES, LANES)] = lowered
            mn = jnp.minimum(mn, jnp.where(causal, a, jnp.inf))
            mx = jnp.maximum(mx, lowered)
        return mn, mx

    mn, mx = lax.fori_loop(0, nkb, score_body, (jnp.full((tq, LANES), jnp.inf, F32),
                                                jnp.full((tq, LANES), -jnp.inf, F32)))

    def scan(step, init, *row_args):
        outs = []
        for r0 in range(0, tq, COUNT_ROWS):
            args = [a[r0:r0 + COUNT_ROWS] for a in row_args]

            def body(kb, acc, r0=r0, args=args):
                off = pl.multiple_of(kb * tk, tk)
                for i in range(n_sub):
                    blk = sc_ref[pl.ds(r0, COUNT_ROWS), pl.ds(off + i * LANES, LANES)]
                    acc = step(acc, blk, lane + (off + i * LANES), *args)
                return acc

            outs.append(lax.fori_loop(0, nkb, body, jnp.full((COUNT_ROWS, LANES), init, F32)))
        return jnp.concatenate(outs, axis=0)

    def count(pred, *row_args):
        c = scan(lambda acc, blk, pos, *a: acc + jnp.where(pred(blk, pos, *a), 1.0, 0.0),
                 0.0, *row_args)
        return rep(jnp.sum(c, axis=1, keepdims=True))

    def below_max(bound):
        m = scan(lambda acc, blk, pos, b: jnp.maximum(acc, jnp.where(blk < b, blk, -jnp.inf)),
                 -jnp.inf, bound)
        return rep(jnp.max(m, axis=1, keepdims=True))

    kf = float(topk)
    few = row < topk
    rmin = rep(jnp.min(mn, axis=1, keepdims=True))
    rmax = rep(jnp.max(mx, axis=1, keepdims=True))
    above = jnp.where(rmax > 0.0, 2.0 * rmax, 0.5 * rmax) + 1.0
    lo0 = jnp.where(few, 0.0, rmin)
    hi0 = jnp.where(few, 0.0, above)
    state0 = jnp.where(few, 1.0, 0.0)

    def search_cond(st):
        it, pending = st[0], st[1]
        return (it < BISECT_STEPS) & (pending < 0.5)

    def search_body(st):
        it, _, lo, hi, state = st
        mid = 0.5 * lo + 0.5 * hi
        cnt = count(lambda blk, pos, m: blk >= m, mid)
        active = state == 0.0
        ge = cnt >= kf
        lo = jnp.where(active & ge, mid, lo)
        hi = jnp.where(active & jnp.logical_not(ge), mid, hi)
        state = jnp.where(active & (cnt == kf), 1.0, state)
        return it + 1, jnp.min(state), lo, hi, state

    _, pending, lo, hi, state = lax.while_loop(
        search_cond, search_body, (jnp.int32(0), jnp.min(state0), lo0, hi0, state0))
    thr = jnp.where(few, -jnp.inf, lo)

    def snap_body(st):
        _, hi, thr, state = st
        cand = below_max(hi)
        cnt = count(lambda blk, pos, c: blk >= c, cand)
        active = state == 0.0
        found = active & (cnt >= kf)
        thr = jnp.where(found, cand, thr)
        hi = jnp.where(active & jnp.logical_not(found), cand, hi)
        state = jnp.where(found, 2.0, state)
        return jnp.min(state), hi, thr, state

    _, _, thr, state = lax.while_loop(lambda st: st[0] < 0.5, snap_body, (pending, hi, thr, state))

    icut_ref[...] = jnp.full(icut_ref.shape, s_len, I32)
    tied = state == 2.0

    @pl.when(jnp.max(state) > 1.5)
    def _():
        need = kf - count(lambda blk, pos, t: blk > t, thr)
        n_bits = max(1, (s_len - 1).bit_length())

        def idx_body(i, cut):
            cand = cut | jnp.left_shift(jnp.int32(1), n_bits - 1 - i)
            cnt = count(lambda blk, pos, t, cd: (blk == t) & (pos < cd), thr, cand)
            return jnp.where(cnt < need, cand, cut)

        cut = lax.fori_loop(0, n_bits, idx_body, jnp.zeros((tq, LANES), I32))
        icut_ref[...] = jnp.where(tied, cut, s_len)

    icut = icut_ref[...]

    for r0 in range(0, tq, COUNT_ROWS):
        rows = pl.ds(r0, COUNT_ROWS)
        thr_r, icut_r, row_r = (a[r0:r0 + COUNT_ROWS] for a in (thr, icut, row))

        def bias_body(kb, _, rows=rows, thr_r=thr_r, icut_r=icut_r, row_r=row_r):
            off = pl.multiple_of(kb * tk, tk)
            for i in range(n_sub):
                sl = pl.ds(off + i * LANES, LANES)
                blk = sc_ref[rows, sl]
                pos = lane + (off + i * LANES)
                sel = (blk > thr_r) | ((blk == thr_r) & (pos <= icut_r))
                sc_ref[rows, sl] = jnp.where(sel & (pos <= row_r), 0.0, NEG)
            return 0

        lax.fori_loop(0, nkb, bias_body, 0)

    qa = qa_ref[0]
    kpos = lax.broadcasted_iota(I32, (1, tk), 1)
    for pair in range(N_HEADS // 2):
        slopes = [LOG2E * 2.0 ** (-8.0 * (2 * pair + i + 1) / N_HEADS) for i in range(2)]
        q_pair = _q_pair(qa, pair)
        _flash_reset(m_ref, l_ref, acc_ref)

        def attn_body(kb, _, pair=pair, slopes=slopes, q_pair=q_pair):
            off = pl.multiple_of(kb * tk, tk)
            rel = (kpos + (off - t0)).astype(F32)
            alibi = [slopes[i] * rel for i in range(2)]

            def chunk_logits(i, r0, c, x):
                mask = sc_ref[pl.ds(r0, FLASH_ROWS), pl.ds(off + c * LANES, LANES)]
                return x + (mask + alibi[i][:, c * LANES:(c + 1) * LANES])

            _flash_pair_step(q_pair, kat_ref, va_ref, pair, off, chunk_logits,
                             m_ref, l_ref, acc_ref, p_ref)
            return 0

        lax.fori_loop(0, nkb, attn_body, 0)
        o_ref[0, :, pair * LANES:(pair + 1) * LANES] = _flash_finish(l_ref, acc_ref).astype(BF16)


def _dsa_attn(qa, qi, small, kat, va, kit, *, tq=512, tk=512):
    b, s, _ = qa.shape
    row = lambda i, j: (i, j, 0)
    whole = lambda i, j: (i, 0, 0)
    once = pl.Buffered(1)
    kern = functools.partial(_dsa_kernel, tq=tq, tk=tk, topk=min(TOPK, s // 4))
    return pl.pallas_call(
        kern,
        grid=(b, s // tq),
        in_specs=[pl.BlockSpec((1, tq, WIDTH), row), pl.BlockSpec((1, tq, WIDTH), row),
                  pl.BlockSpec((1, tq, LANES), row),
                  pl.BlockSpec((1, WIDTH, s), whole, pipeline_mode=once),
                  pl.BlockSpec((1, s, WIDTH), whole, pipeline_mode=once),
                  pl.BlockSpec((1, 3 * HEAD_DIM, s), whole, pipeline_mode=once)],
        out_specs=pl.BlockSpec((1, tq, WIDTH), row),
        out_shape=jax.ShapeDtypeStruct((b, s, WIDTH), BF16),
        scratch_shapes=[pltpu.VMEM((tq, s), F32), pltpu.VMEM((tq, LANES), I32)]
        + _flash_scratch(tq, tk),
        compiler_params=_cparams(2),
        name="dsa_attn",
    )(qa, qi, small, kat, va, kit)


def _fox_kernel(qf_ref, smallt_ref, kft_ref, vf_ref, o_ref, m_ref, l_ref, acc_ref, p_ref, *, tq, tk):
    t0 = pl.program_id(1) * tq
    n_full = t0 // tk
    nkb = (t0 + tq + tk - 1) // tk
    row = lax.broadcasted_iota(I32, (FLASH_ROWS, LANES), 0) + t0
    col = lax.broadcasted_iota(I32, (FLASH_ROWS, LANES), 1)
    qf = qf_ref[0]
    f_t0 = smallt_ref[0, SM_F:SM_F + N_HEADS, pl.ds(pl.multiple_of(t0, LANES), LANES)][:, 0:1]
    for pair in range(N_HEADS // 2):
        q_pair = _q_pair(qf, pair)
        _flash_reset(m_ref, l_ref, acc_ref)

        def attn_body(kb, _, causal, pair=pair, q_pair=q_pair):
            off = pl.multiple_of(kb * tk, tk)
            decay = [LOG2E * (f_t0[2 * pair + i:2 * pair + i + 1, :]
                              - smallt_ref[0, SM_F + 2 * pair + i:SM_F + 2 * pair + i + 1, pl.ds(off, tk)])
                     for i in range(2)]

            def chunk_logits(i, r0, c, x):
                x = x + decay[i][:, c * LANES:(c + 1) * LANES]
                if causal:
                    x = jnp.where(col + (off + c * LANES) <= row + r0, x, NEG)
                return x

            _flash_pair_step(q_pair, kft_ref, vf_ref, pair, off, chunk_logits,
                             m_ref, l_ref, acc_ref, p_ref)
            return 0

        lax.fori_loop(0, n_full, functools.partial(attn_body, causal=False), 0)
        lax.fori_loop(n_full, nkb, functools.partial(attn_body, causal=True), 0)
        o_ref[0, :, pair * LANES:(pair + 1) * LANES] = _flash_finish(l_ref, acc_ref).astype(BF16)


def _fox_attn(qf, smallt, kft, vf, *, tq=512, tk=512):
    b, s, _ = qf.shape
    row = lambda i, j: (i, j, 0)
    whole = lambda i, j: (i, 0, 0)
    once = pl.Buffered(1)
    kern = functools.partial(_fox_kernel, tq=tq, tk=tk)
    return pl.pallas_call(
        kern,
        grid=(b, s // tq),
        in_specs=[pl.BlockSpec((1, tq, WIDTH), row),
                  pl.BlockSpec((1, LANES, s), whole, pipeline_mode=once),
                  pl.BlockSpec((1, WIDTH, s), whole, pipeline_mode=once),
                  pl.BlockSpec((1, s, WIDTH), whole, pipeline_mode=once)],
        out_specs=pl.BlockSpec((1, tq, WIDTH), row),
        out_shape=jax.ShapeDtypeStruct((b, s, WIDTH), BF16),
        scratch_shapes=_flash_scratch(tq, tk),
        compiler_params=_cparams(2),
        name="fox_attn",
    )(qf, smallt, kft, vf)


def _mix_kernel(x_ref, mod_ref, g_ref, ad_ref, af_ref, wgate_ref, wbd_ref, wbf_ref, wo_ref, o_ref):
    d = x_ref.shape[2]
    x = x_ref[0]
    shift = mod_ref[0, 3:4, :]
    scale = mod_ref[0, 4:5, :]
    gate = mod_ref[0, 5:6, :]
    h = _rms_adaln(x, g_ref[...], scale, shift).astype(BF16)
    ga = _dot(h, wgate_ref[:, 0:d])
    gb = _dot(h, wgate_ref[:, d:2 * d])
    y_dsa = _dot(ad_ref[0], wbd_ref[...])
    y_fox = _dot(af_ref[0], wbf_ref[...])
    merged = jax.nn.sigmoid(ga) * y_dsa + jax.nn.sigmoid(gb) * y_fox
    o_ref[0] = x + gate * _dot(merged.astype(BF16), wo_ref[...])


def _mix_out(x, mod, gain, a_dsa, a_fox, w_gate, w_br_dsa, w_br_fox, w_out, *, tm=512):
    b, s, d = x.shape
    row = lambda i, j: (i, j, 0)
    return pl.pallas_call(
        _mix_kernel,
        grid=(b, s // tm),
        in_specs=[pl.BlockSpec((1, tm, d), row),
                  pl.BlockSpec((1, N_MOD, d), lambda i, j: (i, 0, 0)),
                  _const_spec((1, d)),
                  pl.BlockSpec((1, tm, WIDTH), row), pl.BlockSpec((1, tm, WIDTH), row),
                  _const_spec(w_gate.shape), _const_spec(w_br_dsa.shape),
                  _const_spec(w_br_fox.shape), _const_spec(w_out.shape)],
        out_specs=pl.BlockSpec((1, tm, d), row),
        out_shape=jax.ShapeDtypeStruct((b, s, d), F32),
        compiler_params=_cparams(2),
        name="mix_out",
    )(x, mod, gain.reshape(1, d), a_dsa, a_fox, w_gate, w_br_dsa, w_br_fox, w_out)


def _layer(x, mod, norm1_g, ffn1_wg, ffn1_wu, ffn1_wd, norm2_g, w_in, b_forget,
           qn_dsa, kn_dsa, qn_fox, kn_fox, w_br_dsa, w_br_fox, w_out, norm3_g,
           ffn2_wg, ffn2_wu, ffn2_wd):
    d = x.shape[2]
    bf = lambda w: w.astype(BF16)
    x = _ffn(x, mod, norm1_g, bf(ffn1_wg), bf(ffn1_wu), bf(ffn1_wd), mod_base=0)

    n_main = 7 * WIDTH
    n_small = HEAD_DIM + 2 * N_HEADS
    w_main = bf(w_in[:, :n_main])
    w_small = jnp.zeros((d, LANES), BF16).at[:, :n_small].set(bf(w_in[:, n_main:n_main + n_small]))
    w_gate = bf(w_in[:, n_main + n_small:])
    b_small = jnp.zeros((1, LANES), F32).at[0, SM_F:SM_F + N_HEADS].set(b_forget)
    q_scale = LOG2E * HEAD_DIM ** -0.5
    q_gains = jnp.stack([jnp.tile(qn_dsa, N_HEADS), jnp.tile(qn_fox, N_HEADS)]) * q_scale
    k_gains = jnp.stack([kn_dsa, kn_fox], axis=1)

    qa, kat, va, qf, kft, vf, qi, small, smallt, kit = _in_proj(
        x, mod, norm2_g, w_main, w_small, b_small, q_gains, k_gains)
    a_dsa = _dsa_attn(qa, qi, small, kat, va, kit)
    a_fox = _fox_attn(qf, smallt, kft, vf)
    x = _mix_out(x, mod, norm2_g, a_dsa, a_fox, w_gate, bf(w_br_dsa), bf(w_br_fox), bf(w_out))
    return _ffn(x, mod, norm3_g, bf(ffn2_wg), bf(ffn2_wu), bf(ffn2_wd), mod_base=6)


def kernel(x, c, ada_w, ada_b, norm1_g, ffn1_wg, ffn1_wu, ffn1_wd, norm2_g, w_in, b_forget,
           qn_dsa, kn_dsa, qn_fox, kn_fox, w_br_dsa, w_br_fox, w_out, norm3_g,
           ffn2_wg, ffn2_wu, ffn2_wd):
    per_layer = (norm1_g, ffn1_wg, ffn1_wu, ffn1_wd, norm2_g, w_in, b_forget,
                 qn_dsa, kn_dsa, qn_fox, kn_fox, w_br_dsa, w_br_fox, w_out, norm3_g,
                 ffn2_wg, ffn2_wu, ffn2_wd)
    for l in range(ada_w.shape[0]):
        mod = _adaln_mod(c, ada_w[l], ada_b[l])
        x = _layer(x, mod, *(p[l] for p in per_layer))
    return x
```

```python
import functools

import jax
import jax.numpy as jnp
from jax import lax
from jax.experimental import pallas as pl
from jax.experimental.pallas import tpu as pltpu

F32 = jnp.float32
BF16 = jnp.bfloat16
I32 = jnp.int32

HEAD_DIM = 64
N_HEADS = 8
WIDTH = N_HEADS * HEAD_DIM
TOPK = 256
EPS = 1e-6
N_MOD = 9
LANES = 128
NEG = -1e30
BISECT_STEPS = 24
VMEM_LIMIT = 56 * 1024 * 1024

SM_WI = 64
SM_F = 72


def _cparams(n_axes):
    return pltpu.CompilerParams(
        dimension_semantics=("arbitrary",) * n_axes, vmem_limit_bytes=VMEM_LIMIT)


def _const_spec(shape):
    nd = len(shape)
    return pl.BlockSpec(shape, lambda *_: (0,) * nd)


def _dot(a, b):
    return jnp.dot(a, b, preferred_element_type=F32)


def _rms_adaln(x, gain, scale, shift):
    y = x * lax.rsqrt(jnp.mean(x * x, axis=-1, keepdims=True) + EPS)
    return (y * gain) * (1.0 + scale) + shift


def _mod_kernel(c_ref, w_ref, b_ref, o_ref):
    c = c_ref[...]
    a = c * jax.nn.sigmoid(c)
    o_ref[...] = jnp.dot(a, w_ref[...], preferred_element_type=F32,
                         precision=lax.Precision.HIGHEST) + b_ref[...]


def _adaln_mod(c, ada_w, ada_b):
    b, d = c.shape
    n = ada_w.shape[1]
    rows = 8
    tn = n // 8
    c_pad = jnp.zeros((rows, d), F32).at[:b].set(c)
    out = pl.pallas_call(
        _mod_kernel,
        grid=(n // tn,),
        in_specs=[pl.BlockSpec((rows, d), lambda j: (0, 0)),
                  pl.BlockSpec((d, tn), lambda j: (0, j)),
                  pl.BlockSpec((1, tn), lambda j: (0, j))],
        out_specs=pl.BlockSpec((rows, tn), lambda j: (0, j)),
        out_shape=jax.ShapeDtypeStruct((rows, n), F32),
        compiler_params=_cparams(1),
        name="adaln_mod",
    )(c_pad, ada_w, ada_b.reshape(1, n))
    return out[:b].reshape(b, N_MOD, d)


def _ffn_kernel(x_ref, mod_ref, g_ref, wg_ref, wu_ref, wd_ref, o_ref, *, mod_base, n_chunks):
    x = x_ref[0]
    shift = mod_ref[0, mod_base:mod_base + 1, :]
    scale = mod_ref[0, mod_base + 1:mod_base + 2, :]
    gate = mod_ref[0, mod_base + 2:mod_base + 3, :]
    h = _rms_adaln(x, g_ref[...], scale, shift).astype(BF16)
    dff = wg_ref.shape[1]
    ck = dff // n_chunks
    acc = jnp.zeros(x.shape, F32)
    for i in range(n_chunks):
        g = _dot(h, wg_ref[:, i * ck:(i + 1) * ck])
        u = _dot(h, wu_ref[:, i * ck:(i + 1) * ck])
        a = (g * jax.nn.sigmoid(g) * u).astype(BF16)
        acc = acc + _dot(a, wd_ref[i * ck:(i + 1) * ck, :])
    o_ref[0] = x + (0.5 * gate) * acc


def _ffn(x, mod, gain, wg, wu, wd, *, mod_base, tm=512):
    b, s, d = x.shape
    dff = wg.shape[1]
    kern = functools.partial(_ffn_kernel, mod_base=mod_base, n_chunks=2)
    return pl.pallas_call(
        kern,
        grid=(b, s // tm),
        in_specs=[pl.BlockSpec((1, tm, d), lambda i, j: (i, j, 0)),
                  pl.BlockSpec((1, N_MOD, d), lambda i, j: (i, 0, 0)),
                  _const_spec((1, d)),
                  _const_spec((d, dff)), _const_spec((d, dff)), _const_spec((dff, d))],
        out_specs=pl.BlockSpec((1, tm, d), lambda i, j: (i, j, 0)),
        out_shape=jax.ShapeDtypeStruct((b, s, d), F32),
        compiler_params=_cparams(2),
        name="ffn",
    )(x, mod, gain.reshape(1, d), wg, wu, wd)


def _split3_bf16(v):
    p1 = v.astype(BF16)
    r1 = v - p1.astype(F32)
    p2 = r1.astype(BF16)
    r2 = r1 - p2.astype(F32)
    return p1, p2, r2.astype(BF16)


def _proj_kernel(x_ref, mod_ref, g_ref, wm_ref, ws_ref, bsm_ref, qg_ref, kg_ref,
                 qa_ref, kat_ref, va_ref, qf_ref, kft_ref, vf_ref, qi_ref,
                 small_ref, smallt_ref, kit_ref, carry_ref):
    tm = x_ref.shape[1]
    x = x_ref[0]
    shift = mod_ref[0, 3:4, :]
    scale = mod_ref[0, 4:5, :]
    h = _rms_adaln(x, g_ref[...], scale, shift).astype(BF16)

    def z(i):
        return _dot(h, wm_ref[:, i * WIDTH:(i + 1) * WIDTH])

    r = lax.broadcasted_iota(I32, (WIDTH, WIDTH), 0) // HEAD_DIM
    c = lax.broadcasted_iota(I32, (WIDTH, WIDTH), 1) // HEAD_DIM
    avg = jnp.where(r == c, 1.0 / HEAD_DIM, 0.0).astype(BF16)

    def norm_q(q, gain_row):
        ms = _dot((q * q).astype(BF16), avg)
        return (q * lax.rsqrt(ms + EPS) * gain_row).astype(BF16)

    def norm_kt(k, gain_col):
        kt = k.T.reshape(N_HEADS, HEAD_DIM, tm)
        ms = jnp.mean(kt * kt, axis=1, keepdims=True)
        kt = kt * lax.rsqrt(ms + EPS) * gain_col[None]
        return kt.reshape(WIDTH, tm).astype(BF16)

    qa_ref[0] = norm_q(z(0), qg_ref[0:1, :])
    kat_ref[0] = norm_kt(z(1), kg_ref[:, 0:1])
    va_ref[0] = z(2).astype(BF16)
    qf_ref[0] = norm_q(z(3), qg_ref[1:2, :])
    kft_ref[0] = norm_kt(z(4), kg_ref[:, 1:2])
    vf_ref[0] = z(5).astype(BF16)
    qi_ref[0] = z(6) * (HEAD_DIM ** -0.5)

    zs = _dot(h, ws_ref[...])
    pre = zs + bsm_ref[...]
    logf = jnp.minimum(pre, 0.0) - jnp.log(1.0 + jnp.exp(-jnp.abs(pre)))

    @pl.when(pl.program_id(1) == 0)
    def _():
        carry_ref[...] = jnp.zeros_like(carry_ref)

    ri = lax.broadcasted_iota(I32, (tm, tm), 0)
    ci = lax.broadcasted_iota(I32, (tm, tm), 1)
    tri = jnp.where(ci <= ri, 1.0, 0.0).astype(BF16)
    p1, p2, p3 = _split3_bf16(logf)
    cum = (_dot(tri, p1) + _dot(tri, p2)) + _dot(tri, p3) + carry_ref[...]
    carry_ref[...] = cum[tm - 1:tm, :]

    lane = lax.broadcasted_iota(I32, (tm, LANES), 1)
    small = jnp.where(lane < SM_WI, zs,
                      jnp.where(lane < SM_F, zs * (N_HEADS ** -0.5), cum))
    small_ref[0] = small
    st = small.T
    smallt_ref[0] = st
    ki = st[0:HEAD_DIM, :]
    ki_hi = ki.astype(BF16)
    ki_lo = (ki - ki_hi.astype(F32)).astype(BF16)
    kit_ref[0] = jnp.concatenate([ki_hi, ki_hi, ki_lo], axis=0)


def _in_proj(x, mod, gain, w_main, w_small, b_small, q_gains, k_gains, *, tm=512):
    b, s, d = x.shape
    row = lambda i, j: (i, j, 0)
    col = lambda i, j: (i, 0, j)
    sd = jax.ShapeDtypeStruct
    out_shape = [sd((b, s, WIDTH), BF16), sd((b, WIDTH, s), BF16), sd((b, s, WIDTH), BF16),
                 sd((b, s, WIDTH), BF16), sd((b, WIDTH, s), BF16), sd((b, s, WIDTH), BF16),
                 sd((b, s, WIDTH), F32),
                 sd((b, s, LANES), F32), sd((b, LANES, s), F32), sd((b, 3 * HEAD_DIM, s), BF16)]
    rspec = pl.BlockSpec((1, tm, WIDTH), row)
    cspec = pl.BlockSpec((1, WIDTH, tm), col)
    out_specs = [rspec, cspec, rspec, rspec, cspec, rspec, rspec,
                 pl.BlockSpec((1, tm, LANES), row), pl.BlockSpec((1, LANES, tm), col),
                 pl.BlockSpec((1, 3 * HEAD_DIM, tm), col)]
    return pl.pallas_call(
        _proj_kernel,
        grid=(b, s // tm),
        in_specs=[pl.BlockSpec((1, tm, d), row),
                  pl.BlockSpec((1, N_MOD, d), lambda i, j: (i, 0, 0)),
                  _const_spec((1, d)),
                  _const_spec(w_main.shape), _const_spec(w_small.shape),
                  _const_spec((1, LANES)), _const_spec((2, WIDTH)), _const_spec((HEAD_DIM, 2))],
        out_specs=out_specs,
        out_shape=out_shape,
        scratch_shapes=[pltpu.VMEM((1, LANES), F32)],
        compiler_params=_cparams(2),
        name="in_proj",
    )(x, mod, gain.reshape(1, d), w_main, w_small, b_small, q_gains, k_gains)


FLASH_ROWS = 64
COUNT_ROWS = 64
LOG2E = 1.4426950408889634


def _flash_scratch(tq, tk):
    return [pltpu.VMEM((N_HEADS, tq, LANES), F32), pltpu.VMEM((N_HEADS, tq, LANES), F32),
            pltpu.VMEM((N_HEADS, tq, LANES), F32), pltpu.VMEM((N_HEADS, tq, tk), BF16)]


def _flash_reset(m_ref, l_ref, acc_ref):
    m_ref[...] = jnp.full(m_ref.shape, NEG, F32)
    l_ref[...] = jnp.zeros(l_ref.shape, F32)
    acc_ref[...] = jnp.zeros(acc_ref.shape, F32)


def _flash_step(q, kt_ref, v_ref, off, chunk_logits, m_ref, l_ref, acc_ref, p_ref):
    tq, tk = p_ref.shape[1:]
    n_sub = tk // LANES
    for h in range(N_HEADS):
        pair = h // 2
        q_h = q[:, h * HEAD_DIM:(h + 1) * HEAD_DIM]
        kt = kt_ref[0, h * HEAD_DIM:(h + 1) * HEAD_DIM, pl.ds(off, tk)]
        for r0 in range(0, tq, FLASH_ROWS):
            rows = pl.ds(r0, FLASH_ROWS)
            s = _dot(q_h[r0:r0 + FLASH_ROWS], kt)
            sc = [chunk_logits(h, r0, c, s[:, c * LANES:(c + 1) * LANES]) for c in range(n_sub)]
            mx = sc[0]
            for x in sc[1:]:
                mx = jnp.maximum(mx, x)
            m_old = m_ref[h, rows, :]
            m_new = jnp.maximum(m_old, jnp.max(mx, axis=1, keepdims=True))
            alpha = jnp.exp2(m_old - m_new)
            ps = [jnp.exp2(x - m_new) for x in sc]
            lsum = ps[0]
            for x in ps[1:]:
                lsum = lsum + x
            m_ref[h, rows, :] = m_new
            l_ref[h, rows, :] = alpha * l_ref[h, rows, :] + lsum
            acc_ref[h, rows, :] = alpha * acc_ref[h, rows, :]
            p_ref[h, rows, :] = jnp.concatenate(ps, axis=1).astype(BF16)
        v = v_ref[0, pl.ds(off, tk), pair * LANES:(pair + 1) * LANES]
        acc_ref[h] = acc_ref[h] + _dot(p_ref[h], v)


def _flash_finish(l_ref, acc_ref, o_ref):
    for pair in range(N_HEADS // 2):
        out = [acc_ref[h] / jnp.sum(l_ref[h], axis=1, keepdims=True) for h in (2 * pair, 2 * pair + 1)]
        lane = lax.broadcasted_iota(I32, out[0].shape, 1)
        o_ref[0, :, pair * LANES:(pair + 1) * LANES] = jnp.where(
            lane < HEAD_DIM, out[0], out[1]).astype(BF16)


def _dsa_kernel(qa_ref, qi_ref, small_ref, kat_ref, va_ref, kit_ref, o_ref, sc_ref, icut_ref,
                m_ref, l_ref, acc_ref, p_ref, *, tq, tk, topk):
    s_len = kat_ref.shape[2]
    t0 = pl.program_id(1) * tq
    nkb = (t0 + tq + tk - 1) // tk
    n_sub = tk // LANES
    row = lax.broadcasted_iota(I32, (tq, LANES), 0) + t0
    lane = lax.broadcasted_iota(I32, (COUNT_ROWS, LANES), 1)
    rep = lambda col: jnp.broadcast_to(col, (col.shape[0], LANES))

    wi = small_ref[0][:, SM_WI:SM_WI + N_HEADS]
    qi = qi_ref[0]
    lhs = []
    for h in range(N_HEADS):
        qh = qi[:, h * HEAD_DIM:(h + 1) * HEAD_DIM]
        hi = qh.astype(BF16)
        lo = (qh - hi.astype(F32)).astype(BF16)
        lhs.append(jnp.concatenate([hi, lo, hi], axis=1))
    lane_q = lax.broadcasted_iota(I32, (tq, LANES), 1)

    def score_body(kb, carry):
        mn, mx = carry
        off = pl.multiple_of(kb * tk, tk)
        kib = kit_ref[0, :, pl.ds(off, tk)]
        acc = jnp.zeros((tq, tk), F32)
        for h in range(N_HEADS):
            acc = acc + jnp.maximum(_dot(lhs[h], kib), 0.0) * wi[:, h:h + 1]
        for i in range(n_sub):
            a = acc[:, i * LANES:(i + 1) * LANES]
            causal = lane_q + (off + i * LANES) <= row
            lowered = jnp.where(causal, a, -jnp.inf)
            sc_ref[:, pl.ds(off + i * LANES, LANES)] = lowered
            mn = jnp.minimum(mn, jnp.where(causal, a, jnp.inf))
            mx = jnp.maximum(mx, lowered)
        return mn, mx

    mn, mx = lax.fori_loop(0, nkb, score_body, (jnp.full((tq, LANES), jnp.inf, F32),
                                                jnp.full((tq, LANES), -jnp.inf, F32)))

    slabs = range(0, tq, COUNT_ROWS)

    def scan(step, init, *row_args, settled=None):
        outs = []
        for k, r0 in enumerate(slabs):
            args = [a[r0:r0 + COUNT_ROWS] for a in row_args]
            start = jnp.full((COUNT_ROWS, LANES), init, F32)

            def body(kb, acc, r0=r0, args=args):
                off = pl.multiple_of(kb * tk, tk)
                for i in range(n_sub):
                    blk = sc_ref[pl.ds(r0, COUNT_ROWS), pl.ds(off + i * LANES, LANES)]
                    acc = step(acc, blk, lane + (off + i * LANES), *args)
                return acc

            run = functools.partial(lax.fori_loop, 0, nkb, body, start)
            outs.append(run() if settled is None
                        else lax.cond(settled[k] > 0.5, lambda start=start: start, run))
        return jnp.concatenate(outs, axis=0)

    def count(pred, *row_args, settled=None):
        c = scan(lambda acc, blk, pos, *a: acc + jnp.where(pred(blk, pos, *a), 1.0, 0.0),
                 0.0, *row_args, settled=settled)
        return rep(jnp.sum(c, axis=1, keepdims=True))

    def below_max(bound, settled):
        m = scan(lambda acc, blk, pos, b: jnp.maximum(acc, jnp.where(blk < b, blk, -jnp.inf)),
                 -jnp.inf, bound, settled=settled)
        return rep(jnp.max(m, axis=1, keepdims=True))

    def slab_settled(state):
        return tuple(jnp.min(state[r0:r0 + COUNT_ROWS]) for r0 in slabs)

    kf = float(topk)
    few = row < topk
    rmin = rep(jnp.min(mn, axis=1, keepdims=True))
    rmax = rep(jnp.max(mx, axis=1, keepdims=True))
    above = jnp.where(rmax > 0.0, 2.0 * rmax, 0.5 * rmax) + 1.0
    lo0 = jnp.where(few, 0.0, rmin)
    hi0 = jnp.where(few, 0.0, above)
    state0 = jnp.where(few, 1.0, 0.0)

    def search_cond(st):
        it, settled = st[0], st[1]
        return (it < BISECT_STEPS) & (functools.reduce(jnp.minimum, settled) < 0.5)

    def search_body(st):
        it, settled, lo, hi, state = st
        mid = 0.5 * lo + 0.5 * hi
        cnt = count(lambda blk, pos, m: blk >= m, mid, settled=settled)
        active = state == 0.0
        ge = cnt >= kf
        lo = jnp.where(active, jnp.where(ge, mid, lo), lo)
        hi = jnp.where(active, jnp.where(ge, hi, mid), hi)
        state = jnp.where(active, jnp.where(cnt == kf, 1.0, 0.0), state)
        return it + 1, slab_settled(state), lo, hi, state

    _, settled, lo, hi, state = lax.while_loop(
        search_cond, search_body, (jnp.int32(0), slab_settled(state0), lo0, hi0, state0))
    thr = jnp.where(few, -jnp.inf, lo)

    def snap_body(st):
        settled, hi, thr, state = st
        cand = below_max(hi, settled)
        cnt = count(lambda blk, pos, c: blk >= c, cand, settled=settled)
        active = state == 0.0
        found = cnt >= kf
        thr = jnp.where(active, jnp.where(found, cand, thr), thr)
        hi = jnp.where(active, jnp.where(found, hi, cand), hi)
        state = jnp.where(active, jnp.where(found, 2.0, 0.0), state)
        return slab_settled(state), hi, thr, state

    _, _, thr, state = lax.while_loop(
        lambda st: functools.reduce(jnp.minimum, st[0]) < 0.5, snap_body, (settled, hi, thr, state))

    icut_ref[...] = jnp.full(icut_ref.shape, s_len, I32)
    tied = state == 2.0

    @pl.when(jnp.max(state) > 1.5)
    def _():
        need = kf - count(lambda blk, pos, t: blk > t, thr)
        n_bits = max(1, (s_len - 1).bit_length())

        def idx_body(i, cut):
            cand = cut | jnp.left_shift(jnp.int32(1), n_bits - 1 - i)
            cnt = count(lambda blk, pos, t, cd: (blk == t) & (pos < cd), thr, cand)
            return jnp.where(cnt < need, cand, cut)

        cut = lax.fori_loop(0, n_bits, idx_body, jnp.zeros((tq, LANES), I32))
        icut_ref[...] = jnp.where(tied, cut, s_len)

    icut = icut_ref[...]

    for r0 in range(0, tq, COUNT_ROWS):
        rows = pl.ds(r0, COUNT_ROWS)
        thr_r, icut_r, row_r = (a[r0:r0 + COUNT_ROWS] for a in (thr, icut, row))

        def bias_body(kb, _, rows=rows, thr_r=thr_r, icut_r=icut_r, row_r=row_r):
            off = pl.multiple_of(kb * tk, tk)
            for i in range(n_sub):
                sl = pl.ds(off + i * LANES, LANES)
                blk = sc_ref[rows, sl]
                pos = lane + (off + i * LANES)
                sel = (blk > thr_r) | ((blk == thr_r) & (pos <= icut_r))
                sc_ref[rows, sl] = jnp.where(sel & (pos <= row_r), 0.0, NEG)
            return 0

        lax.fori_loop(0, nkb, bias_body, 0)

    qa = qa_ref[0]
    kpos = lax.broadcasted_iota(I32, (1, tk), 1)
    slopes = [LOG2E * 2.0 ** (-8.0 * (h + 1) / N_HEADS) for h in range(N_HEADS)]
    _flash_reset(m_ref, l_ref, acc_ref)

    def attn_body(kb, _):
        off = pl.multiple_of(kb * tk, tk)
        rel = (kpos + (off - t0)).astype(F32)

        def chunk_logits(h, r0, c, x):
            mask = sc_ref[pl.ds(r0, FLASH_ROWS), pl.ds(off + c * LANES, LANES)]
            return x + (mask + slopes[h] * rel[:, c * LANES:(c + 1) * LANES])

        _flash_step(qa, kat_ref, va_ref, off, chunk_logits, m_ref, l_ref, acc_ref, p_ref)
        return 0

    lax.fori_loop(0, nkb, attn_body, 0)
    _flash_finish(l_ref, acc_ref, o_ref)


def _dsa_attn(qa, qi, small, kat, va, kit, *, tq=512, tk=512):
    b, s, _ = qa.shape
    row = lambda i, j: (i, j, 0)
    whole = lambda i, j: (i, 0, 0)
    once = pl.Buffered(1)
    kern = functools.partial(_dsa_kernel, tq=tq, tk=tk, topk=min(TOPK, s // 4))
    return pl.pallas_call(
        kern,
        grid=(b, s // tq),
        in_specs=[pl.BlockSpec((1, tq, WIDTH), row), pl.BlockSpec((1, tq, WIDTH), row),
                  pl.BlockSpec((1, tq, LANES), row),
                  pl.BlockSpec((1, WIDTH, s), whole, pipeline_mode=once),
                  pl.BlockSpec((1, s, WIDTH), whole, pipeline_mode=once),
                  pl.BlockSpec((1, 3 * HEAD_DIM, s), whole, pipeline_mode=once)],
        out_specs=pl.BlockSpec((1, tq, WIDTH), row),
        out_shape=jax.ShapeDtypeStruct((b, s, WIDTH), BF16),
        scratch_shapes=[pltpu.VMEM((tq, s), F32), pltpu.VMEM((tq, LANES), I32)]
        + _flash_scratch(tq, tk),
        compiler_params=_cparams(2),
        name="dsa_attn",
    )(qa, qi, small, kat, va, kit)


def _fox_kernel(qf_ref, smallt_ref, kft_ref, vf_ref, o_ref, m_ref, l_ref, acc_ref, p_ref, *, tq, tk):
    t0 = pl.program_id(1) * tq
    n_full = t0 // tk
    nkb = (t0 + tq + tk - 1) // tk
    row = lax.broadcasted_iota(I32, (FLASH_ROWS, LANES), 0) + t0
    col = lax.broadcasted_iota(I32, (FLASH_ROWS, LANES), 1)
    qf = qf_ref[0]
    f_t0 = smallt_ref[0, SM_F:SM_F + N_HEADS, pl.ds(pl.multiple_of(t0, LANES), LANES)][:, 0:1]
    _flash_reset(m_ref, l_ref, acc_ref)

    def attn_body(kb, _, causal):
        off = pl.multiple_of(kb * tk, tk)
        decay = LOG2E * (f_t0 - smallt_ref[0, SM_F:SM_F + N_HEADS, pl.ds(off, tk)])

        def chunk_logits(h, r0, c, x):
            x = x + decay[h:h + 1, c * LANES:(c + 1) * LANES]
            if causal:
                x = jnp.where(col + (off + c * LANES) <= row + r0, x, NEG)
            return x

        _flash_step(qf, kft_ref, vf_ref, off, chunk_logits, m_ref, l_ref, acc_ref, p_ref)
        return 0

    lax.fori_loop(0, n_full, functools.partial(attn_body, causal=False), 0)
    lax.fori_loop(n_full, nkb, functools.partial(attn_body, causal=True), 0)
    _flash_finish(l_ref, acc_ref, o_ref)


def _fox_attn(qf, smallt, kft, vf, *, tq=512, tk=512):
    b, s, _ = qf.shape
    row = lambda i, j: (i, j, 0)
    whole = lambda i, j: (i, 0, 0)
    once = pl.Buffered(1)
    kern = functools.partial(_fox_kernel, tq=tq, tk=tk)
    return pl.pallas_call(
        kern,
        grid=(b, s // tq),
        in_specs=[pl.BlockSpec((1, tq, WIDTH), row),
                  pl.BlockSpec((1, LANES, s), whole, pipeline_mode=once),
                  pl.BlockSpec((1, WIDTH, s), whole, pipeline_mode=once),
                  pl.BlockSpec((1, s, WIDTH), whole, pipeline_mode=once)],
        out_specs=pl.BlockSpec((1, tq, WIDTH), row),
        out_shape=jax.ShapeDtypeStruct((b, s, WIDTH), BF16),
        scratch_shapes=_flash_scratch(tq, tk),
        compiler_params=_cparams(2),
        name="fox_attn",
    )(qf, smallt, kft, vf)


def _mix_kernel(x_ref, mod_ref, g_ref, ad_ref, af_ref, wgate_ref, wbd_ref, wbf_ref, wo_ref, o_ref):
    d = x_ref.shape[2]
    x = x_ref[0]
    shift = mod_ref[0, 3:4, :]
    scale = mod_ref[0, 4:5, :]
    gate = mod_ref[0, 5:6, :]
    h = _rms_adaln(x, g_ref[...], scale, shift).astype(BF16)
    ga = _dot(h, wgate_ref[:, 0:d])
    gb = _dot(h, wgate_ref[:, d:2 * d])
    y_dsa = _dot(ad_ref[0], wbd_ref[...])
    y_fox = _dot(af_ref[0], wbf_ref[...])
    merged = jax.nn.sigmoid(ga) * y_dsa + jax.nn.sigmoid(gb) * y_fox
    o_ref[0] = x + gate * _dot(merged.astype(BF16), wo_ref[...])


def _mix_out(x, mod, gain, a_dsa, a_fox, w_gate, w_br_dsa, w_br_fox, w_out, *, tm=512):
    b, s, d = x.shape
    row = lambda i, j: (i, j, 0)
    return pl.pallas_call(
        _mix_kernel,
        grid=(b, s // tm),
        in_specs=[pl.BlockSpec((1, tm, d), row),
                  pl.BlockSpec((1, N_MOD, d), lambda i, j: (i, 0, 0)),
                  _const_spec((1, d)),
                  pl.BlockSpec((1, tm, WIDTH), row), pl.BlockSpec((1, tm, WIDTH), row),
                  _const_spec(w_gate.shape), _const_spec(w_br_dsa.shape),
                  _const_spec(w_br_fox.shape), _const_spec(w_out.shape)],
        out_specs=pl.BlockSpec((1, tm, d), row),
        out_shape=jax.ShapeDtypeStruct((b, s, d), F32),
        compiler_params=_cparams(2),
        name="mix_out",
    )(x, mod, gain.reshape(1, d), a_dsa, a_fox, w_gate, w_br_dsa, w_br_fox, w_out)


def _layer(x, mod, norm1_g, ffn1_wg, ffn1_wu, ffn1_wd, norm2_g, w_in, b_forget,
           qn_dsa, kn_dsa, qn_fox, kn_fox, w_br_dsa, w_br_fox, w_out, norm3_g,
           ffn2_wg, ffn2_wu, ffn2_wd):
    d = x.shape[2]
    bf = lambda w: w.astype(BF16)
    x = _ffn(x, mod, norm1_g, bf(ffn1_wg), bf(ffn1_wu), bf(ffn1_wd), mod_base=0)

    n_main = 7 * WIDTH
    n_small = HEAD_DIM + 2 * N_HEADS
    w_main = bf(w_in[:, :n_main])
    w_small = jnp.zeros((d, LANES), BF16).at[:, :n_small].set(bf(w_in[:, n_main:n_main + n_small]))
    w_gate = bf(w_in[:, n_main + n_small:])
    b_small = jnp.zeros((1, LANES), F32).at[0, SM_F:SM_F + N_HEADS].set(b_forget)
    q_scale = LOG2E * HEAD_DIM ** -0.5
    q_gains = jnp.stack([jnp.tile(qn_dsa, N_HEADS), jnp.tile(qn_fox, N_HEADS)]) * q_scale
    k_gains = jnp.stack([kn_dsa, kn_fox], axis=1)

    qa, kat, va, qf, kft, vf, qi, small, smallt, kit = _in_proj(
        x, mod, norm2_g, w_main, w_small, b_small, q_gains, k_gains)
    a_dsa = _dsa_attn(qa, qi, small, kat, va, kit)
    a_fox = _fox_attn(qf, smallt, kft, vf)
    x = _mix_out(x, mod, norm2_g, a_dsa, a_fox, w_gate, bf(w_br_dsa), bf(w_br_fox), bf(w_out))
    return _ffn(x, mod, norm3_g, bf(ffn2_wg), bf(ffn2_wu), bf(ffn2_wd), mod_base=6)


def kernel(x, c, ada_w, ada_b, norm1_g, ffn1_wg, ffn1_wu, ffn1_wd, norm2_g, w_in, b_forget,
           qn_dsa, kn_dsa, qn_fox, kn_fox, w_br_dsa, w_br_fox, w_out, norm3_g,
           ffn2_wg, ffn2_wu, ffn2_wd):
    per_layer = (norm1_g, ffn1_wg, ffn1_wu, ffn1_wd, norm2_g, w_in, b_forget,
                 qn_dsa, kn_dsa, qn_fox, kn_fox, w_br_dsa, w_br_fox, w_out, norm3_g,
                 ffn2_wg, ffn2_wu, ffn2_wd)
    for l in range(ada_w.shape[0]):
        mod = _adaln_mod(c, ada_w[l], ada_b[l])
        x = _layer(x, mod, *(p[l] for p in per_layer))
    return x
```

```python
import functools

import jax
import jax.numpy as jnp
from jax import lax
from jax.experimental import pallas as pl
from jax.experimental.pallas import tpu as pltpu

F32 = jnp.float32
BF16 = jnp.bfloat16
I32 = jnp.int32

HEAD_DIM = 64
N_HEADS = 8
WIDTH = N_HEADS * HEAD_DIM
TOPK = 256
EPS = 1e-6
N_MOD = 9
LANES = 128
NEG = -1e30
BISECT_STEPS = 24
VMEM_LIMIT = 56 * 1024 * 1024

SM_WI = 64
SM_F = 72


def _cparams(n_axes):
    return pltpu.CompilerParams(
        dimension_semantics=("arbitrary",) * n_axes, vmem_limit_bytes=VMEM_LIMIT)


def _const_spec(shape):
    nd = len(shape)
    return pl.BlockSpec(shape, lambda *_: (0,) * nd)


def _dot(a, b):
    return jnp.dot(a, b, preferred_element_type=F32)


def _rms_adaln(x, gain, scale, shift):
    y = x * lax.rsqrt(jnp.mean(x * x, axis=-1, keepdims=True) + EPS)
    return (y * gain) * (1.0 + scale) + shift


def _mod_kernel(c_ref, w_ref, b_ref, o_ref):
    c = c_ref[...]
    a = c * jax.nn.sigmoid(c)
    o_ref[...] = jnp.dot(a, w_ref[...], preferred_element_type=F32,
                         precision=lax.Precision.HIGHEST) + b_ref[...]


def _adaln_mod(c, ada_w, ada_b):
    b, d = c.shape
    n = ada_w.shape[1]
    rows = 8
    tn = n // 8
    c_pad = jnp.zeros((rows, d), F32).at[:b].set(c)
    out = pl.pallas_call(
        _mod_kernel,
        grid=(n // tn,),
        in_specs=[pl.BlockSpec((rows, d), lambda j: (0, 0)),
                  pl.BlockSpec((d, tn), lambda j: (0, j)),
                  pl.BlockSpec((1, tn), lambda j: (0, j))],
        out_specs=pl.BlockSpec((rows, tn), lambda j: (0, j)),
        out_shape=jax.ShapeDtypeStruct((rows, n), F32),
        compiler_params=_cparams(1),
        name="adaln_mod",
    )(c_pad, ada_w, ada_b.reshape(1, n))
    return out[:b].reshape(b, N_MOD, d)


def _ffn_kernel(x_ref, mod_ref, g_ref, wg_ref, wu_ref, wd_ref, o_ref, *, mod_base, n_chunks):
    x = x_ref[0]
    shift = mod_ref[0, mod_base:mod_base + 1, :]
    scale = mod_ref[0, mod_base + 1:mod_base + 2, :]
    gate = mod_ref[0, mod_base + 2:mod_base + 3, :]
    h = _rms_adaln(x, g_ref[...], scale, shift).astype(BF16)
    dff = wg_ref.shape[1]
    ck = dff // n_chunks
    acc = jnp.zeros(x.shape, F32)
    for i in range(n_chunks):
        g = _dot(h, wg_ref[:, i * ck:(i + 1) * ck])
        u = _dot(h, wu_ref[:, i * ck:(i + 1) * ck])
        a = (g * jax.nn.sigmoid(g) * u).astype(BF16)
        acc = acc + _dot(a, wd_ref[i * ck:(i + 1) * ck, :])
    o_ref[0] = x + (0.5 * gate) * acc


def _ffn(x, mod, gain, wg, wu, wd, *, mod_base, tm=512):
    b, s, d = x.shape
    dff = wg.shape[1]
    kern = functools.partial(_ffn_kernel, mod_base=mod_base, n_chunks=2)
    return pl.pallas_call(
        kern,
        grid=(b, s // tm),
        in_specs=[pl.BlockSpec((1, tm, d), lambda i, j: (i, j, 0)),
                  pl.BlockSpec((1, N_MOD, d), lambda i, j: (i, 0, 0)),
                  _const_spec((1, d)),
                  _const_spec((d, dff)), _const_spec((d, dff)), _const_spec((dff, d))],
        out_specs=pl.BlockSpec((1, tm, d), lambda i, j: (i, j, 0)),
        out_shape=jax.ShapeDtypeStruct((b, s, d), F32),
        compiler_params=_cparams(2),
        name="ffn",
    )(x, mod, gain.reshape(1, d), wg, wu, wd)


def _split3_bf16(v):
    p1 = v.astype(BF16)
    r1 = v - p1.astype(F32)
    p2 = r1.astype(BF16)
    r2 = r1 - p2.astype(F32)
    return p1, p2, r2.astype(BF16)


def _proj_kernel(x_ref, mod_ref, g_ref, wm_ref, ws_ref, bsm_ref, qg_ref, kg_ref,
                 qa_ref, kat_ref, va_ref, qf_ref, kft_ref, vf_ref, qi_ref,
                 small_ref, smallt_ref, kit_ref, carry_ref):
    tm = x_ref.shape[1]
    x = x_ref[0]
    shift = mod_ref[0, 3:4, :]
    scale = mod_ref[0, 4:5, :]
    h = _rms_adaln(x, g_ref[...], scale, shift).astype(BF16)

    def z(i):
        return _dot(h, wm_ref[:, i * WIDTH:(i + 1) * WIDTH])

    r = lax.broadcasted_iota(I32, (WIDTH, WIDTH), 0) // HEAD_DIM
    c = lax.broadcasted_iota(I32, (WIDTH, WIDTH), 1) // HEAD_DIM
    avg = jnp.where(r == c, 1.0 / HEAD_DIM, 0.0).astype(BF16)

    def norm_q(q, gain_row):
        ms = _dot((q * q).astype(BF16), avg)
        return (q * lax.rsqrt(ms + EPS) * gain_row).astype(BF16)

    def norm_kt(k, gain_col):
        kt = k.T.reshape(N_HEADS, HEAD_DIM, tm)
        ms = jnp.mean(kt * kt, axis=1, keepdims=True)
        kt = kt * lax.rsqrt(ms + EPS) * gain_col[None]
        return kt.reshape(WIDTH, tm).astype(BF16)

    qa_ref[0] = norm_q(z(0), qg_ref[0:1, :])
    kat_ref[0] = norm_kt(z(1), kg_ref[:, 0:1])
    va_ref[0] = z(2).astype(BF16)
    qf_ref[0] = norm_q(z(3), qg_ref[1:2, :])
    kft_ref[0] = norm_kt(z(4), kg_ref[:, 1:2])
    vf_ref[0] = z(5).astype(BF16)
    qi_ref[0] = z(6) * (HEAD_DIM ** -0.5)

    zs = _dot(h, ws_ref[...])
    pre = zs + bsm_ref[...]
    logf = jnp.minimum(pre, 0.0) - jnp.log(1.0 + jnp.exp(-jnp.abs(pre)))

    @pl.when(pl.program_id(1) == 0)
    def _():
        carry_ref[...] = jnp.zeros_like(carry_ref)

    ri = lax.broadcasted_iota(I32, (tm, tm), 0)
    ci = lax.broadcasted_iota(I32, (tm, tm), 1)
    tri = jnp.where(ci <= ri, 1.0, 0.0).astype(BF16)
    p1, p2, p3 = _split3_bf16(logf)
    cum = (_dot(tri, p1) + _dot(tri, p2)) + _dot(tri, p3) + carry_ref[...]
    carry_ref[...] = cum[tm - 1:tm, :]

    lane = lax.broadcasted_iota(I32, (tm, LANES), 1)
    small = jnp.where(lane < SM_WI, zs,
                      jnp.where(lane < SM_F, zs * (N_HEADS ** -0.5), cum))
    small_ref[0] = small
    st = small.T
    smallt_ref[0] = st
    ki = st[0:HEAD_DIM, :]
    ki_hi = ki.astype(BF16)
    ki_lo = (ki - ki_hi.astype(F32)).astype(BF16)
    kit_ref[0] = jnp.concatenate([ki_hi, ki_hi, ki_lo], axis=0)


def _in_proj(x, mod, gain, w_main, w_small, b_small, q_gains, k_gains, *, tm=512):
    b, s, d = x.shape
    row = lambda i, j: (i, j, 0)
    col = lambda i, j: (i, 0, j)
    sd = jax.ShapeDtypeStruct
    out_shape = [sd((b, s, WIDTH), BF16), sd((b, WIDTH, s), BF16), sd((b, s, WIDTH), BF16),
                 sd((b, s, WIDTH), BF16), sd((b, WIDTH, s), BF16), sd((b, s, WIDTH), BF16),
                 sd((b, s, WIDTH), F32),
                 sd((b, s, LANES), F32), sd((b, LANES, s), F32), sd((b, 3 * HEAD_DIM, s), BF16)]
    rspec = pl.BlockSpec((1, tm, WIDTH), row)
    cspec = pl.BlockSpec((1, WIDTH, tm), col)
    out_specs = [rspec, cspec, rspec, rspec, cspec, rspec, rspec,
                 pl.BlockSpec((1, tm, LANES), row), pl.BlockSpec((1, LANES, tm), col),
                 pl.BlockSpec((1, 3 * HEAD_DIM, tm), col)]
    return pl.pallas_call(
        _proj_kernel,
        grid=(b, s // tm),
        in_specs=[pl.BlockSpec((1, tm, d), row),
                  pl.BlockSpec((1, N_MOD, d), lambda i, j: (i, 0, 0)),
                  _const_spec((1, d)),
                  _const_spec(w_main.shape), _const_spec(w_small.shape),
                  _const_spec((1, LANES)), _const_spec((2, WIDTH)), _const_spec((HEAD_DIM, 2))],
        out_specs=out_specs,
        out_shape=out_shape,
        scratch_shapes=[pltpu.VMEM((1, LANES), F32)],
        compiler_params=_cparams(2),
        name="in_proj",
    )(x, mod, gain.reshape(1, d), w_main, w_small, b_small, q_gains, k_gains)


FLASH_ROWS = 64
COUNT_ROWS = 128
LOG2E = 1.4426950408889634


def _flash_scratch(tq, tk):
    return [pltpu.VMEM((N_HEADS, tq, LANES), F32), pltpu.VMEM((N_HEADS, tq, LANES), F32),
            pltpu.VMEM((N_HEADS, tq, LANES), F32), pltpu.VMEM((N_HEADS, tq, tk), BF16)]


def _flash_reset(m_ref, l_ref, acc_ref):
    m_ref[...] = jnp.full(m_ref.shape, NEG, F32)
    l_ref[...] = jnp.zeros(l_ref.shape, F32)
    acc_ref[...] = jnp.zeros(acc_ref.shape, F32)


def _flash_step(q, kt_ref, v_ref, off, chunk_logits, m_ref, l_ref, acc_ref, p_ref):
    tq, tk = p_ref.shape[1:]
    n_sub = tk // LANES
    for h in range(N_HEADS):
        pair = h // 2
        q_h = q[:, h * HEAD_DIM:(h + 1) * HEAD_DIM]
        kt = kt_ref[0, h * HEAD_DIM:(h + 1) * HEAD_DIM, pl.ds(off, tk)]
        for r0 in range(0, tq, FLASH_ROWS):
            rows = pl.ds(r0, FLASH_ROWS)
            s = _dot(q_h[r0:r0 + FLASH_ROWS], kt)
            sc = [chunk_logits(h, r0, c, s[:, c * LANES:(c + 1) * LANES]) for c in range(n_sub)]
            mx = sc[0]
            for x in sc[1:]:
                mx = jnp.maximum(mx, x)
            m_old = m_ref[h, rows, :]
            m_new = jnp.maximum(m_old, jnp.max(mx, axis=1, keepdims=True))
            alpha = jnp.exp2(m_old - m_new)
            ps = [jnp.exp2(x - m_new) for x in sc]
            lsum = ps[0]
            for x in ps[1:]:
                lsum = lsum + x
            m_ref[h, rows, :] = m_new
            l_ref[h, rows, :] = alpha * l_ref[h, rows, :] + lsum
            acc_ref[h, rows, :] = alpha * acc_ref[h, rows, :]
            p_ref[h, rows, :] = jnp.concatenate(ps, axis=1).astype(BF16)
        v = v_ref[0, pl.ds(off, tk), pair * LANES:(pair + 1) * LANES]
        acc_ref[h] = acc_ref[h] + _dot(p_ref[h], v)


def _flash_finish(l_ref, acc_ref, o_ref):
    for pair in range(N_HEADS // 2):
        out = [acc_ref[h] / jnp.sum(l_ref[h], axis=1, keepdims=True) for h in (2 * pair, 2 * pair + 1)]
        lane = lax.broadcasted_iota(I32, out[0].shape, 1)
        o_ref[0, :, pair * LANES:(pair + 1) * LANES] = jnp.where(
            lane < HEAD_DIM, out[0], out[1]).astype(BF16)


def _dsa_kernel(qa_ref, qi_ref, small_ref, kat_ref, va_ref, kit_ref, o_ref, sc_ref, icut_ref,
                m_ref, l_ref, acc_ref, p_ref, *, tq, tk, topk):
    s_len = kat_ref.shape[2]
    t0 = pl.program_id(1) * tq
    nkb = (t0 + tq + tk - 1) // tk
    n_sub = tk // LANES
    row = lax.broadcasted_iota(I32, (tq, LANES), 0) + t0
    lane = lax.broadcasted_iota(I32, (COUNT_ROWS, LANES), 1)
    rep = lambda col: jnp.broadcast_to(col, (col.shape[0], LANES))

    wi = small_ref[0][:, SM_WI:SM_WI + N_HEADS]
    qi = qi_ref[0]
    lhs = []
    for h in range(N_HEADS):
        qh = qi[:, h * HEAD_DIM:(h + 1) * HEAD_DIM]
        hi = qh.astype(BF16)
        lo = (qh - hi.astype(F32)).astype(BF16)
        lhs.append(jnp.concatenate([hi, lo, hi], axis=1))
    lane_q = lax.broadcasted_iota(I32, (tq, LANES), 1)

    def score_body(kb, carry):
        mn, mx = carry
        off = pl.multiple_of(kb * tk, tk)
        kib = kit_ref[0, :, pl.ds(off, tk)]
        acc = jnp.zeros((tq, tk), F32)
        for h in range(N_HEADS):
            acc = acc + jnp.maximum(_dot(lhs[h], kib), 0.0) * wi[:, h:h + 1]
        for i in range(n_sub):
            a = acc[:, i * LANES:(i + 1) * LANES]
            causal = lane_q + (off + i * LANES) <= row
            lowered = jnp.where(causal, a, -jnp.inf)
            sc_ref[kb * n_sub + i] = lowered
            mn = jnp.minimum(mn, jnp.where(causal, a, jnp.inf))
            mx = jnp.maximum(mx, lowered)
        return mn, mx

    mn, mx = lax.fori_loop(0, nkb, score_body, (jnp.full((tq, LANES), jnp.inf, F32),
                                                jnp.full((tq, LANES), -jnp.inf, F32)))

    slabs = range(0, tq, COUNT_ROWS)

    def scan(step, init, *row_args, settled=None):
        outs = []
        for k, r0 in enumerate(slabs):
            args = [a[r0:r0 + COUNT_ROWS] for a in row_args]
            start = jnp.full((COUNT_ROWS, LANES), init, F32)

            def body(kb, acc, r0=r0, args=args):
                off = pl.multiple_of(kb * tk, tk)
                for i in range(n_sub):
                    blk = sc_ref[kb * n_sub + i, pl.ds(r0, COUNT_ROWS), :]
                    acc = step(acc, blk, lane + (off + i * LANES), *args)
                return acc

            run = functools.partial(lax.fori_loop, 0, nkb, body, start)
            outs.append(run() if settled is None
                        else lax.cond(settled[k] > 0.5, lambda start=start: start, run))
        return jnp.concatenate(outs, axis=0)

    def count(pred, *row_args, settled=None):
        c = scan(lambda acc, blk, pos, *a: acc + jnp.where(pred(blk, pos, *a), 1.0, 0.0),
                 0.0, *row_args, settled=settled)
        return rep(jnp.sum(c, axis=1, keepdims=True))

    def below_max(bound, settled):
        m = scan(lambda acc, blk, pos, b: jnp.maximum(acc, jnp.where(blk < b, blk, -jnp.inf)),
                 -jnp.inf, bound, settled=settled)
        return rep(jnp.max(m, axis=1, keepdims=True))

    def slab_settled(state):
        return tuple(jnp.min(state[r0:r0 + COUNT_ROWS]) for r0 in slabs)

    kf = float(topk)
    few = row < topk
    rmin = rep(jnp.min(mn, axis=1, keepdims=True))
    rmax = rep(jnp.max(mx, axis=1, keepdims=True))
    above = jnp.where(rmax > 0.0, 2.0 * rmax, 0.5 * rmax) + 1.0
    lo0 = jnp.where(few, 0.0, rmin)
    hi0 = jnp.where(few, 0.0, above)
    state0 = jnp.where(few, 1.0, 0.0)

    def search_cond(st):
        it, settled = st[0], st[1]
        return (it < BISECT_STEPS) & (functools.reduce(jnp.minimum, settled) < 0.5)

    def search_body(st):
        it, settled, lo, hi, state = st
        mid = 0.5 * lo + 0.5 * hi
        cnt = count(lambda blk, pos, m: blk >= m, mid, settled=settled)
        active = state == 0.0
        ge = cnt >= kf
        lo = jnp.where(active, jnp.where(ge, mid, lo), lo)
        hi = jnp.where(active, jnp.where(ge, hi, mid), hi)
        state = jnp.where(active, jnp.where(cnt == kf, 1.0, 0.0), state)
        return it + 1, slab_settled(state), lo, hi, state

    _, settled, lo, hi, state = lax.while_loop(
        search_cond, search_body, (jnp.int32(0), slab_settled(state0), lo0, hi0, state0))
    thr = jnp.where(few, -jnp.inf, lo)

    def snap_body(st):
        settled, hi, thr, state = st
        cand = below_max(hi, settled)
        cnt = count(lambda blk, pos, c: blk >= c, cand, settled=settled)
        active = state == 0.0
        found = cnt >= kf
        thr = jnp.where(active, jnp.where(found, cand, thr), thr)
        hi = jnp.where(active, jnp.where(found, hi, cand), hi)
        state = jnp.where(active, jnp.where(found, 2.0, 0.0), state)
        return slab_settled(state), hi, thr, state

    _, _, thr, state = lax.while_loop(
        lambda st: functools.reduce(jnp.minimum, st[0]) < 0.5, snap_body, (settled, hi, thr, state))

    icut_ref[...] = jnp.full(icut_ref.shape, s_len, I32)
    tied = state == 2.0

    @pl.when(jnp.max(state) > 1.5)
    def _():
        need = kf - count(lambda blk, pos, t: blk > t, thr)
        n_bits = max(1, (s_len - 1).bit_length())

        def idx_body(i, cut):
            cand = cut | jnp.left_shift(jnp.int32(1), n_bits - 1 - i)
            cnt = count(lambda blk, pos, t, cd: (blk == t) & (pos < cd), thr, cand)
            return jnp.where(cnt < need, cand, cut)

        cut = lax.fori_loop(0, n_bits, idx_body, jnp.zeros((tq, LANES), I32))
        icut_ref[...] = jnp.where(tied, cut, s_len)

    icut = icut_ref[...]

    for r0 in range(0, tq, COUNT_ROWS):
        rows = pl.ds(r0, COUNT_ROWS)
        thr_r, icut_r, row_r = (a[r0:r0 + COUNT_ROWS] for a in (thr, icut, row))

        def bias_body(kb, _, rows=rows, thr_r=thr_r, icut_r=icut_r, row_r=row_r):
            off = pl.multiple_of(kb * tk, tk)
            for i in range(n_sub):
                blk = sc_ref[kb * n_sub + i, rows, :]
                pos = lane + (off + i * LANES)
                sel = (blk > thr_r) | ((blk == thr_r) & (pos <= icut_r))
                sc_ref[kb * n_sub + i, rows, :] = jnp.where(sel & (pos <= row_r), 0.0, NEG)
            return 0

        lax.fori_loop(0, nkb, bias_body, 0)

    qa = qa_ref[0]
    kpos = lax.broadcasted_iota(I32, (1, tk), 1)
    slopes = [LOG2E * 2.0 ** (-8.0 * (h + 1) / N_HEADS) for h in range(N_HEADS)]
    _flash_reset(m_ref, l_ref, acc_ref)

    def attn_body(kb, _):
        off = pl.multiple_of(kb * tk, tk)
        rel = (kpos + (off - t0)).astype(F32)

        def chunk_logits(h, r0, c, x):
            mask = sc_ref[kb * n_sub + c, pl.ds(r0, FLASH_ROWS), :]
            return x + (mask + slopes[h] * rel[:, c * LANES:(c + 1) * LANES])

        _flash_step(qa, kat_ref, va_ref, off, chunk_logits, m_ref, l_ref, acc_ref, p_ref)
        return 0

    lax.fori_loop(0, nkb, attn_body, 0)
    _flash_finish(l_ref, acc_ref, o_ref)


def _dsa_attn(qa, qi, small, kat, va, kit, *, tq=512, tk=512):
    b, s, _ = qa.shape
    row = lambda i, j: (i, j, 0)
    whole = lambda i, j: (i, 0, 0)
    once = pl.Buffered(1)
    kern = functools.partial(_dsa_kernel, tq=tq, tk=tk, topk=min(TOPK, s // 4))
    return pl.pallas_call(
        kern,
        grid=(b, s // tq),
        in_specs=[pl.BlockSpec((1, tq, WIDTH), row), pl.BlockSpec((1, tq, WIDTH), row),
                  pl.BlockSpec((1, tq, LANES), row),
                  pl.BlockSpec((1, WIDTH, s), whole, pipeline_mode=once),
                  pl.BlockSpec((1, s, WIDTH), whole, pipeline_mode=once),
                  pl.BlockSpec((1, 3 * HEAD_DIM, s), whole, pipeline_mode=once)],
        out_specs=pl.BlockSpec((1, tq, WIDTH), row),
        out_shape=jax.ShapeDtypeStruct((b, s, WIDTH), BF16),
        scratch_shapes=[pltpu.VMEM((s // LANES, tq, LANES), F32), pltpu.VMEM((tq, LANES), I32)]
        + _flash_scratch(tq, tk),
        compiler_params=_cparams(2),
        name="dsa_attn",
    )(qa, qi, small, kat, va, kit)


def _fox_kernel(qf_ref, smallt_ref, kft_ref, vf_ref, o_ref, m_ref, l_ref, acc_ref, p_ref, *, tq, tk):
    t0 = pl.program_id(1) * tq
    n_full = t0 // tk
    nkb = (t0 + tq + tk - 1) // tk
    row = lax.broadcasted_iota(I32, (FLASH_ROWS, LANES), 0) + t0
    col = lax.broadcasted_iota(I32, (FLASH_ROWS, LANES), 1)
    qf = qf_ref[0]
    f_t0 = smallt_ref[0, SM_F:SM_F + N_HEADS, pl.ds(pl.multiple_of(t0, LANES), LANES)][:, 0:1]
    _flash_reset(m_ref, l_ref, acc_ref)

    def attn_body(kb, _, causal):
        off = pl.multiple_of(kb * tk, tk)
        decay = LOG2E * (f_t0 - smallt_ref[0, SM_F:SM_F + N_HEADS, pl.ds(off, tk)])

        def chunk_logits(h, r0, c, x):
            x = x + decay[h:h + 1, c * LANES:(c + 1) * LANES]
            if causal:
                x = jnp.where(col + (off + c * LANES) <= row + r0, x, NEG)
            return x

        _flash_step(qf, kft_ref, vf_ref, off, chunk_logits, m_ref, l_ref, acc_ref, p_ref)
        return 0

    lax.fori_loop(0, n_full, functools.partial(attn_body, causal=False), 0)
    lax.fori_loop(n_full, nkb, functools.partial(attn_body, causal=True), 0)
    _flash_finish(l_ref, acc_ref, o_ref)


def _fox_attn(qf, smallt, kft, vf, *, tq=512, tk=512):
    b, s, _ = qf.shape
    row = lambda i, j: (i, j, 0)
    whole = lambda i, j: (i, 0, 0)
    once = pl.Buffered(1)
    kern = functools.partial(_fox_kernel, tq=tq, tk=tk)
    return pl.pallas_call(
        kern,
        grid=(b, s // tq),
        in_specs=[pl.BlockSpec((1, tq, WIDTH), row),
                  pl.BlockSpec((1, LANES, s), whole, pipeline_mode=once),
                  pl.BlockSpec((1, WIDTH, s), whole, pipeline_mode=once),
                  pl.BlockSpec((1, s, WIDTH), whole, pipeline_mode=once)],
        out_specs=pl.BlockSpec((1, tq, WIDTH), row),
        out_shape=jax.ShapeDtypeStruct((b, s, WIDTH), BF16),
        scratch_shapes=_flash_scratch(tq, tk),
        compiler_params=_cparams(2),
        name="fox_attn",
    )(qf, smallt, kft, vf)


def _mix_kernel(x_ref, mod_ref, g_ref, ad_ref, af_ref, wgate_ref, wbd_ref, wbf_ref, wo_ref, o_ref):
    d = x_ref.shape[2]
    x = x_ref[0]
    shift = mod_ref[0, 3:4, :]
    scale = mod_ref[0, 4:5, :]
    gate = mod_ref[0, 5:6, :]
    h = _rms_adaln(x, g_ref[...], scale, shift).astype(BF16)
    ga = _dot(h, wgate_ref[:, 0:d])
    gb = _dot(h, wgate_ref[:, d:2 * d])
    y_dsa = _dot(ad_ref[0], wbd_ref[...])
    y_fox = _dot(af_ref[0], wbf_ref[...])
    merged = jax.nn.sigmoid(ga) * y_dsa + jax.nn.sigmoid(gb) * y_fox
    o_ref[0] = x + gate * _dot(merged.astype(BF16), wo_ref[...])


def _mix_out(x, mod, gain, a_dsa, a_fox, w_gate, w_br_dsa, w_br_fox, w_out, *, tm=512):
    b, s, d = x.shape
    row = lambda i, j: (i, j, 0)
    return pl.pallas_call(
        _mix_kernel,
        grid=(b, s // tm),
        in_specs=[pl.BlockSpec((1, tm, d), row),
                  pl.BlockSpec((1, N_MOD, d), lambda i, j: (i, 0, 0)),
                  _const_spec((1, d)),
                  pl.BlockSpec((1, tm, WIDTH), row), pl.BlockSpec((1, tm, WIDTH), row),
                  _const_spec(w_gate.shape), _const_spec(w_br_dsa.shape),
                  _const_spec(w_br_fox.shape), _const_spec(w_out.shape)],
        out_specs=pl.BlockSpec((1, tm, d), row),
        out_shape=jax.ShapeDtypeStruct((b, s, d), F32),
        compiler_params=_cparams(2),
        name="mix_out",
    )(x, mod, gain.reshape(1, d), a_dsa, a_fox, w_gate, w_br_dsa, w_br_fox, w_out)


def _layer(x, mod, norm1_g, ffn1_wg, ffn1_wu, ffn1_wd, norm2_g, w_in, b_forget,
           qn_dsa, kn_dsa, qn_fox, kn_fox, w_br_dsa, w_br_fox, w_out, norm3_g,
           ffn2_wg, ffn2_wu, ffn2_wd):
    d = x.shape[2]
    bf = lambda w: w.astype(BF16)
    x = _ffn(x, mod, norm1_g, bf(ffn1_wg), bf(ffn1_wu), bf(ffn1_wd), mod_base=0)

    n_main = 7 * WIDTH
    n_small = HEAD_DIM + 2 * N_HEADS
    w_main = bf(w_in[:, :n_main])
    w_small = jnp.zeros((d, LANES), BF16).at[:, :n_small].set(bf(w_in[:, n_main:n_main + n_small]))
    w_gate = bf(w_in[:, n_main + n_small:])
    b_small = jnp.zeros((1, LANES), F32).at[0, SM_F:SM_F + N_HEADS].set(b_forget)
    q_scale = LOG2E * HEAD_DIM ** -0.5
    q_gains = jnp.stack([jnp.tile(qn_dsa, N_HEADS), jnp.tile(qn_fox, N_HEADS)]) * q_scale
    k_gains = jnp.stack([kn_dsa, kn_fox], axis=1)

    qa, kat, va, qf, kft, vf, qi, small, smallt, kit = _in_proj(
        x, mod, norm2_g, w_main, w_small, b_small, q_gains, k_gains)
    a_dsa = _dsa_attn(qa, qi, small, kat, va, kit)
    a_fox = _fox_attn(qf, smallt, kft, vf)
    x = _mix_out(x, mod, norm2_g, a_dsa, a_fox, w_gate, bf(w_br_dsa), bf(w_br_fox), bf(w_out))
    return _ffn(x, mod, norm3_g, bf(ffn2_wg), bf(ffn2_wu), bf(ffn2_wd), mod_base=6)


def kernel(x, c, ada_w, ada_b, norm1_g, ffn1_wg, ffn1_wu, ffn1_wd, norm2_g, w_in, b_forget,
           qn_dsa, kn_dsa, qn_fox, kn_fox, w_br_dsa, w_br_fox, w_out, norm3_g,
           ffn2_wg, ffn2_wu, ffn2_wd):
    per_layer = (norm1_g, ffn1_wg, ffn1_wu, ffn1_wd, norm2_g, w_in, b_forget,
                 qn_dsa, kn_dsa, qn_fox, kn_fox, w_br_dsa, w_br_fox, w_out, norm3_g,
                 ffn2_wg, ffn2_wu, ffn2_wd)
    for l in range(ada_w.shape[0]):
        mod = _adaln_mod(c, ada_w[l], ada_b[l])
        x = _layer(x, mod, *(p[l] for p in per_layer))
    return x
```

```python
import functools

import jax
import jax.numpy as jnp
from jax import lax
from jax.experimental import pallas as pl
from jax.experimental.pallas import tpu as pltpu

F32 = jnp.float32
BF16 = jnp.bfloat16
I32 = jnp.int32

HEAD_DIM = 64
N_HEADS = 8
WIDTH = N_HEADS * HEAD_DIM
TOPK = 256
EPS = 1e-6
N_MOD = 9
LANES = 128
NEG = -1e30
BISECT_STEPS = 24
VMEM_LIMIT = 56 * 1024 * 1024

SM_WI = 64
SM_F = 72


def _cparams(n_axes):
    return pltpu.CompilerParams(
        dimension_semantics=("arbitrary",) * n_axes, vmem_limit_bytes=VMEM_LIMIT)


def _const_spec(shape):
    nd = len(shape)
    return pl.BlockSpec(shape, lambda *_: (0,) * nd)


def _dot(a, b):
    return jnp.dot(a, b, preferred_element_type=F32)


def _rms_adaln(x, gain, scale, shift):
    y = x * lax.rsqrt(jnp.mean(x * x, axis=-1, keepdims=True) + EPS)
    return (y * gain) * (1.0 + scale) + shift


def _mod_kernel(c_ref, w_ref, b_ref, o_ref):
    c = c_ref[...]
    a = c * jax.nn.sigmoid(c)
    o_ref[...] = jnp.dot(a, w_ref[...], preferred_element_type=F32,
                         precision=lax.Precision.HIGHEST) + b_ref[...]


def _adaln_mod(c, ada_w, ada_b):
    b, d = c.shape
    n = ada_w.shape[1]
    rows = 8
    tn = n // 8
    c_pad = jnp.zeros((rows, d), F32).at[:b].set(c)
    out = pl.pallas_call(
        _mod_kernel,
        grid=(n // tn,),
        in_specs=[pl.BlockSpec((rows, d), lambda j: (0, 0)),
                  pl.BlockSpec((d, tn), lambda j: (0, j)),
                  pl.BlockSpec((1, tn), lambda j: (0, j))],
        out_specs=pl.BlockSpec((rows, tn), lambda j: (0, j)),
        out_shape=jax.ShapeDtypeStruct((rows, n), F32),
        compiler_params=_cparams(1),
        name="adaln_mod",
    )(c_pad, ada_w, ada_b.reshape(1, n))
    return out[:b].reshape(b, N_MOD, d)


def _ffn_kernel(x_ref, mod_ref, g_ref, wg_ref, wu_ref, wd_ref, o_ref, *, mod_base, n_chunks):
    x = x_ref[0]
    shift = mod_ref[0, mod_base:mod_base + 1, :]
    scale = mod_ref[0, mod_base + 1:mod_base + 2, :]
    gate = mod_ref[0, mod_base + 2:mod_base + 3, :]
    h = _rms_adaln(x, g_ref[...], scale, shift).astype(BF16)
    dff = wg_ref.shape[1]
    ck = dff // n_chunks
    acc = jnp.zeros(x.shape, F32)
    for i in range(n_chunks):
        g = _dot(h, wg_ref[:, i * ck:(i + 1) * ck])
        u = _dot(h, wu_ref[:, i * ck:(i + 1) * ck])
        a = (g * jax.nn.sigmoid(g) * u).astype(BF16)
        acc = acc + _dot(a, wd_ref[i * ck:(i + 1) * ck, :])
    o_ref[0] = x + (0.5 * gate) * acc


def _ffn(x, mod, gain, wg, wu, wd, *, mod_base, tm=512):
    b, s, d = x.shape
    dff = wg.shape[1]
    kern = functools.partial(_ffn_kernel, mod_base=mod_base, n_chunks=2)
    return pl.pallas_call(
        kern,
        grid=(b, s // tm),
        in_specs=[pl.BlockSpec((1, tm, d), lambda i, j: (i, j, 0)),
                  pl.BlockSpec((1, N_MOD, d), lambda i, j: (i, 0, 0)),
                  _const_spec((1, d)),
                  _const_spec((d, dff)), _const_spec((d, dff)), _const_spec((dff, d))],
        out_specs=pl.BlockSpec((1, tm, d), lambda i, j: (i, j, 0)),
        out_shape=jax.ShapeDtypeStruct((b, s, d), F32),
        compiler_params=_cparams(2),
        name="ffn",
    )(x, mod, gain.reshape(1, d), wg, wu, wd)


def _split3_bf16(v):
    p1 = v.astype(BF16)
    r1 = v - p1.astype(F32)
    p2 = r1.astype(BF16)
    r2 = r1 - p2.astype(F32)
    return p1, p2, r2.astype(BF16)


def _proj_kernel(x_ref, mod_ref, g_ref, wm_ref, ws_ref, bsm_ref, qg_ref, kg_ref,
                 qa_ref, kat_ref, va_ref, qf_ref, kft_ref, vf_ref, qi_ref,
                 small_ref, smallt_ref, kit_ref, carry_ref):
    tm = x_ref.shape[1]
    x = x_ref[0]
    shift = mod_ref[0, 3:4, :]
    scale = mod_ref[0, 4:5, :]
    h = _rms_adaln(x, g_ref[...], scale, shift).astype(BF16)

    def z(i):
        return _dot(h, wm_ref[:, i * WIDTH:(i + 1) * WIDTH])

    r = lax.broadcasted_iota(I32, (WIDTH, WIDTH), 0) // HEAD_DIM
    c = lax.broadcasted_iota(I32, (WIDTH, WIDTH), 1) // HEAD_DIM
    avg = jnp.where(r == c, 1.0 / HEAD_DIM, 0.0).astype(BF16)

    def norm_q(q, gain_row):
        ms = _dot((q * q).astype(BF16), avg)
        return (q * lax.rsqrt(ms + EPS) * gain_row).astype(BF16)

    def norm_kt(k, gain_col):
        kt = k.T.reshape(N_HEADS, HEAD_DIM, tm)
        ms = jnp.mean(kt * kt, axis=1, keepdims=True)
        kt = kt * lax.rsqrt(ms + EPS) * gain_col[None]
        return kt.reshape(WIDTH, tm).astype(BF16)

    qa_ref[0] = norm_q(z(0), qg_ref[0:1, :])
    kat_ref[0] = norm_kt(z(1), kg_ref[:, 0:1])
    va_ref[0] = z(2).astype(BF16)
    qf_ref[0] = norm_q(z(3), qg_ref[1:2, :])
    kft_ref[0] = norm_kt(z(4), kg_ref[:, 1:2])
    vf_ref[0] = z(5).astype(BF16)
    qi_ref[0] = z(6) * (HEAD_DIM ** -0.5)

    zs = _dot(h, ws_ref[...])
    pre = zs + bsm_ref[...]
    logf = jnp.minimum(pre, 0.0) - jnp.log(1.0 + jnp.exp(-jnp.abs(pre)))

    @pl.when(pl.program_id(1) == 0)
    def _():
        carry_ref[...] = jnp.zeros_like(carry_ref)

    ri = lax.broadcasted_iota(I32, (tm, tm), 0)
    ci = lax.broadcasted_iota(I32, (tm, tm), 1)
    tri = jnp.where(ci <= ri, 1.0, 0.0).astype(BF16)
    p1, p2, p3 = _split3_bf16(logf)
    cum = (_dot(tri, p1) + _dot(tri, p2)) + _dot(tri, p3) + carry_ref[...]
    carry_ref[...] = cum[tm - 1:tm, :]

    lane = lax.broadcasted_iota(I32, (tm, LANES), 1)
    small = jnp.where(lane < SM_WI, zs,
                      jnp.where(lane < SM_F, zs * (N_HEADS ** -0.5), cum))
    small_ref[0] = small
    st = small.T
    smallt_ref[0] = st
    ki = st[0:HEAD_DIM, :]
    ki_hi = ki.astype(BF16)
    ki_lo = (ki - ki_hi.astype(F32)).astype(BF16)
    kit_ref[0] = jnp.concatenate([ki_hi, ki_hi, ki_lo], axis=0)


def _in_proj(x, mod, gain, w_main, w_small, b_small, q_gains, k_gains, *, tm=512):
    b, s, d = x.shape
    row = lambda i, j: (i, j, 0)
    col = lambda i, j: (i, 0, j)
    sd = jax.ShapeDtypeStruct
    out_shape = [sd((b, s, WIDTH), BF16), sd((b, WIDTH, s), BF16), sd((b, s, WIDTH), BF16),
                 sd((b, s, WIDTH), BF16), sd((b, WIDTH, s), BF16), sd((b, s, WIDTH), BF16),
                 sd((b, s, WIDTH), F32),
                 sd((b, s, LANES), F32), sd((b, LANES, s), F32), sd((b, 3 * HEAD_DIM, s), BF16)]
    rspec = pl.BlockSpec((1, tm, WIDTH), row)
    cspec = pl.BlockSpec((1, WIDTH, tm), col)
    out_specs = [rspec, cspec, rspec, rspec, cspec, rspec, rspec,
                 pl.BlockSpec((1, tm, LANES), row), pl.BlockSpec((1, LANES, tm), col),
                 pl.BlockSpec((1, 3 * HEAD_DIM, tm), col)]
    return pl.pallas_call(
        _proj_kernel,
        grid=(b, s // tm),
        in_specs=[pl.BlockSpec((1, tm, d), row),
                  pl.BlockSpec((1, N_MOD, d), lambda i, j: (i, 0, 0)),
                  _const_spec((1, d)),
                  _const_spec(w_main.shape), _const_spec(w_small.shape),
                  _const_spec((1, LANES)), _const_spec((2, WIDTH)), _const_spec((HEAD_DIM, 2))],
        out_specs=out_specs,
        out_shape=out_shape,
        scratch_shapes=[pltpu.VMEM((1, LANES), F32)],
        compiler_params=_cparams(2),
        name="in_proj",
    )(x, mod, gain.reshape(1, d), w_main, w_small, b_small, q_gains, k_gains)


FLASH_ROWS = 64
COUNT_ROWS = 128
LOG2E = 1.4426950408889634
FOX_SAFE_BOUND = 50.0


def _flash_scratch(tq, tk):
    return [pltpu.VMEM((N_HEADS, tq, LANES), F32), pltpu.VMEM((N_HEADS, tq, LANES), F32),
            pltpu.VMEM((N_HEADS, tq, LANES), F32), pltpu.VMEM((N_HEADS, tq, tk), BF16)]


def _flash_reset(m_ref, l_ref, acc_ref):
    m_ref[...] = jnp.full(m_ref.shape, NEG, F32)
    l_ref[...] = jnp.zeros(l_ref.shape, F32)
    acc_ref[...] = jnp.zeros(acc_ref.shape, F32)


def _flash_step(q, kt_ref, v_ref, off, chunk_logits, m_ref, l_ref, acc_ref, p_ref, online=True):
    tq, tk = p_ref.shape[1:]
    n_sub = tk // LANES
    for h in range(N_HEADS):
        pair = h // 2
        q_h = q[:, h * HEAD_DIM:(h + 1) * HEAD_DIM]
        kt = kt_ref[0, h * HEAD_DIM:(h + 1) * HEAD_DIM, pl.ds(off, tk)]
        for r0 in range(0, tq, FLASH_ROWS):
            rows = pl.ds(r0, FLASH_ROWS)
            s = _dot(q_h[r0:r0 + FLASH_ROWS], kt)
            sc = [chunk_logits(h, r0, c, s[:, c * LANES:(c + 1) * LANES]) for c in range(n_sub)]
            m_row = m_ref[h, rows, :]
            if online:
                mx = sc[0]
                for x in sc[1:]:
                    mx = jnp.maximum(mx, x)
                m_old = m_row
                m_row = jnp.maximum(m_old, jnp.max(mx, axis=1, keepdims=True))
                alpha = jnp.exp2(m_old - m_row)
                m_ref[h, rows, :] = m_row
                acc_ref[h, rows, :] = alpha * acc_ref[h, rows, :]
            ps = [jnp.exp2(x - m_row) for x in sc]
            lsum = ps[0]
            for x in ps[1:]:
                lsum = lsum + x
            l_old = l_ref[h, rows, :]
            l_ref[h, rows, :] = (alpha * l_old if online else l_old) + lsum
            p_ref[h, rows, :] = jnp.concatenate(ps, axis=1).astype(BF16)
        v = v_ref[0, pl.ds(off, tk), pair * LANES:(pair + 1) * LANES]
        acc_ref[h] = acc_ref[h] + _dot(p_ref[h], v)


def _flash_finish(l_ref, acc_ref, o_ref):
    for pair in range(N_HEADS // 2):
        out = [acc_ref[h] / jnp.sum(l_ref[h], axis=1, keepdims=True) for h in (2 * pair, 2 * pair + 1)]
        lane = lax.broadcasted_iota(I32, out[0].shape, 1)
        o_ref[0, :, pair * LANES:(pair + 1) * LANES] = jnp.where(
            lane < HEAD_DIM, out[0], out[1]).astype(BF16)


def _dsa_kernel(qa_ref, qi_ref, small_ref, kat_ref, va_ref, kit_ref, o_ref, sc_ref, icut_ref,
                m_ref, l_ref, acc_ref, p_ref, *, tq, tk, topk):
    s_len = kat_ref.shape[2]
    t0 = pl.program_id(1) * tq
    nkb = (t0 + tq + tk - 1) // tk
    n_sub = tk // LANES
    row = lax.broadcasted_iota(I32, (tq, LANES), 0) + t0
    lane = lax.broadcasted_iota(I32, (COUNT_ROWS, LANES), 1)
    rep = lambda col: jnp.broadcast_to(col, (col.shape[0], LANES))

    wi = small_ref[0][:, SM_WI:SM_WI + N_HEADS]
    qi = qi_ref[0]
    lhs = []
    for h in range(N_HEADS):
        qh = qi[:, h * HEAD_DIM:(h + 1) * HEAD_DIM]
        hi = qh.astype(BF16)
        lo = (qh - hi.astype(F32)).astype(BF16)
        lhs.append(jnp.concatenate([hi, lo, hi], axis=1))
    lane_q = lax.broadcasted_iota(I32, (tq, LANES), 1)

    def score_body(kb, carry):
        mn, mx = carry
        off = pl.multiple_of(kb * tk, tk)
        kib = kit_ref[0, :, pl.ds(off, tk)]
        acc = jnp.zeros((tq, tk), F32)
        for h in range(N_HEADS):
            acc = acc + jnp.maximum(_dot(lhs[h], kib), 0.0) * wi[:, h:h + 1]
        for i in range(n_sub):
            a = acc[:, i * LANES:(i + 1) * LANES]
            causal = lane_q + (off + i * LANES) <= row
            lowered = jnp.where(causal, a, -jnp.inf)
            sc_ref[kb * n_sub + i] = lowered
            mn = jnp.minimum(mn, jnp.where(causal, a, jnp.inf))
            mx = jnp.maximum(mx, lowered)
        return mn, mx

    mn, mx = lax.fori_loop(0, nkb, score_body, (jnp.full((tq, LANES), jnp.inf, F32),
                                                jnp.full((tq, LANES), -jnp.inf, F32)))

    slabs = range(0, tq, COUNT_ROWS)

    def scan(step, init, *row_args, settled=None):
        outs = []
        for k, r0 in enumerate(slabs):
            args = [a[r0:r0 + COUNT_ROWS] for a in row_args]
            start = jnp.full((COUNT_ROWS, LANES), init, F32)

            def body(kb, acc, r0=r0, args=args):
                off = pl.multiple_of(kb * tk, tk)
                for i in range(n_sub):
                    blk = sc_ref[kb * n_sub + i, pl.ds(r0, COUNT_ROWS), :]
                    acc = step(acc, blk, lane + (off + i * LANES), *args)
                return acc

            run = functools.partial(lax.fori_loop, 0, nkb, body, start)
            outs.append(run() if settled is None
                        else lax.cond(settled[k] > 0.5, lambda start=start: start, run))
        return jnp.concatenate(outs, axis=0)

    def count(pred, *row_args, settled=None):
        c = scan(lambda acc, blk, pos, *a: acc + jnp.where(pred(blk, pos, *a), 1.0, 0.0),
                 0.0, *row_args, settled=settled)
        return rep(jnp.sum(c, axis=1, keepdims=True))

    def below_max(bound, settled):
        m = scan(lambda acc, blk, pos, b: jnp.maximum(acc, jnp.where(blk < b, blk, -jnp.inf)),
                 -jnp.inf, bound, settled=settled)
        return rep(jnp.max(m, axis=1, keepdims=True))

    def slab_settled(state):
        return tuple(jnp.min(state[r0:r0 + COUNT_ROWS]) for r0 in slabs)

    kf = float(topk)
    few = row < topk
    rmin = rep(jnp.min(mn, axis=1, keepdims=True))
    rmax = rep(jnp.max(mx, axis=1, keepdims=True))
    above = jnp.where(rmax > 0.0, 2.0 * rmax, 0.5 * rmax) + 1.0
    lo0 = jnp.where(few, 0.0, rmin)
    hi0 = jnp.where(few, 0.0, above)
    state0 = jnp.where(few, 1.0, 0.0)

    def search_cond(st):
        it, settled = st[0], st[1]
        return (it < BISECT_STEPS) & (functools.reduce(jnp.minimum, settled) < 0.5)

    def search_body(st):
        it, settled, lo, hi, state = st
        mid = 0.5 * lo + 0.5 * hi
        cnt = count(lambda blk, pos, m: blk >= m, mid, settled=settled)
        active = state == 0.0
        ge = cnt >= kf
        lo = jnp.where(active, jnp.where(ge, mid, lo), lo)
        hi = jnp.where(active, jnp.where(ge, hi, mid), hi)
        state = jnp.where(active, jnp.where(cnt == kf, 1.0, 0.0), state)
        return it + 1, slab_settled(state), lo, hi, state

    _, settled, lo, hi, state = lax.while_loop(
        search_cond, search_body, (jnp.int32(0), slab_settled(state0), lo0, hi0, state0))
    thr = jnp.where(few, -jnp.inf, lo)

    def snap_body(st):
        settled, hi, thr, state = st
        cand = below_max(hi, settled)
        cnt = count(lambda blk, pos, c: blk >= c, cand, settled=settled)
        active = state == 0.0
        found = cnt >= kf
        thr = jnp.where(active, jnp.where(found, cand, thr), thr)
        hi = jnp.where(active, jnp.where(found, hi, cand), hi)
        state = jnp.where(active, jnp.where(found, 2.0, 0.0), state)
        return slab_settled(state), hi, thr, state

    _, _, thr, state = lax.while_loop(
        lambda st: functools.reduce(jnp.minimum, st[0]) < 0.5, snap_body, (settled, hi, thr, state))

    icut_ref[...] = jnp.full(icut_ref.shape, s_len, I32)
    tied = state == 2.0

    @pl.when(jnp.max(state) > 1.5)
    def _():
        need = kf - count(lambda blk, pos, t: blk > t, thr)
        n_bits = max(1, (s_len - 1).bit_length())

        def idx_body(i, cut):
            cand = cut | jnp.left_shift(jnp.int32(1), n_bits - 1 - i)
            cnt = count(lambda blk, pos, t, cd: (blk == t) & (pos < cd), thr, cand)
            return jnp.where(cnt < need, cand, cut)

        cut = lax.fori_loop(0, n_bits, idx_body, jnp.zeros((tq, LANES), I32))
        icut_ref[...] = jnp.where(tied, cut, s_len)

    icut = icut_ref[...]

    for r0 in range(0, tq, COUNT_ROWS):
        rows = pl.ds(r0, COUNT_ROWS)
        thr_r, icut_r, row_r = (a[r0:r0 + COUNT_ROWS] for a in (thr, icut, row))

        def bias_body(kb, _, rows=rows, thr_r=thr_r, icut_r=icut_r, row_r=row_r):
            off = pl.multiple_of(kb * tk, tk)
            for i in range(n_sub):
                blk = sc_ref[kb * n_sub + i, rows, :]
                pos = lane + (off + i * LANES)
                sel = (blk > thr_r) | ((blk == thr_r) & (pos <= icut_r))
                sc_ref[kb * n_sub + i, rows, :] = jnp.where(sel & (pos <= row_r), 0.0, NEG)
            return 0

        lax.fori_loop(0, nkb, bias_body, 0)

    qa = qa_ref[0]
    kpos = lax.broadcasted_iota(I32, (1, tk), 1)
    slopes = [LOG2E * 2.0 ** (-8.0 * (h + 1) / N_HEADS) for h in range(N_HEADS)]
    _flash_reset(m_ref, l_ref, acc_ref)

    def attn_body(kb, _):
        off = pl.multiple_of(kb * tk, tk)
        rel = (kpos + (off - t0)).astype(F32)

        def chunk_logits(h, r0, c, x):
            mask = sc_ref[kb * n_sub + c, pl.ds(r0, FLASH_ROWS), :]
            return x + (mask + slopes[h] * rel[:, c * LANES:(c + 1) * LANES])

        _flash_step(qa, kat_ref, va_ref, off, chunk_logits, m_ref, l_ref, acc_ref, p_ref)
        return 0

    lax.fori_loop(0, nkb, attn_body, 0)
    _flash_finish(l_ref, acc_ref, o_ref)


def _dsa_attn(qa, qi, small, kat, va, kit, *, tq=512, tk=512):
    b, s, _ = qa.shape
    row = lambda i, j: (i, j, 0)
    whole = lambda i, j: (i, 0, 0)
    once = pl.Buffered(1)
    kern = functools.partial(_dsa_kernel, tq=tq, tk=tk, topk=min(TOPK, s // 4))
    return pl.pallas_call(
        kern,
        grid=(b, s // tq),
        in_specs=[pl.BlockSpec((1, tq, WIDTH), row), pl.BlockSpec((1, tq, WIDTH), row),
                  pl.BlockSpec((1, tq, LANES), row),
                  pl.BlockSpec((1, WIDTH, s), whole, pipeline_mode=once),
                  pl.BlockSpec((1, s, WIDTH), whole, pipeline_mode=once),
                  pl.BlockSpec((1, 3 * HEAD_DIM, s), whole, pipeline_mode=once)],
        out_specs=pl.BlockSpec((1, tq, WIDTH), row),
        out_shape=jax.ShapeDtypeStruct((b, s, WIDTH), BF16),
        scratch_shapes=[pltpu.VMEM((s // LANES, tq, LANES), F32), pltpu.VMEM((tq, LANES), I32)]
        + _flash_scratch(tq, tk),
        compiler_params=_cparams(2),
        name="dsa_attn",
    )(qa, qi, small, kat, va, kit)


def _fox_kernel(bound_ref, qf_ref, small_ref, smallt_ref, kft_ref, vf_ref, o_ref,
                m_ref, l_ref, acc_ref, p_ref, *, tq, tk):
    t0 = pl.program_id(1) * tq
    n_full = t0 // tk
    nkb = (t0 + tq + tk - 1) // tk
    row = lax.broadcasted_iota(I32, (FLASH_ROWS, LANES), 0) + t0
    col = lax.broadcasted_iota(I32, (FLASH_ROWS, LANES), 1)
    qf = qf_ref[0]
    f_t0 = smallt_ref[0, SM_F:SM_F + N_HEADS, pl.ds(pl.multiple_of(t0, LANES), LANES)][:, 0:1]

    def attend(online):
        def attn_body(kb, _, causal):
            off = pl.multiple_of(kb * tk, tk)
            decay = LOG2E * (f_t0 - smallt_ref[0, SM_F:SM_F + N_HEADS, pl.ds(off, tk)])

            def chunk_logits(h, r0, c, x):
                x = x + decay[h:h + 1, c * LANES:(c + 1) * LANES]
                if causal:
                    x = jnp.where(col + (off + c * LANES) <= row + r0, x, NEG)
                return x

            _flash_step(qf, kft_ref, vf_ref, off, chunk_logits, m_ref, l_ref, acc_ref, p_ref, online)
            return 0

        lax.fori_loop(0, n_full, functools.partial(attn_body, causal=False), 0)
        lax.fori_loop(n_full, nkb, functools.partial(attn_body, causal=True), 0)
        _flash_finish(l_ref, acc_ref, o_ref)

    bound = bound_ref[0, 0]

    @pl.when(bound <= FOX_SAFE_BOUND)
    def _():
        _flash_reset(m_ref, l_ref, acc_ref)
        f_t = small_ref[0][:, SM_F:SM_F + N_HEADS]
        for h in range(N_HEADS):
            top = bound + LOG2E * (f_t0[h:h + 1, :] - f_t[:, h:h + 1])
            m_ref[h] = jnp.broadcast_to(top, (tq, LANES))
        attend(online=False)

    @pl.when(bound > FOX_SAFE_BOUND)
    def _():
        _flash_reset(m_ref, l_ref, acc_ref)
        attend(online=True)


def _fox_attn(bound, qf, small, smallt, kft, vf, *, tq=512, tk=512):
    b, s, _ = qf.shape
    row = lambda i, j: (i, j, 0)
    whole = lambda i, j: (i, 0, 0)
    once = pl.Buffered(1)
    kern = functools.partial(_fox_kernel, tq=tq, tk=tk)
    return pl.pallas_call(
        kern,
        grid=(b, s // tq),
        in_specs=[pl.BlockSpec(memory_space=pltpu.SMEM),
                  pl.BlockSpec((1, tq, WIDTH), row), pl.BlockSpec((1, tq, LANES), row),
                  pl.BlockSpec((1, LANES, s), whole, pipeline_mode=once),
                  pl.BlockSpec((1, WIDTH, s), whole, pipeline_mode=once),
                  pl.BlockSpec((1, s, WIDTH), whole, pipeline_mode=once)],
        out_specs=pl.BlockSpec((1, tq, WIDTH), row),
        out_shape=jax.ShapeDtypeStruct((b, s, WIDTH), BF16),
        scratch_shapes=_flash_scratch(tq, tk),
        compiler_params=_cparams(2),
        name="fox_attn",
    )(bound, qf, small, smallt, kft, vf)


def _mix_kernel(x_ref, mod_ref, g_ref, ad_ref, af_ref, wgate_ref, wbd_ref, wbf_ref, wo_ref, o_ref):
    d = x_ref.shape[2]
    x = x_ref[0]
    shift = mod_ref[0, 3:4, :]
    scale = mod_ref[0, 4:5, :]
    gate = mod_ref[0, 5:6, :]
    h = _rms_adaln(x, g_ref[...], scale, shift).astype(BF16)
    ga = _dot(h, wgate_ref[:, 0:d])
    gb = _dot(h, wgate_ref[:, d:2 * d])
    y_dsa = _dot(ad_ref[0], wbd_ref[...])
    y_fox = _dot(af_ref[0], wbf_ref[...])
    merged = jax.nn.sigmoid(ga) * y_dsa + jax.nn.sigmoid(gb) * y_fox
    o_ref[0] = x + gate * _dot(merged.astype(BF16), wo_ref[...])


def _mix_out(x, mod, gain, a_dsa, a_fox, w_gate, w_br_dsa, w_br_fox, w_out, *, tm=512):
    b, s, d = x.shape
    row = lambda i, j: (i, j, 0)
    return pl.pallas_call(
        _mix_kernel,
        grid=(b, s // tm),
        in_specs=[pl.BlockSpec((1, tm, d), row),
                  pl.BlockSpec((1, N_MOD, d), lambda i, j: (i, 0, 0)),
                  _const_spec((1, d)),
                  pl.BlockSpec((1, tm, WIDTH), row), pl.BlockSpec((1, tm, WIDTH), row),
                  _const_spec(w_gate.shape), _const_spec(w_br_dsa.shape),
                  _const_spec(w_br_fox.shape), _const_spec(w_out.shape)],
        out_specs=pl.BlockSpec((1, tm, d), row),
        out_shape=jax.ShapeDtypeStruct((b, s, d), F32),
        compiler_params=_cparams(2),
        name="mix_out",
    )(x, mod, gain.reshape(1, d), a_dsa, a_fox, w_gate, w_br_dsa, w_br_fox, w_out)


def _layer(x, mod, norm1_g, ffn1_wg, ffn1_wu, ffn1_wd, norm2_g, w_in, b_forget,
           qn_dsa, kn_dsa, qn_fox, kn_fox, w_br_dsa, w_br_fox, w_out, norm3_g,
           ffn2_wg, ffn2_wu, ffn2_wd):
    d = x.shape[2]
    bf = lambda w: w.astype(BF16)
    x = _ffn(x, mod, norm1_g, bf(ffn1_wg), bf(ffn1_wu), bf(ffn1_wd), mod_base=0)

    n_main = 7 * WIDTH
    n_small = HEAD_DIM + 2 * N_HEADS
    w_main = bf(w_in[:, :n_main])
    w_small = jnp.zeros((d, LANES), BF16).at[:, :n_small].set(bf(w_in[:, n_main:n_main + n_small]))
    w_gate = bf(w_in[:, n_main + n_small:])
    b_small = jnp.zeros((1, LANES), F32).at[0, SM_F:SM_F + N_HEADS].set(b_forget)
    q_scale = LOG2E * HEAD_DIM ** -0.5
    q_gains = jnp.stack([jnp.tile(qn_dsa, N_HEADS), jnp.tile(qn_fox, N_HEADS)]) * q_scale
    k_gains = jnp.stack([kn_dsa, kn_fox], axis=1)

    qa, kat, va, qf, kft, vf, qi, small, smallt, kit = _in_proj(
        x, mod, norm2_g, w_main, w_small, b_small, q_gains, k_gains)
    a_dsa = _dsa_attn(qa, qi, small, kat, va, kit)
    fox_bound = (1.05 * HEAD_DIM * q_scale * jnp.max(jnp.abs(qn_fox)) * jnp.max(jnp.abs(kn_fox)))
    a_fox = _fox_attn(fox_bound.reshape(1, 1).astype(F32), qf, small, smallt, kft, vf)
    x = _mix_out(x, mod, norm2_g, a_dsa, a_fox, w_gate, bf(w_br_dsa), bf(w_br_fox), bf(w_out))
    return _ffn(x, mod, norm3_g, bf(ffn2_wg), bf(ffn2_wu), bf(ffn2_wd), mod_base=6)


def kernel(x, c, ada_w, ada_b, norm1_g, ffn1_wg, ffn1_wu, ffn1_wd, norm2_g, w_in, b_forget,
           qn_dsa, kn_dsa, qn_fox, kn_fox, w_br_dsa, w_br_fox, w_out, norm3_g,
           ffn2_wg, ffn2_wu, ffn2_wd):
    per_layer = (norm1_g, ffn1_wg, ffn1_wu, ffn1_wd, norm2_g, w_in, b_forget,
                 qn_dsa, kn_dsa, qn_fox, kn_fox, w_br_dsa, w_br_fox, w_out, norm3_g,
                 ffn2_wg, ffn2_wu, ffn2_wd)
    for l in range(ada_w.shape[0]):
        mod = _adaln_mod(c, ada_w[l], ada_b[l])
        x = _layer(x, mod, *(p[l] for p in per_layer))
    return x
```

```python
import functools

import jax
import jax.numpy as jnp
from jax import lax
from jax.experimental import pallas as pl
from jax.experimental.pallas import tpu as pltpu

F32 = jnp.float32
BF16 = jnp.bfloat16
I32 = jnp.int32

HEAD_DIM = 64
N_HEADS = 8
WIDTH = N_HEADS * HEAD_DIM
TOPK = 256
EPS = 1e-6
N_MOD = 9
LANES = 128
NEG = -1e30
BISECT_STEPS = 24
VMEM_LIMIT = 56 * 1024 * 1024

SM_WI = 64
SM_F = 72


def _cparams(n_axes):
    return pltpu.CompilerParams(
        dimension_semantics=("arbitrary",) * n_axes, vmem_limit_bytes=VMEM_LIMIT)


def _const_spec(shape):
    nd = len(shape)
    return pl.BlockSpec(shape, lambda *_: (0,) * nd)


def _dot(a, b):
    return jnp.dot(a, b, preferred_element_type=F32)


def _rms_adaln(x, gain, scale, shift):
    y = x * lax.rsqrt(jnp.mean(x * x, axis=-1, keepdims=True) + EPS)
    return (y * gain) * (1.0 + scale) + shift


def _mod_kernel(c_ref, w_ref, b_ref, o_ref):
    c = c_ref[...]
    a = c * jax.nn.sigmoid(c)
    o_ref[...] = jnp.dot(a, w_ref[...], preferred_element_type=F32,
                         precision=lax.Precision.HIGHEST) + b_ref[...]


def _adaln_mod(c, ada_w, ada_b):
    b, d = c.shape
    n = ada_w.shape[1]
    rows = 8
    tn = n // 8
    c_pad = jnp.zeros((rows, d), F32).at[:b].set(c)
    out = pl.pallas_call(
        _mod_kernel,
        grid=(n // tn,),
        in_specs=[pl.BlockSpec((rows, d), lambda j: (0, 0)),
                  pl.BlockSpec((d, tn), lambda j: (0, j)),
                  pl.BlockSpec((1, tn), lambda j: (0, j))],
        out_specs=pl.BlockSpec((rows, tn), lambda j: (0, j)),
        out_shape=jax.ShapeDtypeStruct((rows, n), F32),
        compiler_params=_cparams(1),
        name="adaln_mod",
    )(c_pad, ada_w, ada_b.reshape(1, n))
    return out[:b].reshape(b, N_MOD, d)


def _ffn_kernel(x_ref, mod_ref, g_ref, wg_ref, wu_ref, wd_ref, o_ref, *, mod_base, n_chunks):
    x = x_ref[0]
    shift = mod_ref[0, mod_base:mod_base + 1, :]
    scale = mod_ref[0, mod_base + 1:mod_base + 2, :]
    gate = mod_ref[0, mod_base + 2:mod_base + 3, :]
    h = _rms_adaln(x, g_ref[...], scale, shift).astype(BF16)
    dff = wg_ref.shape[1]
    ck = dff // n_chunks
    acc = jnp.zeros(x.shape, F32)
    for i in range(n_chunks):
        g = _dot(h, wg_ref[:, i * ck:(i + 1) * ck])
        u = _dot(h, wu_ref[:, i * ck:(i + 1) * ck])
        a = (g * jax.nn.sigmoid(g) * u).astype(BF16)
        acc = acc + _dot(a, wd_ref[i * ck:(i + 1) * ck, :])
    o_ref[0] = x + (0.5 * gate) * acc


def _ffn(x, mod, gain, wg, wu, wd, *, mod_base, tm=512):
    b, s, d = x.shape
    dff = wg.shape[1]
    kern = functools.partial(_ffn_kernel, mod_base=mod_base, n_chunks=2)
    return pl.pallas_call(
        kern,
        grid=(b, s // tm),
        in_specs=[pl.BlockSpec((1, tm, d), lambda i, j: (i, j, 0)),
                  pl.BlockSpec((1, N_MOD, d), lambda i, j: (i, 0, 0)),
                  _const_spec((1, d)),
                  _const_spec((d, dff)), _const_spec((d, dff)), _const_spec((dff, d))],
        out_specs=pl.BlockSpec((1, tm, d), lambda i, j: (i, j, 0)),
        out_shape=jax.ShapeDtypeStruct((b, s, d), F32),
        compiler_params=_cparams(2),
        name="ffn",
    )(x, mod, gain.reshape(1, d), wg, wu, wd)


def _split3_bf16(v):
    p1 = v.astype(BF16)
    r1 = v - p1.astype(F32)
    p2 = r1.astype(BF16)
    r2 = r1 - p2.astype(F32)
    return p1, p2, r2.astype(BF16)


def _proj_kernel(x_ref, mod_ref, g_ref, wm_ref, ws_ref, bsm_ref, qg_ref, kg_ref,
                 qa_ref, kat_ref, va_ref, qf_ref, kft_ref, vf_ref, qi_ref,
                 small_ref, smallt_ref, kit_ref, carry_ref):
    tm = x_ref.shape[1]
    x = x_ref[0]
    shift = mod_ref[0, 3:4, :]
    scale = mod_ref[0, 4:5, :]
    h = _rms_adaln(x, g_ref[...], scale, shift).astype(BF16)

    def z(i):
        return _dot(h, wm_ref[:, i * WIDTH:(i + 1) * WIDTH])

    r = lax.broadcasted_iota(I32, (WIDTH, WIDTH), 0) // HEAD_DIM
    c = lax.broadcasted_iota(I32, (WIDTH, WIDTH), 1) // HEAD_DIM
    avg = jnp.where(r == c, 1.0 / HEAD_DIM, 0.0).astype(BF16)

    def norm_q(q, gain_row):
        ms = _dot((q * q).astype(BF16), avg)
        return (q * lax.rsqrt(ms + EPS) * gain_row).astype(BF16)

    def norm_kt(k, gain_col):
        kt = k.T.reshape(N_HEADS, HEAD_DIM, tm)
        ms = jnp.mean(kt * kt, axis=1, keepdims=True)
        kt = kt * lax.rsqrt(ms + EPS) * gain_col[None]
        return kt.reshape(WIDTH, tm).astype(BF16)

    qa_ref[0] = norm_q(z(0), qg_ref[0:1, :])
    kat_ref[0] = norm_kt(z(1), kg_ref[:, 0:1])
    va_ref[0] = z(2).astype(BF16)
    qf_ref[0] = norm_q(z(3), qg_ref[1:2, :])
    kft_ref[0] = norm_kt(z(4), kg_ref[:, 1:2])
    vf_ref[0] = z(5).astype(BF16)
    qi_ref[0] = z(6) * (HEAD_DIM ** -0.5)

    zs = _dot(h, ws_ref[...])
    pre = zs + bsm_ref[...]
    logf = jnp.minimum(pre, 0.0) - jnp.log(1.0 + jnp.exp(-jnp.abs(pre)))

    @pl.when(pl.program_id(1) == 0)
    def _():
        carry_ref[...] = jnp.zeros_like(carry_ref)

    ri = lax.broadcasted_iota(I32, (tm, tm), 0)
    ci = lax.broadcasted_iota(I32, (tm, tm), 1)
    tri = jnp.where(ci <= ri, 1.0, 0.0).astype(BF16)
    p1, p2, p3 = _split3_bf16(logf)
    cum = (_dot(tri, p1) + _dot(tri, p2)) + _dot(tri, p3) + carry_ref[...]
    carry_ref[...] = cum[tm - 1:tm, :]

    lane = lax.broadcasted_iota(I32, (tm, LANES), 1)
    small = jnp.where(lane < SM_WI, zs,
                      jnp.where(lane < SM_F, zs * (N_HEADS ** -0.5), cum))
    small_ref[0] = small
    st = small.T
    smallt_ref[0] = st
    ki = st[0:HEAD_DIM, :]
    ki_hi = ki.astype(BF16)
    ki_lo = (ki - ki_hi.astype(F32)).astype(BF16)
    kit_ref[0] = jnp.concatenate([ki_hi, ki_hi, ki_lo], axis=0)


def _in_proj(x, mod, gain, w_main, w_small, b_small, q_gains, k_gains, *, tm=512):
    b, s, d = x.shape
    row = lambda i, j: (i, j, 0)
    col = lambda i, j: (i, 0, j)
    sd = jax.ShapeDtypeStruct
    out_shape = [sd((b, s, WIDTH), BF16), sd((b, WIDTH, s), BF16), sd((b, s, WIDTH), BF16),
                 sd((b, s, WIDTH), BF16), sd((b, WIDTH, s), BF16), sd((b, s, WIDTH), BF16),
                 sd((b, s, WIDTH), F32),
                 sd((b, s, LANES), F32), sd((b, LANES, s), F32), sd((b, 3 * HEAD_DIM, s), BF16)]
    rspec = pl.BlockSpec((1, tm, WIDTH), row)
    cspec = pl.BlockSpec((1, WIDTH, tm), col)
    out_specs = [rspec, cspec, rspec, rspec, cspec, rspec, rspec,
                 pl.BlockSpec((1, tm, LANES), row), pl.BlockSpec((1, LANES, tm), col),
                 pl.BlockSpec((1, 3 * HEAD_DIM, tm), col)]
    return pl.pallas_call(
        _proj_kernel,
        grid=(b, s // tm),
        in_specs=[pl.BlockSpec((1, tm, d), row),
                  pl.BlockSpec((1, N_MOD, d), lambda i, j: (i, 0, 0)),
                  _const_spec((1, d)),
                  _const_spec(w_main.shape), _const_spec(w_small.shape),
                  _const_spec((1, LANES)), _const_spec((2, WIDTH)), _const_spec((HEAD_DIM, 2))],
        out_specs=out_specs,
        out_shape=out_shape,
        scratch_shapes=[pltpu.VMEM((1, LANES), F32)],
        compiler_params=_cparams(2),
        name="in_proj",
    )(x, mod, gain.reshape(1, d), w_main, w_small, b_small, q_gains, k_gains)


FLASH_ROWS = 64
COUNT_ROWS = 128
LOG2E = 1.4426950408889634
SAFE_BOUND = 50.0


def _flash_scratch(tq, tk):
    return [pltpu.VMEM((N_HEADS, tq, LANES), F32), pltpu.VMEM((N_HEADS, tq, LANES), F32),
            pltpu.VMEM((N_HEADS, tq, LANES), F32), pltpu.VMEM((N_HEADS, tq, tk), BF16)]


def _flash_reset(m_ref, l_ref, acc_ref):
    m_ref[...] = jnp.full(m_ref.shape, NEG, F32)
    l_ref[...] = jnp.zeros(l_ref.shape, F32)
    acc_ref[...] = jnp.zeros(acc_ref.shape, F32)


def _flash_step(q, kt_ref, v_ref, off, chunk_logits, m_ref, l_ref, acc_ref, p_ref, online=True):
    tq, tk = p_ref.shape[1:]
    n_sub = tk // LANES
    for h in range(N_HEADS):
        pair = h // 2
        q_h = q[:, h * HEAD_DIM:(h + 1) * HEAD_DIM]
        kt = kt_ref[0, h * HEAD_DIM:(h + 1) * HEAD_DIM, pl.ds(off, tk)]
        for r0 in range(0, tq, FLASH_ROWS):
            rows = pl.ds(r0, FLASH_ROWS)
            s = _dot(q_h[r0:r0 + FLASH_ROWS], kt)
            sc = [chunk_logits(h, r0, c, s[:, c * LANES:(c + 1) * LANES]) for c in range(n_sub)]
            m_row = m_ref[h, rows, :]
            if online:
                mx = sc[0]
                for x in sc[1:]:
                    mx = jnp.maximum(mx, x)
                m_old = m_row
                m_row = jnp.maximum(m_old, jnp.max(mx, axis=1, keepdims=True))
                alpha = jnp.exp2(m_old - m_row)
                m_ref[h, rows, :] = m_row
                acc_ref[h, rows, :] = alpha * acc_ref[h, rows, :]
            ps = [jnp.exp2(x - m_row) for x in sc]
            lsum = ps[0]
            for x in ps[1:]:
                lsum = lsum + x
            l_old = l_ref[h, rows, :]
            l_ref[h, rows, :] = (alpha * l_old if online else l_old) + lsum
            p_ref[h, rows, :] = jnp.concatenate(ps, axis=1).astype(BF16)
        v = v_ref[0, pl.ds(off, tk), pair * LANES:(pair + 1) * LANES]
        acc_ref[h] = acc_ref[h] + _dot(p_ref[h], v)


def _flash_finish(l_ref, acc_ref, o_ref):
    for pair in range(N_HEADS // 2):
        out = [acc_ref[h] / jnp.sum(l_ref[h], axis=1, keepdims=True) for h in (2 * pair, 2 * pair + 1)]
        lane = lax.broadcasted_iota(I32, out[0].shape, 1)
        o_ref[0, :, pair * LANES:(pair + 1) * LANES] = jnp.where(
            lane < HEAD_DIM, out[0], out[1]).astype(BF16)


def _dsa_kernel(bound_ref, qa_ref, qi_ref, small_ref, kat_ref, va_ref, kit_ref, o_ref, sc_ref, icut_ref,
                m_ref, l_ref, acc_ref, p_ref, *, tq, tk, topk):
    s_len = kat_ref.shape[2]
    t0 = pl.program_id(1) * tq
    nkb = (t0 + tq + tk - 1) // tk
    n_sub = tk // LANES
    row = lax.broadcasted_iota(I32, (tq, LANES), 0) + t0
    lane = lax.broadcasted_iota(I32, (COUNT_ROWS, LANES), 1)
    rep = lambda col: jnp.broadcast_to(col, (col.shape[0], LANES))

    wi = small_ref[0][:, SM_WI:SM_WI + N_HEADS]
    qi = qi_ref[0]
    lhs = []
    for h in range(N_HEADS):
        qh = qi[:, h * HEAD_DIM:(h + 1) * HEAD_DIM]
        hi = qh.astype(BF16)
        lo = (qh - hi.astype(F32)).astype(BF16)
        lhs.append(jnp.concatenate([hi, lo, hi], axis=1))
    lane_q = lax.broadcasted_iota(I32, (tq, LANES), 1)

    def score_body(kb, carry):
        mn, mx = carry
        off = pl.multiple_of(kb * tk, tk)
        kib = kit_ref[0, :, pl.ds(off, tk)]
        acc = jnp.zeros((tq, tk), F32)
        for h in range(N_HEADS):
            acc = acc + jnp.maximum(_dot(lhs[h], kib), 0.0) * wi[:, h:h + 1]
        for i in range(n_sub):
            a = acc[:, i * LANES:(i + 1) * LANES]
            causal = lane_q + (off + i * LANES) <= row
            lowered = jnp.where(causal, a, -jnp.inf)
            sc_ref[kb * n_sub + i] = lowered
            mn = jnp.minimum(mn, jnp.where(causal, a, jnp.inf))
            mx = jnp.maximum(mx, lowered)
        return mn, mx

    mn, mx = lax.fori_loop(0, nkb, score_body, (jnp.full((tq, LANES), jnp.inf, F32),
                                                jnp.full((tq, LANES), -jnp.inf, F32)))

    slabs = range(0, tq, COUNT_ROWS)

    def scan(step, init, *row_args, settled=None):
        outs = []
        for k, r0 in enumerate(slabs):
            args = [a[r0:r0 + COUNT_ROWS] for a in row_args]
            start = jnp.full((COUNT_ROWS, LANES), init, F32)

            def body(kb, acc, r0=r0, args=args):
                off = pl.multiple_of(kb * tk, tk)
                for i in range(n_sub):
                    blk = sc_ref[kb * n_sub + i, pl.ds(r0, COUNT_ROWS), :]
                    acc = step(acc, blk, lane + (off + i * LANES), *args)
                return acc

            run = functools.partial(lax.fori_loop, 0, nkb, body, start)
            outs.append(run() if settled is None
                        else lax.cond(settled[k] > 0.5, lambda start=start: start, run))
        return jnp.concatenate(outs, axis=0)

    def count(pred, *row_args, settled=None):
        c = scan(lambda acc, blk, pos, *a: acc + jnp.where(pred(blk, pos, *a), 1.0, 0.0),
                 0.0, *row_args, settled=settled)
        return rep(jnp.sum(c, axis=1, keepdims=True))

    def below_max(bound, settled):
        m = scan(lambda acc, blk, pos, b: jnp.maximum(acc, jnp.where(blk < b, blk, -jnp.inf)),
                 -jnp.inf, bound, settled=settled)
        return rep(jnp.max(m, axis=1, keepdims=True))

    def slab_settled(state):
        return tuple(jnp.min(state[r0:r0 + COUNT_ROWS]) for r0 in slabs)

    kf = float(topk)
    few = row < topk
    rmin = rep(jnp.min(mn, axis=1, keepdims=True))
    rmax = rep(jnp.max(mx, axis=1, keepdims=True))
    above = jnp.where(rmax > 0.0, 2.0 * rmax, 0.5 * rmax) + 1.0
    lo0 = jnp.where(few, 0.0, rmin)
    hi0 = jnp.where(few, 0.0, above)
    state0 = jnp.where(few, 1.0, 0.0)

    def search_cond(st):
        it, settled = st[0], st[1]
        return (it < BISECT_STEPS) & (functools.reduce(jnp.minimum, settled) < 0.5)

    def search_body(st):
        it, settled, lo, hi, state = st
        mid = 0.5 * lo + 0.5 * hi
        cnt = count(lambda blk, pos, m: blk >= m, mid, settled=settled)
        active = state == 0.0
        ge = cnt >= kf
        lo = jnp.where(active, jnp.where(ge, mid, lo), lo)
        hi = jnp.where(active, jnp.where(ge, hi, mid), hi)
        state = jnp.where(active, jnp.where(cnt == kf, 1.0, 0.0), state)
        return it + 1, slab_settled(state), lo, hi, state

    _, settled, lo, hi, state = lax.while_loop(
        search_cond, search_body, (jnp.int32(0), slab_settled(state0), lo0, hi0, state0))
    thr = jnp.where(few, -jnp.inf, lo)

    def snap_body(st):
        settled, hi, thr, state = st
        cand = below_max(hi, settled)
        cnt = count(lambda blk, pos, c: blk >= c, cand, settled=settled)
        active = state == 0.0
        found = cnt >= kf
        thr = jnp.where(active, jnp.where(found, cand, thr), thr)
        hi = jnp.where(active, jnp.where(found, hi, cand), hi)
        state = jnp.where(active, jnp.where(found, 2.0, 0.0), state)
        return slab_settled(state), hi, thr, state

    _, _, thr, state = lax.while_loop(
        lambda st: functools.reduce(jnp.minimum, st[0]) < 0.5, snap_body, (settled, hi, thr, state))

    icut_ref[...] = jnp.full(icut_ref.shape, s_len, I32)
    tied = state == 2.0

    @pl.when(jnp.max(state) > 1.5)
    def _():
        need = kf - count(lambda blk, pos, t: blk > t, thr)
        n_bits = max(1, (s_len - 1).bit_length())

        def idx_body(i, cut):
            cand = cut | jnp.left_shift(jnp.int32(1), n_bits - 1 - i)
            cnt = count(lambda blk, pos, t, cd: (blk == t) & (pos < cd), thr, cand)
            return jnp.where(cnt < need, cand, cut)

        cut = lax.fori_loop(0, n_bits, idx_body, jnp.zeros((tq, LANES), I32))
        icut_ref[...] = jnp.where(tied, cut, s_len)

    icut = icut_ref[...]

    last = []
    for r0 in range(0, tq, COUNT_ROWS):
        rows = pl.ds(r0, COUNT_ROWS)
        thr_r, icut_r, row_r = (a[r0:r0 + COUNT_ROWS] for a in (thr, icut, row))

        def bias_body(kb, far, rows=rows, thr_r=thr_r, icut_r=icut_r, row_r=row_r):
            off = pl.multiple_of(kb * tk, tk)
            for i in range(n_sub):
                blk = sc_ref[kb * n_sub + i, rows, :]
                pos = lane + (off + i * LANES)
                keep = ((blk > thr_r) | ((blk == thr_r) & (pos <= icut_r))) & (pos <= row_r)
                sc_ref[kb * n_sub + i, rows, :] = jnp.where(keep, 0.0, NEG)
                far = jnp.maximum(far, jnp.where(keep, pos, -1))
            return far

        far = lax.fori_loop(0, nkb, bias_body, jnp.full((COUNT_ROWS, LANES), -1, I32))
        last.append(rep(jnp.max(far.astype(F32), axis=1, keepdims=True)))
    last = jnp.concatenate(last, axis=0)

    qa = qa_ref[0]
    kpos = lax.broadcasted_iota(I32, (1, tk), 1)
    slopes = [LOG2E * 2.0 ** (-8.0 * (h + 1) / N_HEADS) for h in range(N_HEADS)]

    def attend(online):
        def attn_body(kb, _):
            off = pl.multiple_of(kb * tk, tk)
            rel = (kpos + (off - t0)).astype(F32)

            def chunk_logits(h, r0, c, x):
                mask = sc_ref[kb * n_sub + c, pl.ds(r0, FLASH_ROWS), :]
                return x + (mask + slopes[h] * rel[:, c * LANES:(c + 1) * LANES])

            _flash_step(qa, kat_ref, va_ref, off, chunk_logits, m_ref, l_ref, acc_ref, p_ref, online)
            return 0

        lax.fori_loop(0, nkb, attn_body, 0)
        _flash_finish(l_ref, acc_ref, o_ref)

    bound = bound_ref[0, 0]

    @pl.when(bound <= SAFE_BOUND)
    def _():
        _flash_reset(m_ref, l_ref, acc_ref)
        for h in range(N_HEADS):
            m_ref[h] = bound + slopes[h] * (last - t0.astype(F32))
        attend(online=False)

    @pl.when(bound > SAFE_BOUND)
    def _():
        _flash_reset(m_ref, l_ref, acc_ref)
        attend(online=True)


def _dsa_attn(bound, qa, qi, small, kat, va, kit, *, tq=512, tk=512):
    b, s, _ = qa.shape
    row = lambda i, j: (i, j, 0)
    whole = lambda i, j: (i, 0, 0)
    once = pl.Buffered(1)
    kern = functools.partial(_dsa_kernel, tq=tq, tk=tk, topk=min(TOPK, s // 4))
    return pl.pallas_call(
        kern,
        grid=(b, s // tq),
        in_specs=[pl.BlockSpec(memory_space=pltpu.SMEM),
                  pl.BlockSpec((1, tq, WIDTH), row), pl.BlockSpec((1, tq, WIDTH), row),
                  pl.BlockSpec((1, tq, LANES), row),
                  pl.BlockSpec((1, WIDTH, s), whole, pipeline_mode=once),
                  pl.BlockSpec((1, s, WIDTH), whole, pipeline_mode=once),
                  pl.BlockSpec((1, 3 * HEAD_DIM, s), whole, pipeline_mode=once)],
        out_specs=pl.BlockSpec((1, tq, WIDTH), row),
        out_shape=jax.ShapeDtypeStruct((b, s, WIDTH), BF16),
        scratch_shapes=[pltpu.VMEM((s // LANES, tq, LANES), F32), pltpu.VMEM((tq, LANES), I32)]
        + _flash_scratch(tq, tk),
        compiler_params=_cparams(2),
        name="dsa_attn",
    )(bound, qa, qi, small, kat, va, kit)


def _fox_kernel(bound_ref, qf_ref, small_ref, smallt_ref, kft_ref, vf_ref, o_ref,
                m_ref, l_ref, acc_ref, p_ref, *, tq, tk):
    t0 = pl.program_id(1) * tq
    n_full = t0 // tk
    nkb = (t0 + tq + tk - 1) // tk
    row = lax.broadcasted_iota(I32, (FLASH_ROWS, LANES), 0) + t0
    col = lax.broadcasted_iota(I32, (FLASH_ROWS, LANES), 1)
    qf = qf_ref[0]
    f_t0 = smallt_ref[0, SM_F:SM_F + N_HEADS, pl.ds(pl.multiple_of(t0, LANES), LANES)][:, 0:1]

    def attend(online):
        def attn_body(kb, _, causal):
            off = pl.multiple_of(kb * tk, tk)
            decay = LOG2E * (f_t0 - smallt_ref[0, SM_F:SM_F + N_HEADS, pl.ds(off, tk)])

            def chunk_logits(h, r0, c, x):
                x = x + decay[h:h + 1, c * LANES:(c + 1) * LANES]
                if causal:
                    x = jnp.where(col + (off + c * LANES) <= row + r0, x, NEG)
                return x

            _flash_step(qf, kft_ref, vf_ref, off, chunk_logits, m_ref, l_ref, acc_ref, p_ref, online)
            return 0

        lax.fori_loop(0, n_full, functools.partial(attn_body, causal=False), 0)
        lax.fori_loop(n_full, nkb, functools.partial(attn_body, causal=True), 0)
        _flash_finish(l_ref, acc_ref, o_ref)

    bound = bound_ref[0, 0]

    @pl.when(bound <= SAFE_BOUND)
    def _():
        _flash_reset(m_ref, l_ref, acc_ref)
        f_t = small_ref[0][:, SM_F:SM_F + N_HEADS]
        for h in range(N_HEADS):
            top = bound + LOG2E * (f_t0[h:h + 1, :] - f_t[:, h:h + 1])
            m_ref[h] = jnp.broadcast_to(top, (tq, LANES))
        attend(online=False)

    @pl.when(bound > SAFE_BOUND)
    def _():
        _flash_reset(m_ref, l_ref, acc_ref)
        attend(online=True)


def _fox_attn(bound, qf, small, smallt, kft, vf, *, tq=512, tk=512):
    b, s, _ = qf.shape
    row = lambda i, j: (i, j, 0)
    whole = lambda i, j: (i, 0, 0)
    once = pl.Buffered(1)
    kern = functools.partial(_fox_kernel, tq=tq, tk=tk)
    return pl.pallas_call(
        kern,
        grid=(b, s // tq),
        in_specs=[pl.BlockSpec(memory_space=pltpu.SMEM),
                  pl.BlockSpec((1, tq, WIDTH), row), pl.BlockSpec((1, tq, LANES), row),
                  pl.BlockSpec((1, LANES, s), whole, pipeline_mode=once),
                  pl.BlockSpec((1, WIDTH, s), whole, pipeline_mode=once),
                  pl.BlockSpec((1, s, WIDTH), whole, pipeline_mode=once)],
        out_specs=pl.BlockSpec((1, tq, WIDTH), row),
        out_shape=jax.ShapeDtypeStruct((b, s, WIDTH), BF16),
        scratch_shapes=_flash_scratch(tq, tk),
        compiler_params=_cparams(2),
        name="fox_attn",
    )(bound, qf, small, smallt, kft, vf)


def _mix_kernel(x_ref, mod_ref, g_ref, ad_ref, af_ref, wgate_ref, wbd_ref, wbf_ref, wo_ref, o_ref):
    d = x_ref.shape[2]
    x = x_ref[0]
    shift = mod_ref[0, 3:4, :]
    scale = mod_ref[0, 4:5, :]
    gate = mod_ref[0, 5:6, :]
    h = _rms_adaln(x, g_ref[...], scale, shift).astype(BF16)
    ga = _dot(h, wgate_ref[:, 0:d])
    gb = _dot(h, wgate_ref[:, d:2 * d])
    y_dsa = _dot(ad_ref[0], wbd_ref[...])
    y_fox = _dot(af_ref[0], wbf_ref[...])
    merged = jax.nn.sigmoid(ga) * y_dsa + jax.nn.sigmoid(gb) * y_fox
    o_ref[0] = x + gate * _dot(merged.astype(BF16), wo_ref[...])


def _mix_out(x, mod, gain, a_dsa, a_fox, w_gate, w_br_dsa, w_br_fox, w_out, *, tm=512):
    b, s, d = x.shape
    row = lambda i, j: (i, j, 0)
    return pl.pallas_call(
        _mix_kernel,
        grid=(b, s // tm),
        in_specs=[pl.BlockSpec((1, tm, d), row),
                  pl.BlockSpec((1, N_MOD, d), lambda i, j: (i, 0, 0)),
                  _const_spec((1, d)),
                  pl.BlockSpec((1, tm, WIDTH), row), pl.BlockSpec((1, tm, WIDTH), row),
                  _const_spec(w_gate.shape), _const_spec(w_br_dsa.shape),
                  _const_spec(w_br_fox.shape), _const_spec(w_out.shape)],
        out_specs=pl.BlockSpec((1, tm, d), row),
        out_shape=jax.ShapeDtypeStruct((b, s, d), F32),
        compiler_params=_cparams(2),
        name="mix_out",
    )(x, mod, gain.reshape(1, d), a_dsa, a_fox, w_gate, w_br_dsa, w_br_fox, w_out)


def _layer(x, mod, norm1_g, ffn1_wg, ffn1_wu, ffn1_wd, norm2_g, w_in, b_forget,
           qn_dsa, kn_dsa, qn_fox, kn_fox, w_br_dsa, w_br_fox, w_out, norm3_g,
           ffn2_wg, ffn2_wu, ffn2_wd):
    d = x.shape[2]
    bf = lambda w: w.astype(BF16)
    x = _ffn(x, mod, norm1_g, bf(ffn1_wg), bf(ffn1_wu), bf(ffn1_wd), mod_base=0)

    n_main = 7 * WIDTH
    n_small = HEAD_DIM + 2 * N_HEADS
    w_main = bf(w_in[:, :n_main])
    w_small = jnp.zeros((d, LANES), BF16).at[:, :n_small].set(bf(w_in[:, n_main:n_main + n_small]))
    w_gate = bf(w_in[:, n_main + n_small:])
    b_small = jnp.zeros((1, LANES), F32).at[0, SM_F:SM_F + N_HEADS].set(b_forget)
    q_scale = LOG2E * HEAD_DIM ** -0.5
    q_gains = jnp.stack([jnp.tile(qn_dsa, N_HEADS), jnp.tile(qn_fox, N_HEADS)]) * q_scale
    k_gains = jnp.stack([kn_dsa, kn_fox], axis=1)

    qa, kat, va, qf, kft, vf, qi, small, smallt, kit = _in_proj(
        x, mod, norm2_g, w_main, w_small, b_small, q_gains, k_gains)
    def logit_bound(qn, kn):
        b = 1.05 * HEAD_DIM * q_scale * jnp.max(jnp.abs(qn)) * jnp.max(jnp.abs(kn))
        return b.reshape(1, 1).astype(F32)

    a_dsa = _dsa_attn(logit_bound(qn_dsa, kn_dsa), qa, qi, small, kat, va, kit)
    a_fox = _fox_attn(logit_bound(qn_fox, kn_fox), qf, small, smallt, kft, vf)
    x = _mix_out(x, mod, norm2_g, a_dsa, a_fox, w_gate, bf(w_br_dsa), bf(w_br_fox), bf(w_out))
    return _ffn(x, mod, norm3_g, bf(ffn2_wg), bf(ffn2_wu), bf(ffn2_wd), mod_base=6)


def kernel(x, c, ada_w, ada_b, norm1_g, ffn1_wg, ffn1_wu, ffn1_wd, norm2_g, w_in, b_forget,
           qn_dsa, kn_dsa, qn_fox, kn_fox, w_br_dsa, w_br_fox, w_out, norm3_g,
           ffn2_wg, ffn2_wu, ffn2_wd):
    per_layer = (norm1_g, ffn1_wg, ffn1_wu, ffn1_wd, norm2_g, w_in, b_forget,
                 qn_dsa, kn_dsa, qn_fox, kn_fox, w_br_dsa, w_br_fox, w_out, norm3_g,
                 ffn2_wg, ffn2_wu, ffn2_wd)
    for l in range(ada_w.shape[0]):
        mod = _adaln_mod(c, ada_w[l], ada_b[l])
        x = _layer(x, mod, *(p[l] for p in per_layer))
    return x
```

```python
import functools

import jax
import jax.numpy as jnp
from jax import lax
from jax.experimental import pallas as pl
from jax.experimental.pallas import tpu as pltpu

F32 = jnp.float32
BF16 = jnp.bfloat16
I32 = jnp.int32

HEAD_DIM = 64
N_HEADS = 8
WIDTH = N_HEADS * HEAD_DIM
TOPK = 256
EPS = 1e-6
N_MOD = 9
LANES = 128
NEG = -1e30
BISECT_STEPS = 13
VMEM_LIMIT = 56 * 1024 * 1024

SM_WI = 64
SM_F = 72


def _cparams(n_axes):
    return pltpu.CompilerParams(
        dimension_semantics=("arbitrary",) * n_axes, vmem_limit_bytes=VMEM_LIMIT)


def _const_spec(shape):
    nd = len(shape)
    return pl.BlockSpec(shape, lambda *_: (0,) * nd)


def _dot(a, b):
    return jnp.dot(a, b, preferred_element_type=F32)


def _rms_adaln(x, gain, scale, shift):
    y = x * lax.rsqrt(jnp.mean(x * x, axis=-1, keepdims=True) + EPS)
    return (y * gain) * (1.0 + scale) + shift


def _mod_kernel(c_ref, w_ref, b_ref, o_ref):
    c = c_ref[...]
    a = c * jax.nn.sigmoid(c)
    o_ref[...] = jnp.dot(a, w_ref[...], preferred_element_type=F32,
                         precision=lax.Precision.HIGHEST) + b_ref[...]


def _adaln_mod(c, ada_w, ada_b):
    b, d = c.shape
    n = ada_w.shape[1]
    rows = 8
    tn = n // 8
    c_pad = jnp.zeros((rows, d), F32).at[:b].set(c)
    out = pl.pallas_call(
        _mod_kernel,
        grid=(n // tn,),
        in_specs=[pl.BlockSpec((rows, d), lambda j: (0, 0)),
                  pl.BlockSpec((d, tn), lambda j: (0, j)),
                  pl.BlockSpec((1, tn), lambda j: (0, j))],
        out_specs=pl.BlockSpec((rows, tn), lambda j: (0, j)),
        out_shape=jax.ShapeDtypeStruct((rows, n), F32),
        compiler_params=_cparams(1),
        name="adaln_mod",
    )(c_pad, ada_w, ada_b.reshape(1, n))
    return out[:b].reshape(b, N_MOD, d)


def _ffn_kernel(x_ref, mod_ref, g_ref, wg_ref, wu_ref, wd_ref, o_ref, *, mod_base, n_chunks):
    x = x_ref[0]
    shift = mod_ref[0, mod_base:mod_base + 1, :]
    scale = mod_ref[0, mod_base + 1:mod_base + 2, :]
    gate = mod_ref[0, mod_base + 2:mod_base + 3, :]
    h = _rms_adaln(x, g_ref[...], scale, shift).astype(BF16)
    dff = wg_ref.shape[1]
    ck = dff // n_chunks
    acc = jnp.zeros(x.shape, F32)
    for i in range(n_chunks):
        g = _dot(h, wg_ref[:, i * ck:(i + 1) * ck])
        u = _dot(h, wu_ref[:, i * ck:(i + 1) * ck])
        a = (g * jax.nn.sigmoid(g) * u).astype(BF16)
        acc = acc + _dot(a, wd_ref[i * ck:(i + 1) * ck, :])
    o_ref[0] = x + (0.5 * gate) * acc


def _ffn(x, mod, gain, wg, wu, wd, *, mod_base, tm=512):
    b, s, d = x.shape
    dff = wg.shape[1]
    kern = functools.partial(_ffn_kernel, mod_base=mod_base, n_chunks=2)
    return pl.pallas_call(
        kern,
        grid=(b, s // tm),
        in_specs=[pl.BlockSpec((1, tm, d), lambda i, j: (i, j, 0)),
                  pl.BlockSpec((1, N_MOD, d), lambda i, j: (i, 0, 0)),
                  _const_spec((1, d)),
                  _const_spec((d, dff)), _const_spec((d, dff)), _const_spec((dff, d))],
        out_specs=pl.BlockSpec((1, tm, d), lambda i, j: (i, j, 0)),
        out_shape=jax.ShapeDtypeStruct((b, s, d), F32),
        compiler_params=_cparams(2),
        name="ffn",
    )(x, mod, gain.reshape(1, d), wg, wu, wd)


def _split3_bf16(v):
    p1 = v.astype(BF16)
    r1 = v - p1.astype(F32)
    p2 = r1.astype(BF16)
    r2 = r1 - p2.astype(F32)
    return p1, p2, r2.astype(BF16)


def _proj_kernel(x_ref, mod_ref, g_ref, wm_ref, ws_ref, bsm_ref, qg_ref, kg_ref,
                 qa_ref, kat_ref, va_ref, qf_ref, kft_ref, vf_ref, qi_ref,
                 small_ref, smallt_ref, kit_ref, carry_ref):
    tm = x_ref.shape[1]
    x = x_ref[0]
    shift = mod_ref[0, 3:4, :]
    scale = mod_ref[0, 4:5, :]
    h = _rms_adaln(x, g_ref[...], scale, shift).astype(BF16)

    def z(i):
        return _dot(h, wm_ref[:, i * WIDTH:(i + 1) * WIDTH])

    r = lax.broadcasted_iota(I32, (WIDTH, WIDTH), 0) // HEAD_DIM
    c = lax.broadcasted_iota(I32, (WIDTH, WIDTH), 1) // HEAD_DIM
    avg = jnp.where(r == c, 1.0 / HEAD_DIM, 0.0).astype(BF16)

    def norm_q(q, gain_row):
        ms = _dot((q * q).astype(BF16), avg)
        return (q * lax.rsqrt(ms + EPS) * gain_row).astype(BF16)

    def norm_kt(k, gain_col):
        kt = k.T.reshape(N_HEADS, HEAD_DIM, tm)
        ms = jnp.mean(kt * kt, axis=1, keepdims=True)
        kt = kt * lax.rsqrt(ms + EPS) * gain_col[None]
        return kt.reshape(WIDTH, tm).astype(BF16)

    qa_ref[0] = norm_q(z(0), qg_ref[0:1, :])
    kat_ref[0] = norm_kt(z(1), kg_ref[:, 0:1])
    va_ref[0] = z(2).astype(BF16)
    qf_ref[0] = norm_q(z(3), qg_ref[1:2, :])
    kft_ref[0] = norm_kt(z(4), kg_ref[:, 1:2])
    vf_ref[0] = z(5).astype(BF16)
    qi_ref[0] = z(6) * (HEAD_DIM ** -0.5)

    zs = _dot(h, ws_ref[...])
    pre = zs + bsm_ref[...]
    logf = jnp.minimum(pre, 0.0) - jnp.log(1.0 + jnp.exp(-jnp.abs(pre)))

    @pl.when(pl.program_id(1) == 0)
    def _():
        carry_ref[...] = jnp.zeros_like(carry_ref)

    ri = lax.broadcasted_iota(I32, (tm, tm), 0)
    ci = lax.broadcasted_iota(I32, (tm, tm), 1)
    tri = jnp.where(ci <= ri, 1.0, 0.0).astype(BF16)
    p1, p2, p3 = _split3_bf16(logf)
    cum = (_dot(tri, p1) + _dot(tri, p2)) + _dot(tri, p3) + carry_ref[...]
    carry_ref[...] = cum[tm - 1:tm, :]

    lane = lax.broadcasted_iota(I32, (tm, LANES), 1)
    small = jnp.where(lane < SM_WI, zs,
                      jnp.where(lane < SM_F, zs * (N_HEADS ** -0.5), cum))
    small_ref[0] = small
    st = small.T
    smallt_ref[0] = st
    ki = st[0:HEAD_DIM, :]
    ki_hi = ki.astype(BF16)
    ki_lo = (ki - ki_hi.astype(F32)).astype(BF16)
    kit_ref[0] = jnp.concatenate([ki_hi, ki_hi, ki_lo], axis=0)


def _in_proj(x, mod, gain, w_main, w_small, b_small, q_gains, k_gains, *, tm=512):
    b, s, d = x.shape
    row = lambda i, j: (i, j, 0)
    col = lambda i, j: (i, 0, j)
    sd = jax.ShapeDtypeStruct
    out_shape = [sd((b, s, WIDTH), BF16), sd((b, WIDTH, s), BF16), sd((b, s, WIDTH), BF16),
                 sd((b, s, WIDTH), BF16), sd((b, WIDTH, s), BF16), sd((b, s, WIDTH), BF16),
                 sd((b, s, WIDTH), F32),
                 sd((b, s, LANES), F32), sd((b, LANES, s), F32), sd((b, 3 * HEAD_DIM, s), BF16)]
    rspec = pl.BlockSpec((1, tm, WIDTH), row)
    cspec = pl.BlockSpec((1, WIDTH, tm), col)
    out_specs = [rspec, cspec, rspec, rspec, cspec, rspec, rspec,
                 pl.BlockSpec((1, tm, LANES), row), pl.BlockSpec((1, LANES, tm), col),
                 pl.BlockSpec((1, 3 * HEAD_DIM, tm), col)]
    return pl.pallas_call(
        _proj_kernel,
        grid=(b, s // tm),
        in_specs=[pl.BlockSpec((1, tm, d), row),
                  pl.BlockSpec((1, N_MOD, d), lambda i, j: (i, 0, 0)),
                  _const_spec((1, d)),
                  _const_spec(w_main.shape), _const_spec(w_small.shape),
                  _const_spec((1, LANES)), _const_spec((2, WIDTH)), _const_spec((HEAD_DIM, 2))],
        out_specs=out_specs,
        out_shape=out_shape,
        scratch_shapes=[pltpu.VMEM((1, LANES), F32)],
        compiler_params=_cparams(2),
        name="in_proj",
    )(x, mod, gain.reshape(1, d), w_main, w_small, b_small, q_gains, k_gains)


FLASH_ROWS = 64
COUNT_ROWS = 128
LOG2E = 1.4426950408889634
SAFE_BOUND = 50.0


def _flash_scratch(tq, tk):
    return [pltpu.VMEM((N_HEADS, tq, LANES), F32), pltpu.VMEM((N_HEADS, tq, LANES), F32),
            pltpu.VMEM((N_HEADS, tq, LANES), F32), pltpu.VMEM((N_HEADS, tq, tk), BF16)]


def _flash_reset(m_ref, l_ref, acc_ref):
    m_ref[...] = jnp.full(m_ref.shape, NEG, F32)
    l_ref[...] = jnp.zeros(l_ref.shape, F32)
    acc_ref[...] = jnp.zeros(acc_ref.shape, F32)


def _flash_step(q, kt_ref, v_ref, off, chunk_logits, m_ref, l_ref, acc_ref, p_ref, online=True):
    tq, tk = p_ref.shape[1:]
    n_sub = tk // LANES
    for h in range(N_HEADS):
        pair = h // 2
        q_h = q[:, h * HEAD_DIM:(h + 1) * HEAD_DIM]
        kt = kt_ref[0, h * HEAD_DIM:(h + 1) * HEAD_DIM, pl.ds(off, tk)]
        for r0 in range(0, tq, FLASH_ROWS):
            rows = pl.ds(r0, FLASH_ROWS)
            s = _dot(q_h[r0:r0 + FLASH_ROWS], kt)
            sc = [chunk_logits(h, r0, c, s[:, c * LANES:(c + 1) * LANES]) for c in range(n_sub)]
            m_row = m_ref[h, rows, :]
            if online:
                mx = sc[0]
                for x in sc[1:]:
                    mx = jnp.maximum(mx, x)
                m_old = m_row
                m_row = jnp.maximum(m_old, jnp.max(mx, axis=1, keepdims=True))
                alpha = jnp.exp2(m_old - m_row)
                m_ref[h, rows, :] = m_row
                acc_ref[h, rows, :] = alpha * acc_ref[h, rows, :]
            ps = [jnp.exp2(x - m_row) for x in sc]
            lsum = ps[0]
            for x in ps[1:]:
                lsum = lsum + x
            l_old = l_ref[h, rows, :]
            l_ref[h, rows, :] = (alpha * l_old if online else l_old) + lsum
            p_ref[h, rows, :] = jnp.concatenate(ps, axis=1).astype(BF16)
        v = v_ref[0, pl.ds(off, tk), pair * LANES:(pair + 1) * LANES]
        acc_ref[h] = acc_ref[h] + _dot(p_ref[h], v)


def _flash_finish(l_ref, acc_ref, o_ref):
    for pair in range(N_HEADS // 2):
        out = [acc_ref[h] / jnp.sum(l_ref[h], axis=1, keepdims=True) for h in (2 * pair, 2 * pair + 1)]
        lane = lax.broadcasted_iota(I32, out[0].shape, 1)
        o_ref[0, :, pair * LANES:(pair + 1) * LANES] = jnp.where(
            lane < HEAD_DIM, out[0], out[1]).astype(BF16)


def _dsa_kernel(bound_ref, qa_ref, qi_ref, small_ref, kat_ref, va_ref, kit_ref, o_ref, sc_ref, icut_ref,
                m_ref, l_ref, acc_ref, p_ref, *, tq, tk, topk):
    s_len = kat_ref.shape[2]
    t0 = pl.program_id(1) * tq
    nkb = (t0 + tq + tk - 1) // tk
    n_sub = tk // LANES
    row = lax.broadcasted_iota(I32, (tq, LANES), 0) + t0
    lane = lax.broadcasted_iota(I32, (COUNT_ROWS, LANES), 1)
    rep = lambda col: jnp.broadcast_to(col, (col.shape[0], LANES))

    wi = small_ref[0][:, SM_WI:SM_WI + N_HEADS]
    qi = qi_ref[0]
    lhs = []
    for h in range(N_HEADS):
        qh = qi[:, h * HEAD_DIM:(h + 1) * HEAD_DIM]
        hi = qh.astype(BF16)
        lo = (qh - hi.astype(F32)).astype(BF16)
        lhs.append(jnp.concatenate([hi, lo, hi], axis=1))
    lane_q = lax.broadcasted_iota(I32, (tq, LANES), 1)

    def score_body(kb, carry):
        mn, top1, top2 = carry
        off = pl.multiple_of(kb * tk, tk)
        kib = kit_ref[0, :, pl.ds(off, tk)]
        acc = jnp.zeros((tq, tk), F32)
        for h in range(N_HEADS):
            acc = acc + jnp.maximum(_dot(lhs[h], kib), 0.0) * wi[:, h:h + 1]
        for i in range(n_sub):
            a = acc[:, i * LANES:(i + 1) * LANES]
            causal = lane_q + (off + i * LANES) <= row
            lowered = jnp.where(causal, a, -jnp.inf)
            sc_ref[kb * n_sub + i] = lowered
            mn = jnp.minimum(mn, jnp.where(causal, a, jnp.inf))
            top2 = jnp.maximum(top2, jnp.minimum(top1, lowered))
            top1 = jnp.maximum(top1, lowered)
        return mn, top1, top2

    low = jnp.full((tq, LANES), -jnp.inf, F32)
    mn, _, top2 = lax.fori_loop(0, nkb, score_body, (jnp.full((tq, LANES), jnp.inf, F32), low, low))

    slabs = range(0, tq, COUNT_ROWS)

    def scan(step, init, *row_args, settled=None):
        outs = []
        for k, r0 in enumerate(slabs):
            args = [a[r0:r0 + COUNT_ROWS] for a in row_args]
            start = jnp.full((COUNT_ROWS, LANES), init, F32)

            def body(kb, acc, r0=r0, args=args):
                off = pl.multiple_of(kb * tk, tk)
                for i in range(n_sub):
                    blk = sc_ref[kb * n_sub + i, pl.ds(r0, COUNT_ROWS), :]
                    acc = step(acc, blk, lane + (off + i * LANES), *args)
                return acc

            run = functools.partial(lax.fori_loop, 0, nkb, body, start)
            outs.append(run() if settled is None
                        else lax.cond(settled[k] > 0.5, lambda start=start: start, run))
        return jnp.concatenate(outs, axis=0)

    def count(pred, *row_args, settled=None):
        c = scan(lambda acc, blk, pos, *a: acc + jnp.where(pred(blk, pos, *a), 1.0, 0.0),
                 0.0, *row_args, settled=settled)
        return rep(jnp.sum(c, axis=1, keepdims=True))

    def below_max(bound, settled):
        m = scan(lambda acc, blk, pos, b: jnp.maximum(acc, jnp.where(blk < b, blk, -jnp.inf)),
                 -jnp.inf, bound, settled=settled)
        return rep(jnp.max(m, axis=1, keepdims=True))

    def slab_settled(state):
        return tuple(jnp.min(state[r0:r0 + COUNT_ROWS]) for r0 in slabs)

    kf = float(topk)
    few = row < topk
    rmin = rep(jnp.min(mn, axis=1, keepdims=True))
    run_lo = rep(jnp.min(top2, axis=1, keepdims=True))
    run_hi = rep(jnp.max(top2, axis=1, keepdims=True))
    above = jnp.where(run_hi > 0.0, run_hi * (1.0 + 1e-6), run_hi * (1.0 - 1e-6)) + 1e-30
    lo0 = jnp.where(few, 0.0, jnp.maximum(run_lo, rmin))
    hi0 = jnp.where(few, 0.0, above)
    state0 = jnp.where(few, 1.0, 0.0)

    def search_cond(st):
        it, settled = st[0], st[1]
        return (it < BISECT_STEPS) & (functools.reduce(jnp.minimum, settled) < 0.5)

    def search_body(st):
        it, settled, lo, hi, state = st
        mid = 0.5 * lo + 0.5 * hi
        cnt = count(lambda blk, pos, m: blk >= m, mid, settled=settled)
        active = state == 0.0
        ge = cnt >= kf
        lo = jnp.where(active, jnp.where(ge, mid, lo), lo)
        hi = jnp.where(active, jnp.where(ge, hi, mid), hi)
        state = jnp.where(active, jnp.where(cnt == kf, 1.0, 0.0), state)
        return it + 1, slab_settled(state), lo, hi, state

    _, settled, lo, hi, state = lax.while_loop(
        search_cond, search_body, (jnp.int32(0), slab_settled(state0), lo0, hi0, state0))
    thr = jnp.where(few, -jnp.inf, lo)

    def snap_body(st):
        settled, hi, thr, state = st
        cand = below_max(hi, settled)
        cnt = count(lambda blk, pos, c: blk >= c, cand, settled=settled)
        active = state == 0.0
        found = cnt >= kf
        thr = jnp.where(active, jnp.where(found, cand, thr), thr)
        hi = jnp.where(active, jnp.where(found, hi, cand), hi)
        state = jnp.where(active, jnp.where(found, 2.0, 0.0), state)
        return slab_settled(state), hi, thr, state

    _, _, thr, state = lax.while_loop(
        lambda st: functools.reduce(jnp.minimum, st[0]) < 0.5, snap_body, (settled, hi, thr, state))

    icut_ref[...] = jnp.full(icut_ref.shape, s_len, I32)
    tied = state == 2.0

    @pl.when(jnp.max(state) > 1.5)
    def _():
        need = kf - count(lambda blk, pos, t: blk > t, thr)
        n_bits = max(1, (s_len - 1).bit_length())

        def idx_body(i, cut):
            cand = cut | jnp.left_shift(jnp.int32(1), n_bits - 1 - i)
            cnt = count(lambda blk, pos, t, cd: (blk == t) & (pos < cd), thr, cand)
            return jnp.where(cnt < need, cand, cut)

        cut = lax.fori_loop(0, n_bits, idx_body, jnp.zeros((tq, LANES), I32))
        icut_ref[...] = jnp.where(tied, cut, s_len)

    icut = icut_ref[...]

    last = []
    for r0 in range(0, tq, COUNT_ROWS):
        rows = pl.ds(r0, COUNT_ROWS)
        thr_r, icut_r, row_r = (a[r0:r0 + COUNT_ROWS] for a in (thr, icut, row))

        def bias_body(kb, far, rows=rows, thr_r=thr_r, icut_r=icut_r, row_r=row_r):
            off = pl.multiple_of(kb * tk, tk)
            for i in range(n_sub):
                blk = sc_ref[kb * n_sub + i, rows, :]
                pos = lane + (off + i * LANES)
                keep = ((blk > thr_r) | ((blk == thr_r) & (pos <= icut_r))) & (pos <= row_r)
                sc_ref[kb * n_sub + i, rows, :] = jnp.where(keep, 0.0, NEG)
                far = jnp.maximum(far, jnp.where(keep, pos, -1))
            return far

        far = lax.fori_loop(0, nkb, bias_body, jnp.full((COUNT_ROWS, LANES), -1, I32))
        last.append(rep(jnp.max(far.astype(F32), axis=1, keepdims=True)))
    last = jnp.concatenate(last, axis=0)

    qa = qa_ref[0]
    kpos = lax.broadcasted_iota(I32, (1, tk), 1)
    slopes = [LOG2E * 2.0 ** (-8.0 * (h + 1) / N_HEADS) for h in range(N_HEADS)]

    def attend(online):
        def attn_body(kb, _):
            off = pl.multiple_of(kb * tk, tk)
            rel = (kpos + (off - t0)).astype(F32)

            def chunk_logits(h, r0, c, x):
                mask = sc_ref[kb * n_sub + c, pl.ds(r0, FLASH_ROWS), :]
                return x + (mask + slopes[h] * rel[:, c * LANES:(c + 1) * LANES])

            _flash_step(qa, kat_ref, va_ref, off, chunk_logits, m_ref, l_ref, acc_ref, p_ref, online)
            return 0

        lax.fori_loop(0, nkb, attn_body, 0)
        _flash_finish(l_ref, acc_ref, o_ref)

    bound = bound_ref[0, 0]

    @pl.when(bound <= SAFE_BOUND)
    def _():
        _flash_reset(m_ref, l_ref, acc_ref)
        for h in range(N_HEADS):
            m_ref[h] = bound + slopes[h] * (last - t0.astype(F32))
        attend(online=False)

    @pl.when(bound > SAFE_BOUND)
    def _():
        _flash_reset(m_ref, l_ref, acc_ref)
        attend(online=True)


def _dsa_attn(bound, qa, qi, small, kat, va, kit, *, tq=512, tk=512):
    b, s, _ = qa.shape
    row = lambda i, j: (i, j, 0)
    whole = lambda i, j: (i, 0, 0)
    once = pl.Buffered(1)
    topk = min(TOPK, s // 4)
    assert topk > LANES and s % tq == 0 and s % tk == 0, (s, tq, tk)
    kern = functools.partial(_dsa_kernel, tq=tq, tk=tk, topk=topk)
    return pl.pallas_call(
        kern,
        grid=(b, s // tq),
        in_specs=[pl.BlockSpec(memory_space=pltpu.SMEM),
                  pl.BlockSpec((1, tq, WIDTH), row), pl.BlockSpec((1, tq, WIDTH), row),
                  pl.BlockSpec((1, tq, LANES), row),
                  pl.BlockSpec((1, WIDTH, s), whole, pipeline_mode=once),
                  pl.BlockSpec((1, s, WIDTH), whole, pipeline_mode=once),
                  pl.BlockSpec((1, 3 * HEAD_DIM, s), whole, pipeline_mode=once)],
        out_specs=pl.BlockSpec((1, tq, WIDTH), row),
        out_shape=jax.ShapeDtypeStruct((b, s, WIDTH), BF16),
        scratch_shapes=[pltpu.VMEM((s // LANES, tq, LANES), F32), pltpu.VMEM((tq, LANES), I32)]
        + _flash_scratch(tq, tk),
        compiler_params=_cparams(2),
        name="dsa_attn",
    )(bound, qa, qi, small, kat, va, kit)


def _fox_kernel(bound_ref, qf_ref, small_ref, smallt_ref, kft_ref, vf_ref, o_ref,
                m_ref, l_ref, acc_ref, p_ref, *, tq, tk):
    t0 = pl.program_id(1) * tq
    n_full = t0 // tk
    nkb = (t0 + tq + tk - 1) // tk
    row = lax.broadcasted_iota(I32, (FLASH_ROWS, LANES), 0) + t0
    col = lax.broadcasted_iota(I32, (FLASH_ROWS, LANES), 1)
    qf = qf_ref[0]
    f_t0 = smallt_ref[0, SM_F:SM_F + N_HEADS, pl.ds(pl.multiple_of(t0, LANES), LANES)][:, 0:1]

    def attend(online):
        def attn_body(kb, _, causal):
            off = pl.multiple_of(kb * tk, tk)
            decay = LOG2E * (f_t0 - smallt_ref[0, SM_F:SM_F + N_HEADS, pl.ds(off, tk)])

            def chunk_logits(h, r0, c, x):
                x = x + decay[h:h + 1, c * LANES:(c + 1) * LANES]
                if causal:
                    x = jnp.where(col + (off + c * LANES) <= row + r0, x, NEG)
                return x

            _flash_step(qf, kft_ref, vf_ref, off, chunk_logits, m_ref, l_ref, acc_ref, p_ref, online)
            return 0

        lax.fori_loop(0, n_full, functools.partial(attn_body, causal=False), 0)
        lax.fori_loop(n_full, nkb, functools.partial(attn_body, causal=True), 0)
        _flash_finish(l_ref, acc_ref, o_ref)

    bound = bound_ref[0, 0]

    @pl.when(bound <= SAFE_BOUND)
    def _():
        _flash_reset(m_ref, l_ref, acc_ref)
        f_t = small_ref[0][:, SM_F:SM_F + N_HEADS]
        for h in range(N_HEADS):
            top = bound + LOG2E * (f_t0[h:h + 1, :] - f_t[:, h:h + 1])
            m_ref[h] = jnp.broadcast_to(top, (tq, LANES))
        attend(online=False)

    @pl.when(bound > SAFE_BOUND)
    def _():
        _flash_reset(m_ref, l_ref, acc_ref)
        attend(online=True)


def _fox_attn(bound, qf, small, smallt, kft, vf, *, tq=512, tk=512):
    b, s, _ = qf.shape
    row = lambda i, j: (i, j, 0)
    whole = lambda i, j: (i, 0, 0)
    once = pl.Buffered(1)
    kern = functools.partial(_fox_kernel, tq=tq, tk=tk)
    return pl.pallas_call(
        kern,
        grid=(b, s // tq),
        in_specs=[pl.BlockSpec(memory_space=pltpu.SMEM),
                  pl.BlockSpec((1, tq, WIDTH), row), pl.BlockSpec((1, tq, LANES), row),
                  pl.BlockSpec((1, LANES, s), whole, pipeline_mode=once),
                  pl.BlockSpec((1, WIDTH, s), whole, pipeline_mode=once),
                  pl.BlockSpec((1, s, WIDTH), whole, pipeline_mode=once)],
        out_specs=pl.BlockSpec((1, tq, WIDTH), row),
        out_shape=jax.ShapeDtypeStruct((b, s, WIDTH), BF16),
        scratch_shapes=_flash_scratch(tq, tk),
        compiler_params=_cparams(2),
        name="fox_attn",
    )(bound, qf, small, smallt, kft, vf)


def _mix_kernel(x_ref, mod_ref, g_ref, ad_ref, af_ref, wgate_ref, wbd_ref, wbf_ref, wo_ref, o_ref):
    d = x_ref.shape[2]
    x = x_ref[0]
    shift = mod_ref[0, 3:4, :]
    scale = mod_ref[0, 4:5, :]
    gate = mod_ref[0, 5:6, :]
    h = _rms_adaln(x, g_ref[...], scale, shift).astype(BF16)
    ga = _dot(h, wgate_ref[:, 0:d])
    gb = _dot(h, wgate_ref[:, d:2 * d])
    y_dsa = _dot(ad_ref[0], wbd_ref[...])
    y_fox = _dot(af_ref[0], wbf_ref[...])
    merged = jax.nn.sigmoid(ga) * y_dsa + jax.nn.sigmoid(gb) * y_fox
    o_ref[0] = x + gate * _dot(merged.astype(BF16), wo_ref[...])


def _mix_out(x, mod, gain, a_dsa, a_fox, w_gate, w_br_dsa, w_br_fox, w_out, *, tm=512):
    b, s, d = x.shape
    row = lambda i, j: (i, j, 0)
    return pl.pallas_call(
        _mix_kernel,
        grid=(b, s // tm),
        in_specs=[pl.BlockSpec((1, tm, d), row),
                  pl.BlockSpec((1, N_MOD, d), lambda i, j: (i, 0, 0)),
                  _const_spec((1, d)),
                  pl.BlockSpec((1, tm, WIDTH), row), pl.BlockSpec((1, tm, WIDTH), row),
                  _const_spec(w_gate.shape), _const_spec(w_br_dsa.shape),
                  _const_spec(w_br_fox.shape), _const_spec(w_out.shape)],
        out_specs=pl.BlockSpec((1, tm, d), row),
        out_shape=jax.ShapeDtypeStruct((b, s, d), F32),
        compiler_params=_cparams(2),
        name="mix_out",
    )(x, mod, gain.reshape(1, d), a_dsa, a_fox, w_gate, w_br_dsa, w_br_fox, w_out)


def _layer(x, mod, norm1_g, ffn1_wg, ffn1_wu, ffn1_wd, norm2_g, w_in, b_forget,
           qn_dsa, kn_dsa, qn_fox, kn_fox, w_br_dsa, w_br_fox, w_out, norm3_g,
           ffn2_wg, ffn2_wu, ffn2_wd):
    d = x.shape[2]
    bf = lambda w: w.astype(BF16)
    x = _ffn(x, mod, norm1_g, bf(ffn1_wg), bf(ffn1_wu), bf(ffn1_wd), mod_base=0)

    n_main = 7 * WIDTH
    n_small = HEAD_DIM + 2 * N_HEADS
    w_main = bf(w_in[:, :n_main])
    w_small = jnp.zeros((d, LANES), BF16).at[:, :n_small].set(bf(w_in[:, n_main:n_main + n_small]))
    w_gate = bf(w_in[:, n_main + n_small:])
    b_small = jnp.zeros((1, LANES), F32).at[0, SM_F:SM_F + N_HEADS].set(b_forget)
    q_scale = LOG2E * HEAD_DIM ** -0.5
    q_gains = jnp.stack([jnp.tile(qn_dsa, N_HEADS), jnp.tile(qn_fox, N_HEADS)]) * q_scale
    k_gains = jnp.stack([kn_dsa, kn_fox], axis=1)

    qa, kat, va, qf, kft, vf, qi, small, smallt, kit = _in_proj(
        x, mod, norm2_g, w_main, w_small, b_small, q_gains, k_gains)
    def logit_bound(qn, kn):
        b = 1.05 * HEAD_DIM * q_scale * jnp.max(jnp.abs(qn)) * jnp.max(jnp.abs(kn))
        return b.reshape(1, 1).astype(F32)

    a_dsa = _dsa_attn(logit_bound(qn_dsa, kn_dsa), qa, qi, small, kat, va, kit)
    a_fox = _fox_attn(logit_bound(qn_fox, kn_fox), qf, small, smallt, kft, vf)
    x = _mix_out(x, mod, norm2_g, a_dsa, a_fox, w_gate, bf(w_br_dsa), bf(w_br_fox), bf(w_out))
    return _ffn(x, mod, norm3_g, bf(ffn2_wg), bf(ffn2_wu), bf(ffn2_wd), mod_base=6)


def kernel(x, c, ada_w, ada_b, norm1_g, ffn1_wg, ffn1_wu, ffn1_wd, norm2_g, w_in, b_forget,
           qn_dsa, kn_dsa, qn_fox, kn_fox, w_br_dsa, w_br_fox, w_out, norm3_g,
           ffn2_wg, ffn2_wu, ffn2_wd):
    per_layer = (norm1_g, ffn1_wg, ffn1_wu, ffn1_wd, norm2_g, w_in, b_forget,
                 qn_dsa, kn_dsa, qn_fox, kn_fox, w_br_dsa, w_br_fox, w_out, norm3_g,
                 ffn2_wg, ffn2_wu, ffn2_wd)
    for l in range(ada_w.shape[0]):
        mod = _adaln_mod(c, ada_w[l], ada_b[l])
        x = _layer(x, mod, *(p[l] for p in per_layer))
    return x
```

```python
import functools

import jax
import jax.numpy as jnp
from jax import lax
from jax.experimental import pallas as pl
from jax.experimental.pallas import tpu as pltpu

F32 = jnp.float32
BF16 = jnp.bfloat16
I32 = jnp.int32

HEAD_DIM = 64
N_HEADS = 8
WIDTH = N_HEADS * HEAD_DIM
TOPK = 256
EPS = 1e-6
N_MOD = 9
LANES = 128
NEG = -1e30
BISECT_STEPS = 13
VMEM_LIMIT = 56 * 1024 * 1024

SM_WI = 64
SM_F = 72


def _cparams(n_axes):
    return pltpu.CompilerParams(
        dimension_semantics=("arbitrary",) * n_axes, vmem_limit_bytes=VMEM_LIMIT)


def _const_spec(shape):
    nd = len(shape)
    return pl.BlockSpec(shape, lambda *_: (0,) * nd)


def _dot(a, b):
    return jnp.dot(a, b, preferred_element_type=F32)


def _rms_adaln(x, gain, scale, shift):
    y = x * lax.rsqrt(jnp.mean(x * x, axis=-1, keepdims=True) + EPS)
    return (y * gain) * (1.0 + scale) + shift


def _mod_kernel(c_ref, w_ref, b_ref, o_ref):
    c = c_ref[...]
    a = c * jax.nn.sigmoid(c)
    o_ref[...] = jnp.dot(a, w_ref[...], preferred_element_type=F32,
                         precision=lax.Precision.HIGHEST) + b_ref[...]


def _adaln_mod(c, ada_w, ada_b):
    b, d = c.shape
    n = ada_w.shape[1]
    rows = 8
    tn = n // 8
    c_pad = jnp.zeros((rows, d), F32).at[:b].set(c)
    out = pl.pallas_call(
        _mod_kernel,
        grid=(n // tn,),
        in_specs=[pl.BlockSpec((rows, d), lambda j: (0, 0)),
                  pl.BlockSpec((d, tn), lambda j: (0, j)),
                  pl.BlockSpec((1, tn), lambda j: (0, j))],
        out_specs=pl.BlockSpec((rows, tn), lambda j: (0, j)),
        out_shape=jax.ShapeDtypeStruct((rows, n), F32),
        compiler_params=_cparams(1),
        name="adaln_mod",
    )(c_pad, ada_w, ada_b.reshape(1, n))
    return out[:b].reshape(b, N_MOD, d)


def _ffn_kernel(x_ref, mod_ref, g_ref, wg_ref, wu_ref, wd_ref, o_ref, *, mod_base, n_chunks):
    x = x_ref[0]
    shift = mod_ref[0, mod_base:mod_base + 1, :]
    scale = mod_ref[0, mod_base + 1:mod_base + 2, :]
    gate = mod_ref[0, mod_base + 2:mod_base + 3, :]
    h = _rms_adaln(x, g_ref[...], scale, shift).astype(BF16)
    dff = wg_ref.shape[1]
    ck = dff // n_chunks
    acc = jnp.zeros(x.shape, F32)
    for i in range(n_chunks):
        g = _dot(h, wg_ref[:, i * ck:(i + 1) * ck])
        u = _dot(h, wu_ref[:, i * ck:(i + 1) * ck])
        a = (g * jax.nn.sigmoid(g) * u).astype(BF16)
        acc = acc + _dot(a, wd_ref[i * ck:(i + 1) * ck, :])
    o_ref[0] = x + (0.5 * gate) * acc


def _ffn(x, mod, gain, wg, wu, wd, *, mod_base, tm=512):
    b, s, d = x.shape
    dff = wg.shape[1]
    kern = functools.partial(_ffn_kernel, mod_base=mod_base, n_chunks=2)
    return pl.pallas_call(
        kern,
        grid=(b, s // tm),
        in_specs=[pl.BlockSpec((1, tm, d), lambda i, j: (i, j, 0)),
                  pl.BlockSpec((1, N_MOD, d), lambda i, j: (i, 0, 0)),
                  _const_spec((1, d)),
                  _const_spec((d, dff)), _const_spec((d, dff)), _const_spec((dff, d))],
        out_specs=pl.BlockSpec((1, tm, d), lambda i, j: (i, j, 0)),
        out_shape=jax.ShapeDtypeStruct((b, s, d), F32),
        compiler_params=_cparams(2),
        name="ffn",
    )(x, mod, gain.reshape(1, d), wg, wu, wd)


def _split3_bf16(v):
    p1 = v.astype(BF16)
    r1 = v - p1.astype(F32)
    p2 = r1.astype(BF16)
    r2 = r1 - p2.astype(F32)
    return p1, p2, r2.astype(BF16)


def _proj_kernel(x_ref, mod_ref, g_ref, wm_ref, ws_ref, bsm_ref, qg_ref, kg_ref,
                 qa_ref, kat_ref, va_ref, qf_ref, kft_ref, vf_ref, qi_ref,
                 small_ref, smallt_ref, kit_ref, carry_ref):
    tm = x_ref.shape[1]
    x = x_ref[0]
    shift = mod_ref[0, 3:4, :]
    scale = mod_ref[0, 4:5, :]
    h = _rms_adaln(x, g_ref[...], scale, shift).astype(BF16)

    def z(i):
        return _dot(h, wm_ref[:, i * WIDTH:(i + 1) * WIDTH])

    r = lax.broadcasted_iota(I32, (WIDTH, WIDTH), 0) // HEAD_DIM
    c = lax.broadcasted_iota(I32, (WIDTH, WIDTH), 1) // HEAD_DIM
    avg = jnp.where(r == c, 1.0 / HEAD_DIM, 0.0).astype(BF16)

    def norm_q(q, gain_row):
        ms = _dot((q * q).astype(BF16), avg)
        return (q * lax.rsqrt(ms + EPS) * gain_row).astype(BF16)

    def norm_kt(k, gain_col):
        kt = k.T.reshape(N_HEADS, HEAD_DIM, tm)
        ms = jnp.mean(kt * kt, axis=1, keepdims=True)
        kt = kt * lax.rsqrt(ms + EPS) * gain_col[None]
        return kt.reshape(WIDTH, tm).astype(BF16)

    qa_ref[0] = norm_q(z(0), qg_ref[0:1, :])
    kat_ref[0] = norm_kt(z(1), kg_ref[:, 0:1])
    va_ref[0] = z(2).astype(BF16)
    qf_ref[0] = norm_q(z(3), qg_ref[1:2, :])
    kft_ref[0] = norm_kt(z(4), kg_ref[:, 1:2])
    vf_ref[0] = z(5).astype(BF16)
    qi_ref[0] = z(6) * (HEAD_DIM ** -0.5)

    zs = _dot(h, ws_ref[...])
    pre = zs + bsm_ref[...]
    logf = jnp.minimum(pre, 0.0) - jnp.log(1.0 + jnp.exp(-jnp.abs(pre)))

    @pl.when(pl.program_id(1) == 0)
    def _():
        carry_ref[...] = jnp.zeros_like(carry_ref)

    ri = lax.broadcasted_iota(I32, (tm, tm), 0)
    ci = lax.broadcasted_iota(I32, (tm, tm), 1)
    tri = jnp.where(ci <= ri, 1.0, 0.0).astype(BF16)
    p1, p2, p3 = _split3_bf16(logf)
    cum = (_dot(tri, p1) + _dot(tri, p2)) + _dot(tri, p3) + carry_ref[...]
    carry_ref[...] = cum[tm - 1:tm, :]

    lane = lax.broadcasted_iota(I32, (tm, LANES), 1)
    small = jnp.where(lane < SM_WI, zs,
                      jnp.where(lane < SM_F, zs * (N_HEADS ** -0.5), cum))
    small_ref[0] = small
    st = small.T
    smallt_ref[0] = st
    ki = st[0:HEAD_DIM, :]
    ki_hi = ki.astype(BF16)
    ki_lo = (ki - ki_hi.astype(F32)).astype(BF16)
    kit_ref[0] = jnp.concatenate([ki_hi, ki_hi, ki_lo], axis=0)


def _in_proj(x, mod, gain, w_main, w_small, b_small, q_gains, k_gains, *, tm=512):
    b, s, d = x.shape
    row = lambda i, j: (i, j, 0)
    col = lambda i, j: (i, 0, j)
    sd = jax.ShapeDtypeStruct
    out_shape = [sd((b, s, WIDTH), BF16), sd((b, WIDTH, s), BF16), sd((b, s, WIDTH), BF16),
                 sd((b, s, WIDTH), BF16), sd((b, WIDTH, s), BF16), sd((b, s, WIDTH), BF16),
                 sd((b, s, WIDTH), F32),
                 sd((b, s, LANES), F32), sd((b, LANES, s), F32), sd((b, 3 * HEAD_DIM, s), BF16)]
    rspec = pl.BlockSpec((1, tm, WIDTH), row)
    cspec = pl.BlockSpec((1, WIDTH, tm), col)
    out_specs = [rspec, cspec, rspec, rspec, cspec, rspec, rspec,
                 pl.BlockSpec((1, tm, LANES), row), pl.BlockSpec((1, LANES, tm), col),
                 pl.BlockSpec((1, 3 * HEAD_DIM, tm), col)]
    return pl.pallas_call(
        _proj_kernel,
        grid=(b, s // tm),
        in_specs=[pl.BlockSpec((1, tm, d), row),
                  pl.BlockSpec((1, N_MOD, d), lambda i, j: (i, 0, 0)),
                  _const_spec((1, d)),
                  _const_spec(w_main.shape), _const_spec(w_small.shape),
                  _const_spec((1, LANES)), _const_spec((2, WIDTH)), _const_spec((HEAD_DIM, 2))],
        out_specs=out_specs,
        out_shape=out_shape,
        scratch_shapes=[pltpu.VMEM((1, LANES), F32)],
        compiler_params=_cparams(2),
        name="in_proj",
    )(x, mod, gain.reshape(1, d), w_main, w_small, b_small, q_gains, k_gains)


FLASH_ROWS = 64
COUNT_ROWS = 128
LOG2E = 1.4426950408889634
SAFE_BOUND = 50.0


def _flash_scratch(tq, tk):
    return [pltpu.VMEM((N_HEADS, tq, LANES), F32), pltpu.VMEM((N_HEADS, tq, LANES), F32),
            pltpu.VMEM((N_HEADS, tq, LANES), F32), pltpu.VMEM((N_HEADS, tq, tk), BF16)]


def _flash_reset(m_ref, l_ref, acc_ref):
    m_ref[...] = jnp.full(m_ref.shape, NEG, F32)
    l_ref[...] = jnp.zeros(l_ref.shape, F32)
    acc_ref[...] = jnp.zeros(acc_ref.shape, F32)


def _flash_step(q, kt_ref, v_ref, off, chunk_logits, m_ref, l_ref, acc_ref, p_ref, online=True):
    tq, tk = p_ref.shape[1:]
    n_sub = tk // LANES
    for h in range(N_HEADS):
        pair = h // 2
        q_h = q[:, h * HEAD_DIM:(h + 1) * HEAD_DIM]
        kt = kt_ref[0, h * HEAD_DIM:(h + 1) * HEAD_DIM, pl.ds(off, tk)]
        for r0 in range(0, tq, FLASH_ROWS):
            rows = pl.ds(r0, FLASH_ROWS)
            s = _dot(q_h[r0:r0 + FLASH_ROWS], kt)
            sc = [chunk_logits(h, r0, c, s[:, c * LANES:(c + 1) * LANES]) for c in range(n_sub)]
            m_row = m_ref[h, rows, :]
            if online:
                mx = sc[0]
                for x in sc[1:]:
                    mx = jnp.maximum(mx, x)
                m_old = m_row
                m_row = jnp.maximum(m_old, jnp.max(mx, axis=1, keepdims=True))
                alpha = jnp.exp2(m_old - m_row)
                m_ref[h, rows, :] = m_row
                acc_ref[h, rows, :] = alpha * acc_ref[h, rows, :]
            ps = [jnp.exp2(x - m_row) for x in sc]
            lsum = ps[0]
            for x in ps[1:]:
                lsum = lsum + x
            l_old = l_ref[h, rows, :]
            l_ref[h, rows, :] = (alpha * l_old if online else l_old) + lsum
            p_ref[h, rows, :] = jnp.concatenate(ps, axis=1).astype(BF16)
        v = v_ref[0, pl.ds(off, tk), pair * LANES:(pair + 1) * LANES]
        acc_ref[h] = acc_ref[h] + _dot(p_ref[h], v)


def _flash_finish(l_ref, acc_ref, o_ref):
    for pair in range(N_HEADS // 2):
        out = [acc_ref[h] / jnp.sum(l_ref[h], axis=1, keepdims=True) for h in (2 * pair, 2 * pair + 1)]
        lane = lax.broadcasted_iota(I32, out[0].shape, 1)
        o_ref[0, :, pair * LANES:(pair + 1) * LANES] = jnp.where(
            lane < HEAD_DIM, out[0], out[1]).astype(BF16)


def _dsa_kernel(bound_ref, qa_ref, qi_ref, small_ref, kat_ref, va_ref, kit_ref, o_ref, sc_ref, icut_ref,
                m_ref, l_ref, acc_ref, p_ref, *, tq, tk, topk):
    s_len = kat_ref.shape[2]
    t0 = pl.program_id(1) * tq
    nkb = (t0 + tq + tk - 1) // tk
    n_sub = tk // LANES
    row = lax.broadcasted_iota(I32, (tq, LANES), 0) + t0
    lane = lax.broadcasted_iota(I32, (COUNT_ROWS, LANES), 1)
    rep = lambda col: jnp.broadcast_to(col, (col.shape[0], LANES))

    wi = small_ref[0][:, SM_WI:SM_WI + N_HEADS]
    qi = qi_ref[0]
    lhs = []
    for h in range(N_HEADS):
        qh = qi[:, h * HEAD_DIM:(h + 1) * HEAD_DIM]
        hi = qh.astype(BF16)
        lo = (qh - hi.astype(F32)).astype(BF16)
        lhs.append(jnp.concatenate([hi, lo, hi], axis=1))
    lane_q = lax.broadcasted_iota(I32, (tq, LANES), 1)

    def score_body(kb, carry):
        mn, top1, top2 = carry
        off = pl.multiple_of(kb * tk, tk)
        kib = kit_ref[0, :, pl.ds(off, tk)]
        acc = jnp.zeros((tq, tk), F32)
        for h in range(N_HEADS):
            acc = acc + jnp.maximum(_dot(lhs[h], kib), 0.0) * wi[:, h:h + 1]
        for i in range(n_sub):
            a = acc[:, i * LANES:(i + 1) * LANES]
            causal = lane_q + (off + i * LANES) <= row
            lowered = jnp.where(causal, a, -jnp.inf)
            sc_ref[kb * n_sub + i] = lowered
            mn = jnp.minimum(mn, jnp.where(causal, a, jnp.inf))
            top2 = jnp.maximum(top2, jnp.minimum(top1, lowered))
            top1 = jnp.maximum(top1, lowered)
        return mn, top1, top2

    low = jnp.full((tq, LANES), -jnp.inf, F32)
    mn, _, top2 = lax.fori_loop(0, nkb, score_body, (jnp.full((tq, LANES), jnp.inf, F32), low, low))

    slabs = range(0, tq, COUNT_ROWS)

    def scan(step, init, *row_args, settled=None):
        outs = []
        for k, r0 in enumerate(slabs):
            args = [a[r0:r0 + COUNT_ROWS] for a in row_args]
            start = jnp.full((COUNT_ROWS, LANES), init, F32)

            def body(kb, acc, r0=r0, args=args):
                off = pl.multiple_of(kb * tk, tk)
                for i in range(n_sub):
                    blk = sc_ref[kb * n_sub + i, pl.ds(r0, COUNT_ROWS), :]
                    acc = step(acc, blk, lane + (off + i * LANES), *args)
                return acc

            run = functools.partial(lax.fori_loop, 0, nkb, body, start)
            outs.append(run() if settled is None
                        else lax.cond(settled[k] > 0.5, lambda start=start: start, run))
        return jnp.concatenate(outs, axis=0)

    def count(pred, *row_args, settled=None):
        c = scan(lambda acc, blk, pos, *a: acc + jnp.where(pred(blk, pos, *a), 1.0, 0.0),
                 0.0, *row_args, settled=settled)
        return rep(jnp.sum(c, axis=1, keepdims=True))

    def below_max(bound, settled):
        m = scan(lambda acc, blk, pos, b: jnp.maximum(acc, jnp.where(blk < b, blk, -jnp.inf)),
                 -jnp.inf, bound, settled=settled)
        return rep(jnp.max(m, axis=1, keepdims=True))

    def slab_settled(state):
        return tuple(jnp.min(state[r0:r0 + COUNT_ROWS]) for r0 in slabs)

    kf = float(topk)
    few = row < topk
    rmin = rep(jnp.min(mn, axis=1, keepdims=True))
    run_lo = rep(jnp.min(top2, axis=1, keepdims=True))
    run_hi = rep(jnp.max(top2, axis=1, keepdims=True))
    above = jnp.where(run_hi > 0.0, run_hi * (1.0 + 1e-6), run_hi * (1.0 - 1e-6)) + 1e-30
    lo0 = jnp.where(few, 0.0, jnp.maximum(run_lo, rmin))
    hi0 = jnp.where(few, 0.0, above)
    state0 = jnp.where(few, 1.0, 0.0)

    def search_cond(st):
        it, settled = st[0], st[1]
        return (it < BISECT_STEPS) & (functools.reduce(jnp.minimum, settled) < 0.5)

    def search_body(st):
        it, settled, lo, hi, state = st
        mid = 0.5 * lo + 0.5 * hi
        cnt = count(lambda blk, pos, m: blk >= m, mid, settled=settled)
        active = state == 0.0
        ge = cnt >= kf
        lo = jnp.where(active, jnp.where(ge, mid, lo), lo)
        hi = jnp.where(active, jnp.where(ge, hi, mid), hi)
        state = jnp.where(active, jnp.where(cnt == kf, 1.0, 0.0), state)
        return it + 1, slab_settled(state), lo, hi, state

    _, settled, lo, hi, state = lax.while_loop(
        search_cond, search_body, (jnp.int32(0), slab_settled(state0), lo0, hi0, state0))
    thr = jnp.where(few, -jnp.inf, lo)

    def snap_body(st):
        settled, hi, thr, state = st
        cand = below_max(hi, settled)
        cnt = count(lambda blk, pos, c: blk >= c, cand, settled=settled)
        active = state == 0.0
        found = cnt >= kf
        thr = jnp.where(active, jnp.where(found, cand, thr), thr)
        hi = jnp.where(active, jnp.where(found, hi, cand), hi)
        state = jnp.where(active, jnp.where(found, jnp.where(cnt == kf, 1.0, 2.0), 0.0), state)
        return slab_settled(state), hi, thr, state

    _, _, thr, state = lax.while_loop(
        lambda st: functools.reduce(jnp.minimum, st[0]) < 0.5, snap_body, (settled, hi, thr, state))

    icut_ref[...] = jnp.full(icut_ref.shape, s_len, I32)
    tied = state == 2.0

    @pl.when(jnp.max(state) > 1.5)
    def _():
        untied = tuple(jnp.where(jnp.max(state[r0:r0 + COUNT_ROWS]) < 1.5, 1.0, 0.0) for r0 in slabs)
        need = kf - count(lambda blk, pos, t: blk > t, thr, settled=untied)
        n_bits = max(1, (s_len - 1).bit_length())

        def idx_body(i, cut):
            cand = cut | jnp.left_shift(jnp.int32(1), n_bits - 1 - i)
            cnt = count(lambda blk, pos, t, cd: (blk == t) & (pos < cd), thr, cand, settled=untied)
            return jnp.where(cnt < need, cand, cut)

        cut = lax.fori_loop(0, n_bits, idx_body, jnp.zeros((tq, LANES), I32))
        icut_ref[...] = jnp.where(tied, cut, s_len)

    icut = icut_ref[...]

    last = []
    for r0 in range(0, tq, COUNT_ROWS):
        rows = pl.ds(r0, COUNT_ROWS)
        thr_r, icut_r, row_r = (a[r0:r0 + COUNT_ROWS] for a in (thr, icut, row))

        def bias_body(kb, far, rows=rows, thr_r=thr_r, icut_r=icut_r, row_r=row_r):
            off = pl.multiple_of(kb * tk, tk)
            for i in range(n_sub):
                blk = sc_ref[kb * n_sub + i, rows, :]
                pos = lane + (off + i * LANES)
                keep = ((blk > thr_r) | ((blk == thr_r) & (pos <= icut_r))) & (pos <= row_r)
                sc_ref[kb * n_sub + i, rows, :] = jnp.where(keep, 0.0, NEG)
                far = jnp.maximum(far, jnp.where(keep, pos, -1))
            return far

        far = lax.fori_loop(0, nkb, bias_body, jnp.full((COUNT_ROWS, LANES), -1, I32))
        last.append(rep(jnp.max(far.astype(F32), axis=1, keepdims=True)))
    last = jnp.concatenate(last, axis=0)

    qa = qa_ref[0]
    kpos = lax.broadcasted_iota(I32, (1, tk), 1)
    slopes = [LOG2E * 2.0 ** (-8.0 * (h + 1) / N_HEADS) for h in range(N_HEADS)]

    def attend(online):
        def attn_body(kb, _):
            off = pl.multiple_of(kb * tk, tk)
            rel = (kpos + (off - t0)).astype(F32)

            def chunk_logits(h, r0, c, x):
                mask = sc_ref[kb * n_sub + c, pl.ds(r0, FLASH_ROWS), :]
                return x + (mask + slopes[h] * rel[:, c * LANES:(c + 1) * LANES])

            _flash_step(qa, kat_ref, va_ref, off, chunk_logits, m_ref, l_ref, acc_ref, p_ref, online)
            return 0

        lax.fori_loop(0, nkb, attn_body, 0)
        _flash_finish(l_ref, acc_ref, o_ref)

    bound = bound_ref[0, 0]

    @pl.when(bound <= SAFE_BOUND)
    def _():
        _flash_reset(m_ref, l_ref, acc_ref)
        for h in range(N_HEADS):
            m_ref[h] = bound + slopes[h] * (last - t0.astype(F32))
        attend(online=False)

    @pl.when(bound > SAFE_BOUND)
    def _():
        _flash_reset(m_ref, l_ref, acc_ref)
        attend(online=True)


def _dsa_attn(bound, qa, qi, small, kat, va, kit, *, tq=512, tk=512):
    b, s, _ = qa.shape
    row = lambda i, j: (i, j, 0)
    whole = lambda i, j: (i, 0, 0)
    once = pl.Buffered(1)
    topk = min(TOPK, s // 4)
    assert topk > LANES and s % tq == 0 and s % tk == 0, (s, tq, tk)
    kern = functools.partial(_dsa_kernel, tq=tq, tk=tk, topk=topk)
    return pl.pallas_call(
        kern,
        grid=(b, s // tq),
        in_specs=[pl.BlockSpec(memory_space=pltpu.SMEM),
                  pl.BlockSpec((1, tq, WIDTH), row), pl.BlockSpec((1, tq, WIDTH), row),
                  pl.BlockSpec((1, tq, LANES), row),
                  pl.BlockSpec((1, WIDTH, s), whole, pipeline_mode=once),
                  pl.BlockSpec((1, s, WIDTH), whole, pipeline_mode=once),
                  pl.BlockSpec((1, 3 * HEAD_DIM, s), whole, pipeline_mode=once)],
        out_specs=pl.BlockSpec((1, tq, WIDTH), row),
        out_shape=jax.ShapeDtypeStruct((b, s, WIDTH), BF16),
        scratch_shapes=[pltpu.VMEM((s // LANES, tq, LANES), F32), pltpu.VMEM((tq, LANES), I32)]
        + _flash_scratch(tq, tk),
        compiler_params=_cparams(2),
        name="dsa_attn",
    )(bound, qa, qi, small, kat, va, kit)


def _fox_kernel(bound_ref, qf_ref, small_ref, smallt_ref, kft_ref, vf_ref, o_ref,
                m_ref, l_ref, acc_ref, p_ref, *, tq, tk):
    t0 = pl.program_id(1) * tq
    n_full = t0 // tk
    nkb = (t0 + tq + tk - 1) // tk
    row = lax.broadcasted_iota(I32, (FLASH_ROWS, LANES), 0) + t0
    col = lax.broadcasted_iota(I32, (FLASH_ROWS, LANES), 1)
    qf = qf_ref[0]
    f_t0 = smallt_ref[0, SM_F:SM_F + N_HEADS, pl.ds(pl.multiple_of(t0, LANES), LANES)][:, 0:1]

    def attend(online):
        def attn_body(kb, _, causal):
            off = pl.multiple_of(kb * tk, tk)
            decay = LOG2E * (f_t0 - smallt_ref[0, SM_F:SM_F + N_HEADS, pl.ds(off, tk)])

            def chunk_logits(h, r0, c, x):
                x = x + decay[h:h + 1, c * LANES:(c + 1) * LANES]
                if causal:
                    x = jnp.where(col + (off + c * LANES) <= row + r0, x, NEG)
                return x

            _flash_step(qf, kft_ref, vf_ref, off, chunk_logits, m_ref, l_ref, acc_ref, p_ref, online)
            return 0

        lax.fori_loop(0, n_full, functools.partial(attn_body, causal=False), 0)
        lax.fori_loop(n_full, nkb, functools.partial(attn_body, causal=True), 0)
        _flash_finish(l_ref, acc_ref, o_ref)

    bound = bound_ref[0, 0]

    @pl.when(bound <= SAFE_BOUND)
    def _():
        _flash_reset(m_ref, l_ref, acc_ref)
        f_t = small_ref[0][:, SM_F:SM_F + N_HEADS]
        for h in range(N_HEADS):
            top = bound + LOG2E * (f_t0[h:h + 1, :] - f_t[:, h:h + 1])
            m_ref[h] = jnp.broadcast_to(top, (tq, LANES))
        attend(online=False)

    @pl.when(bound > SAFE_BOUND)
    def _():
        _flash_reset(m_ref, l_ref, acc_ref)
        attend(online=True)


def _fox_attn(bound, qf, small, smallt, kft, vf, *, tq=512, tk=512):
    b, s, _ = qf.shape
    row = lambda i, j: (i, j, 0)
    whole = lambda i, j: (i, 0, 0)
    once = pl.Buffered(1)
    kern = functools.partial(_fox_kernel, tq=tq, tk=tk)
    return pl.pallas_call(
        kern,
        grid=(b, s // tq),
        in_specs=[pl.BlockSpec(memory_space=pltpu.SMEM),
                  pl.BlockSpec((1, tq, WIDTH), row), pl.BlockSpec((1, tq, LANES), row),
                  pl.BlockSpec((1, LANES, s), whole, pipeline_mode=once),
                  pl.BlockSpec((1, WIDTH, s), whole, pipeline_mode=once),
                  pl.BlockSpec((1, s, WIDTH), whole, pipeline_mode=once)],
        out_specs=pl.BlockSpec((1, tq, WIDTH), row),
        out_shape=jax.ShapeDtypeStruct((b, s, WIDTH), BF16),
        scratch_shapes=_flash_scratch(tq, tk),
        compiler_params=_cparams(2),
        name="fox_attn",
    )(bound, qf, small, smallt, kft, vf)


def _mix_kernel(x_ref, mod_ref, g_ref, ad_ref, af_ref, wgate_ref, wbd_ref, wbf_ref, wo_ref, o_ref):
    d = x_ref.shape[2]
    x = x_ref[0]
    shift = mod_ref[0, 3:4, :]
    scale = mod_ref[0, 4:5, :]
    gate = mod_ref[0, 5:6, :]
    h = _rms_adaln(x, g_ref[...], scale, shift).astype(BF16)
    ga = _dot(h, wgate_ref[:, 0:d])
    gb = _dot(h, wgate_ref[:, d:2 * d])
    y_dsa = _dot(ad_ref[0], wbd_ref[...])
    y_fox = _dot(af_ref[0], wbf_ref[...])
    merged = jax.nn.sigmoid(ga) * y_dsa + jax.nn.sigmoid(gb) * y_fox
    o_ref[0] = x + gate * _dot(merged.astype(BF16), wo_ref[...])


def _mix_out(x, mod, gain, a_dsa, a_fox, w_gate, w_br_dsa, w_br_fox, w_out, *, tm=512):
    b, s, d = x.shape
    row = lambda i, j: (i, j, 0)
    return pl.pallas_call(
        _mix_kernel,
        grid=(b, s // tm),
        in_specs=[pl.BlockSpec((1, tm, d), row),
                  pl.BlockSpec((1, N_MOD, d), lambda i, j: (i, 0, 0)),
                  _const_spec((1, d)),
                  pl.BlockSpec((1, tm, WIDTH), row), pl.BlockSpec((1, tm, WIDTH), row),
                  _const_spec(w_gate.shape), _const_spec(w_br_dsa.shape),
                  _const_spec(w_br_fox.shape), _const_spec(w_out.shape)],
        out_specs=pl.BlockSpec((1, tm, d), row),
        out_shape=jax.ShapeDtypeStruct((b, s, d), F32),
        compiler_params=_cparams(2),
        name="mix_out",
    )(x, mod, gain.reshape(1, d), a_dsa, a_fox, w_gate, w_br_dsa, w_br_fox, w_out)


def _layer(x, mod, norm1_g, ffn1_wg, ffn1_wu, ffn1_wd, norm2_g, w_in, b_forget,
           qn_dsa, kn_dsa, qn_fox, kn_fox, w_br_dsa, w_br_fox, w_out, norm3_g,
           ffn2_wg, ffn2_wu, ffn2_wd):
    d = x.shape[2]
    bf = lambda w: w.astype(BF16)
    x = _ffn(x, mod, norm1_g, bf(ffn1_wg), bf(ffn1_wu), bf(ffn1_wd), mod_base=0)

    n_main = 7 * WIDTH
    n_small = HEAD_DIM + 2 * N_HEADS
    w_main = bf(w_in[:, :n_main])
    w_small = jnp.zeros((d, LANES), BF16).at[:, :n_small].set(bf(w_in[:, n_main:n_main + n_small]))
    w_gate = bf(w_in[:, n_main + n_small:])
    b_small = jnp.zeros((1, LANES), F32).at[0, SM_F:SM_F + N_HEADS].set(b_forget)
    q_scale = LOG2E * HEAD_DIM ** -0.5
    q_gains = jnp.stack([jnp.tile(qn_dsa, N_HEADS), jnp.tile(qn_fox, N_HEADS)]) * q_scale
    k_gains = jnp.stack([kn_dsa, kn_fox], axis=1)

    qa, kat, va, qf, kft, vf, qi, small, smallt, kit = _in_proj(
        x, mod, norm2_g, w_main, w_small, b_small, q_gains, k_gains)
    def logit_bound(qn, kn):
        b = 1.05 * HEAD_DIM * q_scale * jnp.max(jnp.abs(qn)) * jnp.max(jnp.abs(kn))
        return b.reshape(1, 1).astype(F32)

    a_dsa = _dsa_attn(logit_bound(qn_dsa, kn_dsa), qa, qi, small, kat, va, kit)
    a_fox = _fox_attn(logit_bound(qn_fox, kn_fox), qf, small, smallt, kft, vf)
    x = _mix_out(x, mod, norm2_g, a_dsa, a_fox, w_gate, bf(w_br_dsa), bf(w_br_fox), bf(w_out))
    return _ffn(x, mod, norm3_g, bf(ffn2_wg), bf(ffn2_wu), bf(ffn2_wd), mod_base=6)


def kernel(x, c, ada_w, ada_b, norm1_g, ffn1_wg, ffn1_wu, ffn1_wd, norm2_g, w_in, b_forget,
           qn_dsa, kn_dsa, qn_fox, kn_fox, w_br_dsa, w_br_fox, w_out, norm3_g,
           ffn2_wg, ffn2_wu, ffn2_wd):
    per_layer = (norm1_g, ffn1_wg, ffn1_wu, ffn1_wd, norm2_g, w_in, b_forget,
                 qn_dsa, kn_dsa, qn_fox, kn_fox, w_br_dsa, w_br_fox, w_out, norm3_g,
                 ffn2_wg, ffn2_wu, ffn2_wd)
    for l in range(ada_w.shape[0]):
        mod = _adaln_mod(c, ada_w[l], ada_b[l])
        x = _layer(x, mod, *(p[l] for p in per_layer))
    return x
```

```python
import functools

import jax
import jax.numpy as jnp
from jax import lax
from jax.experimental import pallas as pl
from jax.experimental.pallas import tpu as pltpu

F32 = jnp.float32
BF16 = jnp.bfloat16
I32 = jnp.int32

HEAD_DIM = 64
N_HEADS = 8
WIDTH = N_HEADS * HEAD_DIM
TOPK = 256
EPS = 1e-6
N_MOD = 9
LANES = 128
NEG = -1e30
BISECT_STEPS = 13
VMEM_LIMIT = 56 * 1024 * 1024

SM_WI = 64
SM_F = 72


def _cparams(n_axes):
    return pltpu.CompilerParams(
        dimension_semantics=("arbitrary",) * n_axes, vmem_limit_bytes=VMEM_LIMIT)


def _const_spec(shape):
    nd = len(shape)
    return pl.BlockSpec(shape, lambda *_: (0,) * nd)


def _dot(a, b):
    return jnp.dot(a, b, preferred_element_type=F32)


def _rms_adaln(x, gain, scale, shift):
    y = x * lax.rsqrt(jnp.mean(x * x, axis=-1, keepdims=True) + EPS)
    return (y * gain) * (1.0 + scale) + shift


def _mod_kernel(c_ref, w_ref, b_ref, o_ref):
    c = c_ref[...]
    a = c * jax.nn.sigmoid(c)
    o_ref[...] = jnp.dot(a, w_ref[...], preferred_element_type=F32,
                         precision=lax.Precision.HIGHEST) + b_ref[...]


def _adaln_mod(c, ada_w, ada_b):
    b, d = c.shape
    n = ada_w.shape[1]
    rows = 8
    tn = n // 8
    c_pad = jnp.zeros((rows, d), F32).at[:b].set(c)
    out = pl.pallas_call(
        _mod_kernel,
        grid=(n // tn,),
        in_specs=[pl.BlockSpec((rows, d), lambda j: (0, 0)),
                  pl.BlockSpec((d, tn), lambda j: (0, j)),
                  pl.BlockSpec((1, tn), lambda j: (0, j))],
        out_specs=pl.BlockSpec((rows, tn), lambda j: (0, j)),
        out_shape=jax.ShapeDtypeStruct((rows, n), F32),
        compiler_params=_cparams(1),
        name="adaln_mod",
    )(c_pad, ada_w, ada_b.reshape(1, n))
    return out[:b].reshape(b, N_MOD, d)


def _ffn_kernel(x_ref, mod_ref, g_ref, wg_ref, wu_ref, wd_ref, o_ref, *, mod_base, n_chunks):
    x = x_ref[0]
    shift = mod_ref[0, mod_base:mod_base + 1, :]
    scale = mod_ref[0, mod_base + 1:mod_base + 2, :]
    gate = mod_ref[0, mod_base + 2:mod_base + 3, :]
    h = _rms_adaln(x, g_ref[...], scale, shift).astype(BF16)
    dff = wg_ref.shape[1]
    ck = dff // n_chunks
    acc = jnp.zeros(x.shape, F32)
    for i in range(n_chunks):
        g = _dot(h, wg_ref[:, i * ck:(i + 1) * ck])
        u = _dot(h, wu_ref[:, i * ck:(i + 1) * ck])
        a = (g * jax.nn.sigmoid(g) * u).astype(BF16)
        acc = acc + _dot(a, wd_ref[i * ck:(i + 1) * ck, :])
    o_ref[0] = x + (0.5 * gate) * acc


def _ffn(x, mod, gain, wg, wu, wd, *, mod_base, tm=512):
    b, s, d = x.shape
    dff = wg.shape[1]
    kern = functools.partial(_ffn_kernel, mod_base=mod_base, n_chunks=2)
    return pl.pallas_call(
        kern,
        grid=(b, s // tm),
        in_specs=[pl.BlockSpec((1, tm, d), lambda i, j: (i, j, 0)),
                  pl.BlockSpec((1, N_MOD, d), lambda i, j: (i, 0, 0)),
                  _const_spec((1, d)),
                  _const_spec((d, dff)), _const_spec((d, dff)), _const_spec((dff, d))],
        out_specs=pl.BlockSpec((1, tm, d), lambda i, j: (i, j, 0)),
        out_shape=jax.ShapeDtypeStruct((b, s, d), F32),
        compiler_params=_cparams(2),
        name="ffn",
    )(x, mod, gain.reshape(1, d), wg, wu, wd)


def _split3_bf16(v):
    p1 = v.astype(BF16)
    r1 = v - p1.astype(F32)
    p2 = r1.astype(BF16)
    r2 = r1 - p2.astype(F32)
    return p1, p2, r2.astype(BF16)


def _proj_kernel(x_ref, mod_ref, g_ref, wm_ref, ws_ref, bsm_ref, qg_ref, kg_ref,
                 qa_ref, kat_ref, va_ref, qf_ref, kft_ref, vf_ref, qi_ref,
                 small_ref, smallt_ref, kit_ref, carry_ref):
    tm = x_ref.shape[1]
    x = x_ref[0]
    shift = mod_ref[0, 3:4, :]
    scale = mod_ref[0, 4:5, :]
    h = _rms_adaln(x, g_ref[...], scale, shift).astype(BF16)

    def z(i):
        return _dot(h, wm_ref[:, i * WIDTH:(i + 1) * WIDTH])

    r = lax.broadcasted_iota(I32, (WIDTH, WIDTH), 0) // HEAD_DIM
    c = lax.broadcasted_iota(I32, (WIDTH, WIDTH), 1) // HEAD_DIM
    avg = jnp.where(r == c, 1.0 / HEAD_DIM, 0.0).astype(BF16)

    def norm_q(q, gain_row):
        ms = _dot((q * q).astype(BF16), avg)
        return (q * lax.rsqrt(ms + EPS) * gain_row).astype(BF16)

    def norm_kt(k, gain_col):
        kt = k.T.reshape(N_HEADS, HEAD_DIM, tm)
        ms = jnp.mean(kt * kt, axis=1, keepdims=True)
        kt = kt * lax.rsqrt(ms + EPS) * gain_col[None]
        return kt.reshape(WIDTH, tm).astype(BF16)

    qa_ref[0] = norm_q(z(0), qg_ref[0:1, :])
    kat_ref[0] = norm_kt(z(1), kg_ref[:, 0:1])
    va_ref[0] = z(2).astype(BF16)
    qf_ref[0] = norm_q(z(3), qg_ref[1:2, :])
    kft_ref[0] = norm_kt(z(4), kg_ref[:, 1:2])
    vf_ref[0] = z(5).astype(BF16)
    qi_ref[0] = z(6) * (HEAD_DIM ** -0.5)

    zs = _dot(h, ws_ref[...])
    pre = zs + bsm_ref[...]
    logf = jnp.minimum(pre, 0.0) - jnp.log(1.0 + jnp.exp(-jnp.abs(pre)))

    @pl.when(pl.program_id(1) == 0)
    def _():
        carry_ref[...] = jnp.zeros_like(carry_ref)

    ri = lax.broadcasted_iota(I32, (tm, tm), 0)
    ci = lax.broadcasted_iota(I32, (tm, tm), 1)
    tri = jnp.where(ci <= ri, 1.0, 0.0).astype(BF16)
    p1, p2, p3 = _split3_bf16(logf)
    cum = (_dot(tri, p1) + _dot(tri, p2)) + _dot(tri, p3) + carry_ref[...]
    carry_ref[...] = cum[tm - 1:tm, :]

    lane = lax.broadcasted_iota(I32, (tm, LANES), 1)
    small = jnp.where(lane < SM_WI, zs,
                      jnp.where(lane < SM_F, zs * (N_HEADS ** -0.5), cum))
    small_ref[0] = small
    st = small.T
    smallt_ref[0] = st
    ki = st[0:HEAD_DIM, :]
    ki_hi = ki.astype(BF16)
    ki_lo = (ki - ki_hi.astype(F32)).astype(BF16)
    kit_ref[0] = jnp.concatenate([ki_hi, ki_hi, ki_lo], axis=0)


def _in_proj(x, mod, gain, w_main, w_small, b_small, q_gains, k_gains, *, tm=512):
    b, s, d = x.shape
    row = lambda i, j: (i, j, 0)
    col = lambda i, j: (i, 0, j)
    sd = jax.ShapeDtypeStruct
    out_shape = [sd((b, s, WIDTH), BF16), sd((b, WIDTH, s), BF16), sd((b, s, WIDTH), BF16),
                 sd((b, s, WIDTH), BF16), sd((b, WIDTH, s), BF16), sd((b, s, WIDTH), BF16),
                 sd((b, s, WIDTH), F32),
                 sd((b, s, LANES), F32), sd((b, LANES, s), F32), sd((b, 3 * HEAD_DIM, s), BF16)]
    rspec = pl.BlockSpec((1, tm, WIDTH), row)
    cspec = pl.BlockSpec((1, WIDTH, tm), col)
    out_specs = [rspec, cspec, rspec, rspec, cspec, rspec, rspec,
                 pl.BlockSpec((1, tm, LANES), row), pl.BlockSpec((1, LANES, tm), col),
                 pl.BlockSpec((1, 3 * HEAD_DIM, tm), col)]
    return pl.pallas_call(
        _proj_kernel,
        grid=(b, s // tm),
        in_specs=[pl.BlockSpec((1, tm, d), row),
                  pl.BlockSpec((1, N_MOD, d), lambda i, j: (i, 0, 0)),
                  _const_spec((1, d)),
                  _const_spec(w_main.shape), _const_spec(w_small.shape),
                  _const_spec((1, LANES)), _const_spec((2, WIDTH)), _const_spec((HEAD_DIM, 2))],
        out_specs=out_specs,
        out_shape=out_shape,
        scratch_shapes=[pltpu.VMEM((1, LANES), F32)],
        compiler_params=_cparams(2),
        name="in_proj",
    )(x, mod, gain.reshape(1, d), w_main, w_small, b_small, q_gains, k_gains)


FLASH_ROWS = 64
COUNT_ROWS = 128
LOG2E = 1.4426950408889634
SAFE_BOUND = 50.0


def _flash_scratch(tq, tk):
    return [pltpu.VMEM((N_HEADS, tq, LANES), F32), pltpu.VMEM((N_HEADS, tq, LANES), F32),
            pltpu.VMEM((N_HEADS, tq, LANES), F32), pltpu.VMEM((N_HEADS, tq, tk), BF16)]


def _flash_reset(m_ref, l_ref, acc_ref):
    m_ref[...] = jnp.full(m_ref.shape, NEG, F32)
    l_ref[...] = jnp.zeros(l_ref.shape, F32)
    acc_ref[...] = jnp.zeros(acc_ref.shape, F32)


def _flash_step(q, kt_ref, v_ref, off, chunk_logits, m_ref, l_ref, acc_ref, p_ref, online=True):
    tq, tk = p_ref.shape[1:]
    n_sub = tk // LANES
    for h in range(N_HEADS):
        pair = h // 2
        q_h = q[:, h * HEAD_DIM:(h + 1) * HEAD_DIM]
        kt = kt_ref[0, h * HEAD_DIM:(h + 1) * HEAD_DIM, pl.ds(off, tk)]
        for r0 in range(0, tq, FLASH_ROWS):
            rows = pl.ds(r0, FLASH_ROWS)
            s = _dot(q_h[r0:r0 + FLASH_ROWS], kt)
            sc = [chunk_logits(h, r0, c, s[:, c * LANES:(c + 1) * LANES]) for c in range(n_sub)]
            m_row = m_ref[h, rows, :]
            if online:
                mx = sc[0]
                for x in sc[1:]:
                    mx = jnp.maximum(mx, x)
                m_old = m_row
                m_row = jnp.maximum(m_old, jnp.max(mx, axis=1, keepdims=True))
                alpha = jnp.exp2(m_old - m_row)
                m_ref[h, rows, :] = m_row
                acc_ref[h, rows, :] = alpha * acc_ref[h, rows, :]
            ps = [jnp.exp2(x - m_row) for x in sc]
            lsum = ps[0]
            for x in ps[1:]:
                lsum = lsum + x
            l_old = l_ref[h, rows, :]
            l_ref[h, rows, :] = (alpha * l_old if online else l_old) + lsum
            p_ref[h, rows, :] = jnp.concatenate(ps, axis=1).astype(BF16)
        v = v_ref[0, pl.ds(off, tk), pair * LANES:(pair + 1) * LANES]
        acc_ref[h] = acc_ref[h] + _dot(p_ref[h], v)


def _flash_finish(l_ref, acc_ref, o_ref):
    for pair in range(N_HEADS // 2):
        out = [acc_ref[h] / jnp.sum(l_ref[h], axis=1, keepdims=True) for h in (2 * pair, 2 * pair + 1)]
        lane = lax.broadcasted_iota(I32, out[0].shape, 1)
        o_ref[0, :, pair * LANES:(pair + 1) * LANES] = jnp.where(
            lane < HEAD_DIM, out[0], out[1]).astype(BF16)


def _dsa_kernel(bound_ref, qa_ref, qi_ref, small_ref, kat_ref, va_ref, kit_ref, o_ref, sc_ref,
                m_ref, l_ref, acc_ref, p_ref, *, tq, tk, topk):
    s_len = kat_ref.shape[2]
    t0 = pl.program_id(1) * tq
    nkb = (t0 + tq + tk - 1) // tk
    n_sub = tk // LANES
    row = lax.broadcasted_iota(I32, (tq, LANES), 0) + t0
    lane = lax.broadcasted_iota(I32, (COUNT_ROWS, LANES), 1)
    rep = lambda col: jnp.broadcast_to(col, (col.shape[0], LANES))

    wi = small_ref[0][:, SM_WI:SM_WI + N_HEADS]
    qi = qi_ref[0]
    lhs = []
    for h in range(N_HEADS):
        qh = qi[:, h * HEAD_DIM:(h + 1) * HEAD_DIM]
        hi = qh.astype(BF16)
        lo = (qh - hi.astype(F32)).astype(BF16)
        lhs.append(jnp.concatenate([hi, lo, hi], axis=1))
    lane_q = lax.broadcasted_iota(I32, (tq, LANES), 1)

    def score_body(kb, carry):
        mn, top1, top2 = carry
        off = pl.multiple_of(kb * tk, tk)
        kib = kit_ref[0, :, pl.ds(off, tk)]
        acc = jnp.zeros((tq, tk), F32)
        for h in range(N_HEADS):
            acc = acc + jnp.maximum(_dot(lhs[h], kib), 0.0) * wi[:, h:h + 1]
        for i in range(n_sub):
            a = acc[:, i * LANES:(i + 1) * LANES]
            causal = lane_q + (off + i * LANES) <= row
            lowered = jnp.where(causal, a, -jnp.inf)
            sc_ref[kb * n_sub + i] = lowered
            mn = jnp.minimum(mn, jnp.where(causal, a, jnp.inf))
            top2 = jnp.maximum(top2, jnp.minimum(top1, lowered))
            top1 = jnp.maximum(top1, lowered)
        return mn, top1, top2

    low = jnp.full((tq, LANES), -jnp.inf, F32)
    mn, _, top2 = lax.fori_loop(0, nkb, score_body, (jnp.full((tq, LANES), jnp.inf, F32), low, low))

    slabs = range(0, tq, COUNT_ROWS)

    def scan(step, init, *row_args, settled=None):
        outs = []
        for k, r0 in enumerate(slabs):
            args = [a[r0:r0 + COUNT_ROWS] for a in row_args]
            start = jnp.full((COUNT_ROWS, LANES), init, F32)

            def body(kb, acc, r0=r0, args=args):
                off = pl.multiple_of(kb * tk, tk)
                for i in range(n_sub):
                    blk = sc_ref[kb * n_sub + i, pl.ds(r0, COUNT_ROWS), :]
                    acc = step(acc, blk, lane + (off + i * LANES), *args)
                return acc

            run = functools.partial(lax.fori_loop, 0, nkb, body, start)
            outs.append(run() if settled is None
                        else lax.cond(settled[k] > 0.5, lambda start=start: start, run))
        return jnp.concatenate(outs, axis=0)

    def count(pred, *row_args, settled=None):
        c = scan(lambda acc, blk, pos, *a: acc + jnp.where(pred(blk, pos, *a), 1.0, 0.0),
                 0.0, *row_args, settled=settled)
        return rep(jnp.sum(c, axis=1, keepdims=True))

    def below_max(bound, settled):
        m = scan(lambda acc, blk, pos, b: jnp.maximum(acc, jnp.where(blk < b, blk, -jnp.inf)),
                 -jnp.inf, bound, settled=settled)
        return rep(jnp.max(m, axis=1, keepdims=True))

    def slab_settled(state):
        return tuple(jnp.min(state[r0:r0 + COUNT_ROWS]) for r0 in slabs)

    kf = float(topk)
    few = row < topk
    rmin = rep(jnp.min(mn, axis=1, keepdims=True))
    run_lo = rep(jnp.min(top2, axis=1, keepdims=True))
    run_hi = rep(jnp.max(top2, axis=1, keepdims=True))
    above = jnp.where(run_hi > 0.0, run_hi * (1.0 + 1e-6), run_hi * (1.0 - 1e-6)) + 1e-30
    lo0 = jnp.where(few, 0.0, jnp.maximum(run_lo, rmin))
    hi0 = jnp.where(few, 0.0, above)
    state0 = jnp.where(few, 1.0, 0.0)

    def search_cond(st):
        it, settled = st[0], st[1]
        return (it < BISECT_STEPS) & (functools.reduce(jnp.minimum, settled) < 0.5)

    def search_body(st):
        it, settled, lo, hi, state = st
        mid = 0.5 * lo + 0.5 * hi
        cnt = count(lambda blk, pos, m: blk >= m, mid, settled=settled)
        active = state == 0.0
        ge = cnt >= kf
        lo = jnp.where(active, jnp.where(ge, mid, lo), lo)
        hi = jnp.where(active, jnp.where(ge, hi, mid), hi)
        state = jnp.where(active, jnp.where(cnt == kf, 1.0, 0.0), state)
        return it + 1, slab_settled(state), lo, hi, state

    _, settled, lo, hi, state = lax.while_loop(
        search_cond, search_body, (jnp.int32(0), slab_settled(state0), lo0, hi0, state0))
    thr = jnp.where(few, -jnp.inf, lo)

    def snap_body(st):
        settled, hi, thr, state = st
        cand = below_max(hi, settled)
        cnt = count(lambda blk, pos, c: blk >= c, cand, settled=settled)
        active = state == 0.0
        found = cnt >= kf
        thr = jnp.where(active, jnp.where(found, cand, thr), thr)
        hi = jnp.where(active, jnp.where(found, hi, cand), hi)
        state = jnp.where(active, jnp.where(found, jnp.where(cnt == kf, 1.0, 2.0), 0.0), state)
        return slab_settled(state), hi, thr, state

    _, _, thr, state = lax.while_loop(
        lambda st: functools.reduce(jnp.minimum, st[0]) < 0.5, snap_body, (settled, hi, thr, state))

    last = []
    for r0 in slabs:
        rows = pl.ds(r0, COUNT_ROWS)
        thr_r, row_r, state_r = (a[r0:r0 + COUNT_ROWS] for a in (thr, row, state))
        far0 = jnp.full((COUNT_ROWS, LANES), -1, I32)

        def mark(kb, i, keep, far, rows=rows, row_r=row_r):
            pos = lane + (kb * tk + i * LANES)
            keep = keep & (pos <= row_r)
            sc_ref[kb * n_sub + i, rows, :] = jnp.where(keep, 0.0, NEG)
            return jnp.maximum(far, jnp.where(keep, pos, -1))

        def plain(rows=rows, thr_r=thr_r, mark=mark, far0=far0):
            def body(kb, far):
                for i in range(n_sub):
                    far = mark(kb, i, sc_ref[kb * n_sub + i, rows, :] >= thr_r, far)
                return far
            return lax.fori_loop(0, nkb, body, far0)

        def tied(rows=rows, thr_r=thr_r, mark=mark, far0=far0):
            def above(kb, c):
                for i in range(n_sub):
                    c = c + jnp.where(sc_ref[kb * n_sub + i, rows, :] > thr_r, 1.0, 0.0)
                return c
            c = lax.fori_loop(0, nkb, above, jnp.zeros((COUNT_ROWS, LANES), F32))
            need = kf - rep(jnp.sum(c, axis=1, keepdims=True))
            tri = jnp.where(lax.broadcasted_iota(I32, (tk, tk), 0) <= lax.broadcasted_iota(I32, (tk, tk), 1),
                            1.0, 0.0).astype(BF16)

            def body(kb, carry):
                far, seen = carry
                blks = [sc_ref[kb * n_sub + i, rows, :] for i in range(n_sub)]
                is_tie = jnp.concatenate([jnp.where(blk == thr_r, 1.0, 0.0) for blk in blks], axis=1)
                rank = _dot(is_tie.astype(BF16), tri)
                for i, blk in enumerate(blks):
                    rank_i = seen + rank[:, i * LANES:(i + 1) * LANES]
                    far = mark(kb, i, (blk > thr_r) | ((blk == thr_r) & (rank_i <= need)), far)
                return far, seen + rep(rank[:, tk - 1:tk])
            return lax.fori_loop(0, nkb, body, (far0, jnp.zeros((COUNT_ROWS, LANES), F32)))[0]

        far = lax.cond(jnp.max(state_r) > 1.5, tied, plain)
        last.append(rep(jnp.max(far.astype(F32), axis=1, keepdims=True)))
    last = jnp.concatenate(last, axis=0)

    qa = qa_ref[0]
    kpos = lax.broadcasted_iota(I32, (1, tk), 1)
    slopes = [LOG2E * 2.0 ** (-8.0 * (h + 1) / N_HEADS) for h in range(N_HEADS)]

    def attend(online):
        def attn_body(kb, _):
            off = pl.multiple_of(kb * tk, tk)
            rel = (kpos + (off - t0)).astype(F32)

            def chunk_logits(h, r0, c, x):
                mask = sc_ref[kb * n_sub + c, pl.ds(r0, FLASH_ROWS), :]
                return x + (mask + slopes[h] * rel[:, c * LANES:(c + 1) * LANES])

            _flash_step(qa, kat_ref, va_ref, off, chunk_logits, m_ref, l_ref, acc_ref, p_ref, online)
            return 0

        lax.fori_loop(0, nkb, attn_body, 0)
        _flash_finish(l_ref, acc_ref, o_ref)

    bound = bound_ref[0, 0]

    @pl.when(bound <= SAFE_BOUND)
    def _():
        _flash_reset(m_ref, l_ref, acc_ref)
        for h in range(N_HEADS):
            m_ref[h] = bound + slopes[h] * (last - t0.astype(F32))
        attend(online=False)

    @pl.when(bound > SAFE_BOUND)
    def _():
        _flash_reset(m_ref, l_ref, acc_ref)
        attend(online=True)


def _dsa_attn(bound, qa, qi, small, kat, va, kit, *, tq=512, tk=512):
    b, s, _ = qa.shape
    row = lambda i, j: (i, j, 0)
    whole = lambda i, j: (i, 0, 0)
    once = pl.Buffered(1)
    topk = min(TOPK, s // 4)
    assert topk > LANES and s % tq == 0 and s % tk == 0, (s, tq, tk)
    kern = functools.partial(_dsa_kernel, tq=tq, tk=tk, topk=topk)
    return pl.pallas_call(
        kern,
        grid=(b, s // tq),
        in_specs=[pl.BlockSpec(memory_space=pltpu.SMEM),
                  pl.BlockSpec((1, tq, WIDTH), row), pl.BlockSpec((1, tq, WIDTH), row),
                  pl.BlockSpec((1, tq, LANES), row),
                  pl.BlockSpec((1, WIDTH, s), whole, pipeline_mode=once),
                  pl.BlockSpec((1, s, WIDTH), whole, pipeline_mode=once),
                  pl.BlockSpec((1, 3 * HEAD_DIM, s), whole, pipeline_mode=once)],
        out_specs=pl.BlockSpec((1, tq, WIDTH), row),
        out_shape=jax.ShapeDtypeStruct((b, s, WIDTH), BF16),
        scratch_shapes=[pltpu.VMEM((s // LANES, tq, LANES), F32)] + _flash_scratch(tq, tk),
        compiler_params=_cparams(2),
        name="dsa_attn",
    )(bound, qa, qi, small, kat, va, kit)


def _fox_kernel(bound_ref, qf_ref, small_ref, smallt_ref, kft_ref, vf_ref, o_ref,
                m_ref, l_ref, acc_ref, p_ref, *, tq, tk):
    t0 = pl.program_id(1) * tq
    n_full = t0 // tk
    nkb = (t0 + tq + tk - 1) // tk
    row = lax.broadcasted_iota(I32, (FLASH_ROWS, LANES), 0) + t0
    col = lax.broadcasted_iota(I32, (FLASH_ROWS, LANES), 1)
    qf = qf_ref[0]
    f_t0 = smallt_ref[0, SM_F:SM_F + N_HEADS, pl.ds(pl.multiple_of(t0, LANES), LANES)][:, 0:1]

    def attend(online):
        def attn_body(kb, _, causal):
            off = pl.multiple_of(kb * tk, tk)
            decay = LOG2E * (f_t0 - smallt_ref[0, SM_F:SM_F + N_HEADS, pl.ds(off, tk)])

            def chunk_logits(h, r0, c, x):
                x = x + decay[h:h + 1, c * LANES:(c + 1) * LANES]
                if causal:
                    x = jnp.where(col + (off + c * LANES) <= row + r0, x, NEG)
                return x

            _flash_step(qf, kft_ref, vf_ref, off, chunk_logits, m_ref, l_ref, acc_ref, p_ref, online)
            return 0

        lax.fori_loop(0, n_full, functools.partial(attn_body, causal=False), 0)
        lax.fori_loop(n_full, nkb, functools.partial(attn_body, causal=True), 0)
        _flash_finish(l_ref, acc_ref, o_ref)

    bound = bound_ref[0, 0]

    @pl.when(bound <= SAFE_BOUND)
    def _():
        _flash_reset(m_ref, l_ref, acc_ref)
        f_t = small_ref[0][:, SM_F:SM_F + N_HEADS]
        for h in range(N_HEADS):
            top = bound + LOG2E * (f_t0[h:h + 1, :] - f_t[:, h:h + 1])
            m_ref[h] = jnp.broadcast_to(top, (tq, LANES))
        attend(online=False)

    @pl.when(bound > SAFE_BOUND)
    def _():
        _flash_reset(m_ref, l_ref, acc_ref)
        attend(online=True)


def _fox_attn(bound, qf, small, smallt, kft, vf, *, tq=512, tk=512):
    b, s, _ = qf.shape
    row = lambda i, j: (i, j, 0)
    whole = lambda i, j: (i, 0, 0)
    once = pl.Buffered(1)
    kern = functools.partial(_fox_kernel, tq=tq, tk=tk)
    return pl.pallas_call(
        kern,
        grid=(b, s // tq),
        in_specs=[pl.BlockSpec(memory_space=pltpu.SMEM),
                  pl.BlockSpec((1, tq, WIDTH), row), pl.BlockSpec((1, tq, LANES), row),
                  pl.BlockSpec((1, LANES, s), whole, pipeline_mode=once),
                  pl.BlockSpec((1, WIDTH, s), whole, pipeline_mode=once),
                  pl.BlockSpec((1, s, WIDTH), whole, pipeline_mode=once)],
        out_specs=pl.BlockSpec((1, tq, WIDTH), row),
        out_shape=jax.ShapeDtypeStruct((b, s, WIDTH), BF16),
        scratch_shapes=_flash_scratch(tq, tk),
        compiler_params=_cparams(2),
        name="fox_attn",
    )(bound, qf, small, smallt, kft, vf)


def _mix_kernel(x_ref, mod_ref, g_ref, ad_ref, af_ref, wgate_ref, wbd_ref, wbf_ref, wo_ref, o_ref):
    d = x_ref.shape[2]
    x = x_ref[0]
    shift = mod_ref[0, 3:4, :]
    scale = mod_ref[0, 4:5, :]
    gate = mod_ref[0, 5:6, :]
    h = _rms_adaln(x, g_ref[...], scale, shift).astype(BF16)
    ga = _dot(h, wgate_ref[:, 0:d])
    gb = _dot(h, wgate_ref[:, d:2 * d])
    y_dsa = _dot(ad_ref[0], wbd_ref[...])
    y_fox = _dot(af_ref[0], wbf_ref[...])
    merged = jax.nn.sigmoid(ga) * y_dsa + jax.nn.sigmoid(gb) * y_fox
    o_ref[0] = x + gate * _dot(merged.astype(BF16), wo_ref[...])


def _mix_out(x, mod, gain, a_dsa, a_fox, w_gate, w_br_dsa, w_br_fox, w_out, *, tm=512):
    b, s, d = x.shape
    row = lambda i, j: (i, j, 0)
    return pl.pallas_call(
        _mix_kernel,
        grid=(b, s // tm),
        in_specs=[pl.BlockSpec((1, tm, d), row),
                  pl.BlockSpec((1, N_MOD, d), lambda i, j: (i, 0, 0)),
                  _const_spec((1, d)),
                  pl.BlockSpec((1, tm, WIDTH), row), pl.BlockSpec((1, tm, WIDTH), row),
                  _const_spec(w_gate.shape), _const_spec(w_br_dsa.shape),
                  _const_spec(w_br_fox.shape), _const_spec(w_out.shape)],
        out_specs=pl.BlockSpec((1, tm, d), row),
        out_shape=jax.ShapeDtypeStruct((b, s, d), F32),
        compiler_params=_cparams(2),
        name="mix_out",
    )(x, mod, gain.reshape(1, d), a_dsa, a_fox, w_gate, w_br_dsa, w_br_fox, w_out)


def _layer(x, mod, norm1_g, ffn1_wg, ffn1_wu, ffn1_wd, norm2_g, w_in, b_forget,
           qn_dsa, kn_dsa, qn_fox, kn_fox, w_br_dsa, w_br_fox, w_out, norm3_g,
           ffn2_wg, ffn2_wu, ffn2_wd):
    d = x.shape[2]
    bf = lambda w: w.astype(BF16)
    x = _ffn(x, mod, norm1_g, bf(ffn1_wg), bf(ffn1_wu), bf(ffn1_wd), mod_base=0)

    n_main = 7 * WIDTH
    n_small = HEAD_DIM + 2 * N_HEADS
    w_main = bf(w_in[:, :n_main])
    w_small = jnp.zeros((d, LANES), BF16).at[:, :n_small].set(bf(w_in[:, n_main:n_main + n_small]))
    w_gate = bf(w_in[:, n_main + n_small:])
    b_small = jnp.zeros((1, LANES), F32).at[0, SM_F:SM_F + N_HEADS].set(b_forget)
    q_scale = LOG2E * HEAD_DIM ** -0.5
    q_gains = jnp.stack([jnp.tile(qn_dsa, N_HEADS), jnp.tile(qn_fox, N_HEADS)]) * q_scale
    k_gains = jnp.stack([kn_dsa, kn_fox], axis=1)

    qa, kat, va, qf, kft, vf, qi, small, smallt, kit = _in_proj(
        x, mod, norm2_g, w_main, w_small, b_small, q_gains, k_gains)
    def logit_bound(qn, kn):
        b = 1.05 * HEAD_DIM * q_scale * jnp.max(jnp.abs(qn)) * jnp.max(jnp.abs(kn))
        return b.reshape(1, 1).astype(F32)

    a_dsa = _dsa_attn(logit_bound(qn_dsa, kn_dsa), qa, qi, small, kat, va, kit)
    a_fox = _fox_attn(logit_bound(qn_fox, kn_fox), qf, small, smallt, kft, vf)
    x = _mix_out(x, mod, norm2_g, a_dsa, a_fox, w_gate, bf(w_br_dsa), bf(w_br_fox), bf(w_out))
    return _ffn(x, mod, norm3_g, bf(ffn2_wg), bf(ffn2_wu), bf(ffn2_wd), mod_base=6)


def kernel(x, c, ada_w, ada_b, norm1_g, ffn1_wg, ffn1_wu, ffn1_wd, norm2_g, w_in, b_forget,
           qn_dsa, kn_dsa, qn_fox, kn_fox, w_br_dsa, w_br_fox, w_out, norm3_g,
           ffn2_wg, ffn2_wu, ffn2_wd):
    per_layer = (norm1_g, ffn1_wg, ffn1_wu, ffn1_wd, norm2_g, w_in, b_forget,
                 qn_dsa, kn_dsa, qn_fox, kn_fox, w_br_dsa, w_br_fox, w_out, norm3_g,
                 ffn2_wg, ffn2_wu, ffn2_wd)
    for l in range(ada_w.shape[0]):
        mod = _adaln_mod(c, ada_w[l], ada_b[l])
        x = _layer(x, mod, *(p[l] for p in per_layer))
    return x
```

```python
import functools

import jax
import jax.numpy as jnp
from jax import lax
from jax.experimental import pallas as pl
from jax.experimental.pallas import tpu as pltpu

F32 = jnp.float32
BF16 = jnp.bfloat16
I32 = jnp.int32

HEAD_DIM = 64
N_HEADS = 8
WIDTH = N_HEADS * HEAD_DIM
TOPK = 256
EPS = 1e-6
N_MOD = 9
LANES = 128
NEG = -1e30
BISECT_STEPS = 13
VMEM_LIMIT = 56 * 1024 * 1024

SM_WI = 64
SM_F = 72


def _cparams(n_axes):
    return pltpu.CompilerParams(
        dimension_semantics=("arbitrary",) * n_axes, vmem_limit_bytes=VMEM_LIMIT)


def _const_spec(shape):
    nd = len(shape)
    return pl.BlockSpec(shape, lambda *_: (0,) * nd)


def _dot(a, b):
    return jnp.dot(a, b, preferred_element_type=F32)


def _rms_adaln(x, gain, scale, shift):
    y = x * lax.rsqrt(jnp.mean(x * x, axis=-1, keepdims=True) + EPS)
    return (y * gain) * (1.0 + scale) + shift


def _mod_kernel(c_ref, w_ref, b_ref, o_ref):
    c = c_ref[...]
    a = c * jax.nn.sigmoid(c)
    o_ref[...] = jnp.dot(a, w_ref[...], preferred_element_type=F32,
                         precision=lax.Precision.HIGHEST) + b_ref[...]


def _adaln_mod(c, ada_w, ada_b):
    b, d = c.shape
    n = ada_w.shape[1]
    rows = 8
    tn = n // 8
    c_pad = jnp.zeros((rows, d), F32).at[:b].set(c)
    out = pl.pallas_call(
        _mod_kernel,
        grid=(n // tn,),
        in_specs=[pl.BlockSpec((rows, d), lambda j: (0, 0)),
                  pl.BlockSpec((d, tn), lambda j: (0, j)),
                  pl.BlockSpec((1, tn), lambda j: (0, j))],
        out_specs=pl.BlockSpec((rows, tn), lambda j: (0, j)),
        out_shape=jax.ShapeDtypeStruct((rows, n), F32),
        compiler_params=_cparams(1),
        name="adaln_mod",
    )(c_pad, ada_w, ada_b.reshape(1, n))
    return out[:b].reshape(b, N_MOD, d)


def _ffn_kernel(x_ref, mod_ref, g_ref, wg_ref, wu_ref, wd_ref, o_ref, *, mod_base, n_chunks):
    x = x_ref[0]
    shift = mod_ref[0, mod_base:mod_base + 1, :]
    scale = mod_ref[0, mod_base + 1:mod_base + 2, :]
    gate = mod_ref[0, mod_base + 2:mod_base + 3, :]
    h = _rms_adaln(x, g_ref[...], scale, shift).astype(BF16)
    dff = wg_ref.shape[1]
    ck = dff // n_chunks
    acc = jnp.zeros(x.shape, F32)
    for i in range(n_chunks):
        g = _dot(h, wg_ref[:, i * ck:(i + 1) * ck])
        u = _dot(h, wu_ref[:, i * ck:(i + 1) * ck])
        a = (g * jax.nn.sigmoid(g) * u).astype(BF16)
        acc = acc + _dot(a, wd_ref[i * ck:(i + 1) * ck, :])
    o_ref[0] = x + (0.5 * gate) * acc


def _ffn(x, mod, gain, wg, wu, wd, *, mod_base, tm=512):
    b, s, d = x.shape
    dff = wg.shape[1]
    kern = functools.partial(_ffn_kernel, mod_base=mod_base, n_chunks=2)
    return pl.pallas_call(
        kern,
        grid=(b, s // tm),
        in_specs=[pl.BlockSpec((1, tm, d), lambda i, j: (i, j, 0)),
                  pl.BlockSpec((1, N_MOD, d), lambda i, j: (i, 0, 0)),
                  _const_spec((1, d)),
                  _const_spec((d, dff)), _const_spec((d, dff)), _const_spec((dff, d))],
        out_specs=pl.BlockSpec((1, tm, d), lambda i, j: (i, j, 0)),
        out_shape=jax.ShapeDtypeStruct((b, s, d), F32),
        compiler_params=_cparams(2),
        name="ffn",
    )(x, mod, gain.reshape(1, d), wg, wu, wd)


def _split3_bf16(v):
    p1 = v.astype(BF16)
    r1 = v - p1.astype(F32)
    p2 = r1.astype(BF16)
    r2 = r1 - p2.astype(F32)
    return p1, p2, r2.astype(BF16)


def _proj_kernel(x_ref, mod_ref, g_ref, wm_ref, ws_ref, bsm_ref, qg_ref, kg_ref,
                 qa_ref, kat_ref, va_ref, qf_ref, kft_ref, vf_ref, qi_ref,
                 small_ref, smallt_ref, kit_ref, carry_ref):
    tm = x_ref.shape[1]
    x = x_ref[0]
    shift = mod_ref[0, 3:4, :]
    scale = mod_ref[0, 4:5, :]
    h = _rms_adaln(x, g_ref[...], scale, shift).astype(BF16)

    def z(i):
        return _dot(h, wm_ref[:, i * WIDTH:(i + 1) * WIDTH])

    r = lax.broadcasted_iota(I32, (WIDTH, WIDTH), 0) // HEAD_DIM
    c = lax.broadcasted_iota(I32, (WIDTH, WIDTH), 1) // HEAD_DIM
    avg = jnp.where(r == c, 1.0 / HEAD_DIM, 0.0).astype(BF16)

    def norm_q(q, gain_row):
        ms = _dot((q * q).astype(BF16), avg)
        return (q * lax.rsqrt(ms + EPS) * gain_row).astype(BF16)

    def norm_kt(k, gain_col):
        kt = k.T.reshape(N_HEADS, HEAD_DIM, tm)
        ms = jnp.mean(kt * kt, axis=1, keepdims=True)
        kt = kt * lax.rsqrt(ms + EPS) * gain_col[None]
        return kt.reshape(WIDTH, tm).astype(BF16)

    qa_ref[0] = norm_q(z(0), qg_ref[0:1, :])
    kat_ref[0] = norm_kt(z(1), kg_ref[:, 0:1])
    va_ref[0] = z(2).astype(BF16)
    qf_ref[0] = norm_q(z(3), qg_ref[1:2, :])
    kft_ref[0] = norm_kt(z(4), kg_ref[:, 1:2])
    vf_ref[0] = z(5).astype(BF16)
    qi_ref[0] = z(6) * (HEAD_DIM ** -0.5)

    zs = _dot(h, ws_ref[...])
    pre = zs + bsm_ref[...]
    logf = jnp.minimum(pre, 0.0) - jnp.log(1.0 + jnp.exp(-jnp.abs(pre)))

    @pl.when(pl.program_id(1) == 0)
    def _():
        carry_ref[...] = jnp.zeros_like(carry_ref)

    ri = lax.broadcasted_iota(I32, (tm, tm), 0)
    ci = lax.broadcasted_iota(I32, (tm, tm), 1)
    tri = jnp.where(ci <= ri, 1.0, 0.0).astype(BF16)
    p1, p2, p3 = _split3_bf16(logf)
    cum = (_dot(tri, p1) + _dot(tri, p2)) + _dot(tri, p3) + carry_ref[...]
    carry_ref[...] = cum[tm - 1:tm, :]

    lane = lax.broadcasted_iota(I32, (tm, LANES), 1)
    small = jnp.where(lane < SM_WI, zs,
                      jnp.where(lane < SM_F, zs * (N_HEADS ** -0.5), cum))
    small_ref[0] = small
    st = small.T
    smallt_ref[0] = st
    ki = st[0:HEAD_DIM, :]
    ki_hi = ki.astype(BF16)
    ki_lo = (ki - ki_hi.astype(F32)).astype(BF16)
    kit_ref[0] = jnp.concatenate([ki_hi, ki_hi, ki_lo], axis=0)


def _in_proj(x, mod, gain, w_main, w_small, b_small, q_gains, k_gains, *, tm=512):
    b, s, d = x.shape
    row = lambda i, j: (i, j, 0)
    col = lambda i, j: (i, 0, j)
    sd = jax.ShapeDtypeStruct
    out_shape = [sd((b, s, WIDTH), BF16), sd((b, WIDTH, s), BF16), sd((b, s, WIDTH), BF16),
                 sd((b, s, WIDTH), BF16), sd((b, WIDTH, s), BF16), sd((b, s, WIDTH), BF16),
                 sd((b, s, WIDTH), F32),
                 sd((b, s, LANES), F32), sd((b, LANES, s), F32), sd((b, 3 * HEAD_DIM, s), BF16)]
    rspec = pl.BlockSpec((1, tm, WIDTH), row)
    cspec = pl.BlockSpec((1, WIDTH, tm), col)
    out_specs = [rspec, cspec, rspec, rspec, cspec, rspec, rspec,
                 pl.BlockSpec((1, tm, LANES), row), pl.BlockSpec((1, LANES, tm), col),
                 pl.BlockSpec((1, 3 * HEAD_DIM, tm), col)]
    return pl.pallas_call(
        _proj_kernel,
        grid=(b, s // tm),
        in_specs=[pl.BlockSpec((1, tm, d), row),
                  pl.BlockSpec((1, N_MOD, d), lambda i, j: (i, 0, 0)),
                  _const_spec((1, d)),
                  _const_spec(w_main.shape), _const_spec(w_small.shape),
                  _const_spec((1, LANES)), _const_spec((2, WIDTH)), _const_spec((HEAD_DIM, 2))],
        out_specs=out_specs,
        out_shape=out_shape,
        scratch_shapes=[pltpu.VMEM((1, LANES), F32)],
        compiler_params=_cparams(2),
        name="in_proj",
    )(x, mod, gain.reshape(1, d), w_main, w_small, b_small, q_gains, k_gains)


FLASH_ROWS = 64
COUNT_ROWS = 128
LOG2E = 1.4426950408889634
SAFE_BOUND = 50.0


def _flash_scratch(tq, tk):
    return [pltpu.VMEM((N_HEADS, tq, LANES), F32), pltpu.VMEM((N_HEADS, tq, LANES), F32),
            pltpu.VMEM((N_HEADS, tq, LANES), F32), pltpu.VMEM((N_HEADS, tq, tk), BF16)]


def _flash_reset(m_ref, l_ref, acc_ref, online):
    if online:
        m_ref[...] = jnp.full(m_ref.shape, NEG, F32)
    l_ref[...] = jnp.zeros(l_ref.shape, F32)
    acc_ref[...] = jnp.zeros(acc_ref.shape, F32)


def _flash_step(q, kt_ref, v_ref, off, chunk_logits, m_ref, l_ref, acc_ref, p_ref, online=True):
    tq, tk = p_ref.shape[1:]
    n_sub = tk // LANES
    for h in range(N_HEADS):
        pair = h // 2
        q_h = q[:, h * HEAD_DIM:(h + 1) * HEAD_DIM]
        kt = kt_ref[0, h * HEAD_DIM:(h + 1) * HEAD_DIM, pl.ds(off, tk)]
        for r0 in range(0, tq, FLASH_ROWS):
            rows = pl.ds(r0, FLASH_ROWS)
            s = _dot(q_h[r0:r0 + FLASH_ROWS], kt)
            sc = [chunk_logits(h, r0, c, s[:, c * LANES:(c + 1) * LANES]) for c in range(n_sub)]
            m_row = m_ref[h, rows, :]
            if online:
                mx = sc[0]
                for x in sc[1:]:
                    mx = jnp.maximum(mx, x)
                m_old = m_row
                m_row = jnp.maximum(m_old, jnp.max(mx, axis=1, keepdims=True))
                alpha = jnp.exp2(m_old - m_row)
                m_ref[h, rows, :] = m_row
                acc_ref[h, rows, :] = alpha * acc_ref[h, rows, :]
            ps = [jnp.exp2(x - m_row) for x in sc]
            lsum = ps[0]
            for x in ps[1:]:
                lsum = lsum + x
            l_old = l_ref[h, rows, :]
            l_ref[h, rows, :] = (alpha * l_old if online else l_old) + lsum
            p_ref[h, rows, :] = jnp.concatenate(ps, axis=1).astype(BF16)
        v = v_ref[0, pl.ds(off, tk), pair * LANES:(pair + 1) * LANES]
        acc_ref[h] = acc_ref[h] + _dot(p_ref[h], v)


def _flash_finish(l_ref, acc_ref, o_ref):
    for pair in range(N_HEADS // 2):
        out = [acc_ref[h] * (1.0 / jnp.sum(l_ref[h], axis=1, keepdims=True))
               for h in (2 * pair, 2 * pair + 1)]
        lane = lax.broadcasted_iota(I32, out[0].shape, 1)
        o_ref[0, :, pair * LANES:(pair + 1) * LANES] = jnp.where(
            lane < HEAD_DIM, out[0], out[1]).astype(BF16)


def _dsa_kernel(bound_ref, qa_ref, qi_ref, small_ref, kat_ref, va_ref, kit_ref, o_ref, sc_ref,
                m_ref, l_ref, acc_ref, p_ref, *, tq, tk, topk):
    s_len = kat_ref.shape[2]
    t0 = pl.program_id(1) * tq
    nkb = (t0 + tq + tk - 1) // tk
    n_sub = tk // LANES
    row = lax.broadcasted_iota(I32, (tq, LANES), 0) + t0
    lane = lax.broadcasted_iota(I32, (COUNT_ROWS, LANES), 1)
    rep = lambda col: jnp.broadcast_to(col, (col.shape[0], LANES))

    wi = small_ref[0][:, SM_WI:SM_WI + N_HEADS]
    qi = qi_ref[0]
    lhs = []
    for h in range(N_HEADS):
        qh = qi[:, h * HEAD_DIM:(h + 1) * HEAD_DIM]
        hi = qh.astype(BF16)
        lo = (qh - hi.astype(F32)).astype(BF16)
        lhs.append(jnp.concatenate([hi, lo, hi], axis=1))
    lane_q = lax.broadcasted_iota(I32, (tq, LANES), 1)
    for h in range(N_HEADS):
        acc_ref[h] = jnp.broadcast_to(wi[:, h:h + 1], (tq, LANES))
    m_ref[0] = jnp.full((tq, LANES), jnp.inf, F32)
    m_ref[1] = jnp.full((tq, LANES), -jnp.inf, F32)
    m_ref[2] = jnp.full((tq, LANES), -jnp.inf, F32)

    def score_body(kb, _):
        off = pl.multiple_of(kb * tk, tk)
        kib = kit_ref[0, :, pl.ds(off, tk)]
        acc = [jnp.zeros((tq, LANES), F32) for _ in range(n_sub)]
        for h in range(N_HEADS):
            r = _dot(lhs[h], kib)
            w = acc_ref[h]
            for i in range(n_sub):
                acc[i] = acc[i] + jnp.maximum(r[:, i * LANES:(i + 1) * LANES], 0.0) * w
        mn, top1, top2 = m_ref[0], m_ref[1], m_ref[2]
        for i in range(n_sub):
            causal = lane_q + (off + i * LANES) <= row
            lowered = jnp.where(causal, acc[i], -jnp.inf)
            sc_ref[kb * n_sub + i] = lowered
            mn = jnp.minimum(mn, jnp.where(causal, acc[i], jnp.inf))
            top2 = jnp.maximum(top2, jnp.minimum(top1, lowered))
            top1 = jnp.maximum(top1, lowered)
        m_ref[0], m_ref[1], m_ref[2] = mn, top1, top2
        return 0

    lax.fori_loop(0, nkb, score_body, 0)
    mn, top2 = m_ref[0], m_ref[2]

    slabs = range(0, tq, COUNT_ROWS)

    def scan(step, init, *row_args, settled=None):
        outs = []
        for k, r0 in enumerate(slabs):
            args = [a[r0:r0 + COUNT_ROWS] for a in row_args]
            start = jnp.full((COUNT_ROWS, LANES), init, F32)

            def body(kb, acc, r0=r0, args=args):
                off = pl.multiple_of(kb * tk, tk)
                for i in range(n_sub):
                    blk = sc_ref[kb * n_sub + i, pl.ds(r0, COUNT_ROWS), :]
                    acc = step(acc, blk, lane + (off + i * LANES), *args)
                return acc

            run = functools.partial(lax.fori_loop, 0, nkb, body, start)
            outs.append(run() if settled is None
                        else lax.cond(settled[k] > 0.5, lambda start=start: start, run))
        return jnp.concatenate(outs, axis=0)

    def count(pred, *row_args, settled=None):
        c = scan(lambda acc, blk, pos, *a: acc + jnp.where(pred(blk, pos, *a), 1.0, 0.0),
                 0.0, *row_args, settled=settled)
        return rep(jnp.sum(c, axis=1, keepdims=True))

    def below_max(bound, settled):
        m = scan(lambda acc, blk, pos, b: jnp.maximum(acc, jnp.where(blk < b, blk, -jnp.inf)),
                 -jnp.inf, bound, settled=settled)
        return rep(jnp.max(m, axis=1, keepdims=True))

    def slab_settled(state):
        return tuple(jnp.min(state[r0:r0 + COUNT_ROWS]) for r0 in slabs)

    kf = float(topk)
    few = row < topk
    rmin = rep(jnp.min(mn, axis=1, keepdims=True))
    run_lo = rep(jnp.min(top2, axis=1, keepdims=True))
    run_hi = rep(jnp.max(top2, axis=1, keepdims=True))
    above = jnp.where(run_hi > 0.0, run_hi * (1.0 + 1e-6), run_hi * (1.0 - 1e-6)) + 1e-30
    lo0 = jnp.where(few, 0.0, jnp.maximum(run_lo, rmin))
    hi0 = jnp.where(few, 0.0, above)
    state0 = jnp.where(few, 1.0, 0.0)

    def search_cond(st):
        it, settled = st[0], st[1]
        return (it < BISECT_STEPS) & (functools.reduce(jnp.minimum, settled) < 0.5)

    def search_body(st):
        it, settled, lo, hi, state = st
        mid = 0.5 * lo + 0.5 * hi
        cnt = count(lambda blk, pos, m: blk >= m, mid, settled=settled)
        active = state == 0.0
        ge = cnt >= kf
        lo = jnp.where(active, jnp.where(ge, mid, lo), lo)
        hi = jnp.where(active, jnp.where(ge, hi, mid), hi)
        state = jnp.where(active, jnp.where(cnt == kf, 1.0, 0.0), state)
        return it + 1, slab_settled(state), lo, hi, state

    _, settled, lo, hi, state = lax.while_loop(
        search_cond, search_body, (jnp.int32(0), slab_settled(state0), lo0, hi0, state0))
    thr = jnp.where(few, -jnp.inf, lo)

    def snap_body(st):
        settled, hi, thr, state = st
        cand = below_max(hi, settled)
        cnt = count(lambda blk, pos, c: blk >= c, cand, settled=settled)
        active = state == 0.0
        found = cnt >= kf
        thr = jnp.where(active, jnp.where(found, cand, thr), thr)
        hi = jnp.where(active, jnp.where(found, hi, cand), hi)
        state = jnp.where(active, jnp.where(found, jnp.where(cnt == kf, 1.0, 2.0), 0.0), state)
        return slab_settled(state), hi, thr, state

    _, _, thr, state = lax.while_loop(
        lambda st: functools.reduce(jnp.minimum, st[0]) < 0.5, snap_body, (settled, hi, thr, state))

    last = []
    for r0 in slabs:
        rows = pl.ds(r0, COUNT_ROWS)
        thr_r, row_r, state_r = (a[r0:r0 + COUNT_ROWS] for a in (thr, row, state))
        far0 = jnp.full((COUNT_ROWS, LANES), -1, I32)

        def mark(kb, i, keep, far, rows=rows, row_r=row_r):
            pos = lane + (kb * tk + i * LANES)
            keep = keep & (pos <= row_r)
            sc_ref[kb * n_sub + i, rows, :] = jnp.where(keep, 0.0, NEG)
            return jnp.maximum(far, jnp.where(keep, pos, -1))

        def plain(rows=rows, thr_r=thr_r, mark=mark, far0=far0):
            def body(kb, far):
                for i in range(n_sub):
                    far = mark(kb, i, sc_ref[kb * n_sub + i, rows, :] >= thr_r, far)
                return far
            return lax.fori_loop(0, nkb, body, far0)

        def tied(rows=rows, thr_r=thr_r, mark=mark, far0=far0):
            def above(kb, c):
                for i in range(n_sub):
                    c = c + jnp.where(sc_ref[kb * n_sub + i, rows, :] > thr_r, 1.0, 0.0)
                return c
            c = lax.fori_loop(0, nkb, above, jnp.zeros((COUNT_ROWS, LANES), F32))
            need = kf - rep(jnp.sum(c, axis=1, keepdims=True))
            tri = jnp.where(lax.broadcasted_iota(I32, (tk, tk), 0) <= lax.broadcasted_iota(I32, (tk, tk), 1),
                            1.0, 0.0).astype(BF16)

            def body(kb, carry):
                far, seen = carry
                blks = [sc_ref[kb * n_sub + i, rows, :] for i in range(n_sub)]
                is_tie = jnp.concatenate([jnp.where(blk == thr_r, 1.0, 0.0) for blk in blks], axis=1)
                rank = _dot(is_tie.astype(BF16), tri)
                for i, blk in enumerate(blks):
                    rank_i = seen + rank[:, i * LANES:(i + 1) * LANES]
                    far = mark(kb, i, (blk > thr_r) | ((blk == thr_r) & (rank_i <= need)), far)
                return far, seen + rep(rank[:, tk - 1:tk])
            return lax.fori_loop(0, nkb, body, (far0, jnp.zeros((COUNT_ROWS, LANES), F32)))[0]

        far = lax.cond(jnp.max(state_r) > 1.5, tied, plain)
        last.append(rep(jnp.max(far.astype(F32), axis=1, keepdims=True)))
    last = jnp.concatenate(last, axis=0)

    qa = qa_ref[0]
    kpos = lax.broadcasted_iota(I32, (1, tk), 1)
    slopes = [LOG2E * 2.0 ** (-8.0 * (h + 1) / N_HEADS) for h in range(N_HEADS)]

    def attend(online):
        def attn_body(kb, _):
            off = pl.multiple_of(kb * tk, tk)
            rel = (kpos + (off - t0)).astype(F32)

            def chunk_logits(h, r0, c, x):
                mask = sc_ref[kb * n_sub + c, pl.ds(r0, FLASH_ROWS), :]
                return x + (mask + slopes[h] * rel[:, c * LANES:(c + 1) * LANES])

            _flash_step(qa, kat_ref, va_ref, off, chunk_logits, m_ref, l_ref, acc_ref, p_ref, online)
            return 0

        lax.fori_loop(0, nkb, attn_body, 0)
        _flash_finish(l_ref, acc_ref, o_ref)

    bound = bound_ref[0, 0]

    @pl.when(bound <= SAFE_BOUND)
    def _():
        _flash_reset(m_ref, l_ref, acc_ref, online=False)
        for h in range(N_HEADS):
            m_ref[h] = bound + slopes[h] * (last - t0.astype(F32))
        attend(online=False)

    @pl.when(bound > SAFE_BOUND)
    def _():
        _flash_reset(m_ref, l_ref, acc_ref, online=True)
        attend(online=True)


def _dsa_attn(bound, qa, qi, small, kat, va, kit, *, tq=512, tk=512):
    b, s, _ = qa.shape
    row = lambda i, j: (i, j, 0)
    whole = lambda i, j: (i, 0, 0)
    once = pl.Buffered(1)
    topk = min(TOPK, s // 4)
    assert topk > LANES and s % tq == 0 and s % tk == 0, (s, tq, tk)
    kern = functools.partial(_dsa_kernel, tq=tq, tk=tk, topk=topk)
    return pl.pallas_call(
        kern,
        grid=(b, s // tq),
        in_specs=[pl.BlockSpec(memory_space=pltpu.SMEM),
                  pl.BlockSpec((1, tq, WIDTH), row), pl.BlockSpec((1, tq, WIDTH), row),
                  pl.BlockSpec((1, tq, LANES), row),
                  pl.BlockSpec((1, WIDTH, s), whole, pipeline_mode=once),
                  pl.BlockSpec((1, s, WIDTH), whole, pipeline_mode=once),
                  pl.BlockSpec((1, 3 * HEAD_DIM, s), whole, pipeline_mode=once)],
        out_specs=pl.BlockSpec((1, tq, WIDTH), row),
        out_shape=jax.ShapeDtypeStruct((b, s, WIDTH), BF16),
        scratch_shapes=[pltpu.VMEM((s // LANES, tq, LANES), F32)] + _flash_scratch(tq, tk),
        compiler_params=_cparams(2),
        name="dsa_attn",
    )(bound, qa, qi, small, kat, va, kit)


def _fox_kernel(bound_ref, qf_ref, small_ref, smallt_ref, kft_ref, vf_ref, o_ref,
                m_ref, l_ref, acc_ref, p_ref, *, tq, tk):
    t0 = pl.program_id(1) * tq
    n_full = t0 // tk
    nkb = (t0 + tq + tk - 1) // tk
    row = lax.broadcasted_iota(I32, (FLASH_ROWS, LANES), 0) + t0
    col = lax.broadcasted_iota(I32, (FLASH_ROWS, LANES), 1)
    qf = qf_ref[0]
    f_t0 = smallt_ref[0, SM_F:SM_F + N_HEADS, pl.ds(pl.multiple_of(t0, LANES), LANES)][:, 0:1]

    def attend(online):
        def attn_body(kb, _, causal):
            off = pl.multiple_of(kb * tk, tk)
            decay = LOG2E * (f_t0 - smallt_ref[0, SM_F:SM_F + N_HEADS, pl.ds(off, tk)])

            def chunk_logits(h, r0, c, x):
                x = x + decay[h:h + 1, c * LANES:(c + 1) * LANES]
                if causal:
                    x = jnp.where(col + (off + c * LANES) <= row + r0, x, NEG)
                return x

            _flash_step(qf, kft_ref, vf_ref, off, chunk_logits, m_ref, l_ref, acc_ref, p_ref, online)
            return 0

        lax.fori_loop(0, n_full, functools.partial(attn_body, causal=False), 0)
        lax.fori_loop(n_full, nkb, functools.partial(attn_body, causal=True), 0)
        _flash_finish(l_ref, acc_ref, o_ref)

    bound = bound_ref[0, 0]

    @pl.when(bound <= SAFE_BOUND)
    def _():
        _flash_reset(m_ref, l_ref, acc_ref, online=False)
        f_t = small_ref[0][:, SM_F:SM_F + N_HEADS]
        for h in range(N_HEADS):
            top = bound + LOG2E * (f_t0[h:h + 1, :] - f_t[:, h:h + 1])
            m_ref[h] = jnp.broadcast_to(top, (tq, LANES))
        attend(online=False)

    @pl.when(bound > SAFE_BOUND)
    def _():
        _flash_reset(m_ref, l_ref, acc_ref, online=True)
        attend(online=True)


def _fox_attn(bound, qf, small, smallt, kft, vf, *, tq=512, tk=512):
    b, s, _ = qf.shape
    row = lambda i, j: (i, j, 0)
    whole = lambda i, j: (i, 0, 0)
    once = pl.Buffered(1)
    kern = functools.partial(_fox_kernel, tq=tq, tk=tk)
    return pl.pallas_call(
        kern,
        grid=(b, s // tq),
        in_specs=[pl.BlockSpec(memory_space=pltpu.SMEM),
                  pl.BlockSpec((1, tq, WIDTH), row), pl.BlockSpec((1, tq, LANES), row),
                  pl.BlockSpec((1, LANES, s), whole, pipeline_mode=once),
                  pl.BlockSpec((1, WIDTH, s), whole, pipeline_mode=once),
                  pl.BlockSpec((1, s, WIDTH), whole, pipeline_mode=once)],
        out_specs=pl.BlockSpec((1, tq, WIDTH), row),
        out_shape=jax.ShapeDtypeStruct((b, s, WIDTH), BF16),
        scratch_shapes=_flash_scratch(tq, tk),
        compiler_params=_cparams(2),
        name="fox_attn",
    )(bound, qf, small, smallt, kft, vf)


def _mix_kernel(x_ref, mod_ref, g_ref, ad_ref, af_ref, wgate_ref, wbd_ref, wbf_ref, wo_ref, o_ref):
    d = x_ref.shape[2]
    x = x_ref[0]
    shift = mod_ref[0, 3:4, :]
    scale = mod_ref[0, 4:5, :]
    gate = mod_ref[0, 5:6, :]
    h = _rms_adaln(x, g_ref[...], scale, shift).astype(BF16)
    ga = _dot(h, wgate_ref[:, 0:d])
    gb = _dot(h, wgate_ref[:, d:2 * d])
    y_dsa = _dot(ad_ref[0], wbd_ref[...])
    y_fox = _dot(af_ref[0], wbf_ref[...])
    merged = jax.nn.sigmoid(ga) * y_dsa + jax.nn.sigmoid(gb) * y_fox
    o_ref[0] = x + gate * _dot(merged.astype(BF16), wo_ref[...])


def _mix_out(x, mod, gain, a_dsa, a_fox, w_gate, w_br_dsa, w_br_fox, w_out, *, tm=512):
    b, s, d = x.shape
    row = lambda i, j: (i, j, 0)
    return pl.pallas_call(
        _mix_kernel,
        grid=(b, s // tm),
        in_specs=[pl.BlockSpec((1, tm, d), row),
                  pl.BlockSpec((1, N_MOD, d), lambda i, j: (i, 0, 0)),
                  _const_spec((1, d)),
                  pl.BlockSpec((1, tm, WIDTH), row), pl.BlockSpec((1, tm, WIDTH), row),
                  _const_spec(w_gate.shape), _const_spec(w_br_dsa.shape),
                  _const_spec(w_br_fox.shape), _const_spec(w_out.shape)],
        out_specs=pl.BlockSpec((1, tm, d), row),
        out_shape=jax.ShapeDtypeStruct((b, s, d), F32),
        compiler_params=_cparams(2),
        name="mix_out",
    )(x, mod, gain.reshape(1, d), a_dsa, a_fox, w_gate, w_br_dsa, w_br_fox, w_out)


def _layer(x, mod, norm1_g, ffn1_wg, ffn1_wu, ffn1_wd, norm2_g, w_in, b_forget,
           qn_dsa, kn_dsa, qn_fox, kn_fox, w_br_dsa, w_br_fox, w_out, norm3_g,
           ffn2_wg, ffn2_wu, ffn2_wd):
    d = x.shape[2]
    bf = lambda w: w.astype(BF16)
    x = _ffn(x, mod, norm1_g, bf(ffn1_wg), bf(ffn1_wu), bf(ffn1_wd), mod_base=0)

    n_main = 7 * WIDTH
    n_small = HEAD_DIM + 2 * N_HEADS
    w_main = bf(w_in[:, :n_main])
    w_small = jnp.zeros((d, LANES), BF16).at[:, :n_small].set(bf(w_in[:, n_main:n_main + n_small]))
    w_gate = bf(w_in[:, n_main + n_small:])
    b_small = jnp.zeros((1, LANES), F32).at[0, SM_F:SM_F + N_HEADS].set(b_forget)
    q_scale = LOG2E * HEAD_DIM ** -0.5
    q_gains = jnp.stack([jnp.tile(qn_dsa, N_HEADS), jnp.tile(qn_fox, N_HEADS)]) * q_scale
    k_gains = jnp.stack([kn_dsa, kn_fox], axis=1)

    qa, kat, va, qf, kft, vf, qi, small, smallt, kit = _in_proj(
        x, mod, norm2_g, w_main, w_small, b_small, q_gains, k_gains)
    def logit_bound(qn, kn):
        b = 1.05 * HEAD_DIM * q_scale * jnp.max(jnp.abs(qn)) * jnp.max(jnp.abs(kn))
        return b.reshape(1, 1).astype(F32)

    a_dsa = _dsa_attn(logit_bound(qn_dsa, kn_dsa), qa, qi, small, kat, va, kit)
    a_fox = _fox_attn(logit_bound(qn_fox, kn_fox), qf, small, smallt, kft, vf)
    x = _mix_out(x, mod, norm2_g, a_dsa, a_fox, w_gate, bf(w_br_dsa), bf(w_br_fox), bf(w_out))
    return _ffn(x, mod, norm3_g, bf(ffn2_wg), bf(ffn2_wu), bf(ffn2_wd), mod_base=6)


def kernel(x, c, ada_w, ada_b, norm1_g, ffn1_wg, ffn1_wu, ffn1_wd, norm2_g, w_in, b_forget,
           qn_dsa, kn_dsa, qn_fox, kn_fox, w_br_dsa, w_br_fox, w_out, norm3_g,
           ffn2_wg, ffn2_wu, ffn2_wd):
    per_layer = (norm1_g, ffn1_wg, ffn1_wu, ffn1_wd, norm2_g, w_in, b_forget,
                 qn_dsa, kn_dsa, qn_fox, kn_fox, w_br_dsa, w_br_fox, w_out, norm3_g,
                 ffn2_wg, ffn2_wu, ffn2_wd)
    for l in range(ada_w.shape[0]):
        mod = _adaln_mod(c, ada_w[l], ada_b[l])
        x = _layer(x, mod, *(p[l] for p in per_layer))
    return x
```

```python
import functools

import jax
import jax.numpy as jnp
from jax import lax
from jax.experimental import pallas as pl
from jax.experimental.pallas import tpu as pltpu

F32 = jnp.float32
BF16 = jnp.bfloat16
I32 = jnp.int32

HEAD_DIM = 64
N_HEADS = 8
WIDTH = N_HEADS * HEAD_DIM
TOPK = 256
EPS = 1e-6
N_MOD = 9
LANES = 128
NEG = -1e30
BISECT_STEPS = 14
VMEM_LIMIT = 56 * 1024 * 1024

SM_WI = 64
SM_F = 72


def _cparams(n_axes):
    return pltpu.CompilerParams(
        dimension_semantics=("arbitrary",) * n_axes, vmem_limit_bytes=VMEM_LIMIT)


def _const_spec(shape):
    nd = len(shape)
    return pl.BlockSpec(shape, lambda *_: (0,) * nd)


def _dot(a, b):
    return jnp.dot(a, b, preferred_element_type=F32)


def _rms_adaln(x, gain, scale, shift):
    y = x * lax.rsqrt(jnp.mean(x * x, axis=-1, keepdims=True) + EPS)
    return (y * gain) * (1.0 + scale) + shift


def _mod_kernel(c_ref, w_ref, b_ref, o_ref):
    c = c_ref[...]
    a = c * jax.nn.sigmoid(c)
    o_ref[...] = jnp.dot(a, w_ref[...], preferred_element_type=F32,
                         precision=lax.Precision.HIGHEST) + b_ref[...]


def _adaln_mod(c, ada_w, ada_b):
    b, d = c.shape
    n = ada_w.shape[1]
    rows = 8
    tn = n // 8
    c_pad = jnp.zeros((rows, d), F32).at[:b].set(c)
    out = pl.pallas_call(
        _mod_kernel,
        grid=(n // tn,),
        in_specs=[pl.BlockSpec((rows, d), lambda j: (0, 0)),
                  pl.BlockSpec((d, tn), lambda j: (0, j)),
                  pl.BlockSpec((1, tn), lambda j: (0, j))],
        out_specs=pl.BlockSpec((rows, tn), lambda j: (0, j)),
        out_shape=jax.ShapeDtypeStruct((rows, n), F32),
        compiler_params=_cparams(1),
        name="adaln_mod",
    )(c_pad, ada_w, ada_b.reshape(1, n))
    return out[:b].reshape(b, N_MOD, d)


def _ffn_kernel(x_ref, mod_ref, g_ref, wg_ref, wu_ref, wd_ref, o_ref, *, mod_base, n_chunks):
    x = x_ref[0]
    shift = mod_ref[0, mod_base:mod_base + 1, :]
    scale = mod_ref[0, mod_base + 1:mod_base + 2, :]
    gate = mod_ref[0, mod_base + 2:mod_base + 3, :]
    h = _rms_adaln(x, g_ref[...], scale, shift).astype(BF16)
    dff = wg_ref.shape[1]
    ck = dff // n_chunks
    acc = jnp.zeros(x.shape, F32)
    for i in range(n_chunks):
        g = _dot(h, wg_ref[:, i * ck:(i + 1) * ck])
        u = _dot(h, wu_ref[:, i * ck:(i + 1) * ck])
        a = (g * jax.nn.sigmoid(g) * u).astype(BF16)
        acc = acc + _dot(a, wd_ref[i * ck:(i + 1) * ck, :])
    o_ref[0] = x + (0.5 * gate) * acc


def _ffn(x, mod, gain, wg, wu, wd, *, mod_base, tm=512):
    b, s, d = x.shape
    dff = wg.shape[1]
    kern = functools.partial(_ffn_kernel, mod_base=mod_base, n_chunks=2)
    return pl.pallas_call(
        kern,
        grid=(b, s // tm),
        in_specs=[pl.BlockSpec((1, tm, d), lambda i, j: (i, j, 0)),
                  pl.BlockSpec((1, N_MOD, d), lambda i, j: (i, 0, 0)),
                  _const_spec((1, d)),
                  _const_spec((d, dff)), _const_spec((d, dff)), _const_spec((dff, d))],
        out_specs=pl.BlockSpec((1, tm, d), lambda i, j: (i, j, 0)),
        out_shape=jax.ShapeDtypeStruct((b, s, d), F32),
        compiler_params=_cparams(2),
        name="ffn",
    )(x, mod, gain.reshape(1, d), wg, wu, wd)


def _split3_bf16(v):
    p1 = v.astype(BF16)
    r1 = v - p1.astype(F32)
    p2 = r1.astype(BF16)
    r2 = r1 - p2.astype(F32)
    return p1, p2, r2.astype(BF16)


def _proj_kernel(x_ref, mod_ref, g_ref, wm_ref, ws_ref, bsm_ref, qg_ref, kg_ref,
                 qa_ref, kat_ref, va_ref, qf_ref, kft_ref, vf_ref, qi_ref,
                 small_ref, smallt_ref, kit_ref, carry_ref):
    tm = x_ref.shape[1]
    x = x_ref[0]
    shift = mod_ref[0, 3:4, :]
    scale = mod_ref[0, 4:5, :]
    h = _rms_adaln(x, g_ref[...], scale, shift).astype(BF16)

    def z(i):
        return _dot(h, wm_ref[:, i * WIDTH:(i + 1) * WIDTH])

    r = lax.broadcasted_iota(I32, (WIDTH, WIDTH), 0) // HEAD_DIM
    c = lax.broadcasted_iota(I32, (WIDTH, WIDTH), 1) // HEAD_DIM
    avg = jnp.where(r == c, 1.0 / HEAD_DIM, 0.0).astype(BF16)

    def norm_q(q, gain_row):
        ms = _dot((q * q).astype(BF16), avg)
        return (q * lax.rsqrt(ms + EPS) * gain_row).astype(BF16)

    def norm_kt(k, gain_col):
        kt = k.T.reshape(N_HEADS, HEAD_DIM, tm)
        ms = jnp.mean(kt * kt, axis=1, keepdims=True)
        kt = kt * lax.rsqrt(ms + EPS) * gain_col[None]
        return kt.reshape(WIDTH, tm).astype(BF16)

    qa_ref[0] = norm_q(z(0), qg_ref[0:1, :])
    kat_ref[0] = norm_kt(z(1), kg_ref[:, 0:1])
    va_ref[0] = z(2).astype(BF16)
    qf_ref[0] = norm_q(z(3), qg_ref[1:2, :])
    kft_ref[0] = norm_kt(z(4), kg_ref[:, 1:2])
    vf_ref[0] = z(5).astype(BF16)
    qi_ref[0] = z(6) * (HEAD_DIM ** -0.5)

    zs = _dot(h, ws_ref[...])
    pre = zs + bsm_ref[...]
    logf = jnp.minimum(pre, 0.0) - jnp.log(1.0 + jnp.exp(-jnp.abs(pre)))

    @pl.when(pl.program_id(1) == 0)
    def _():
        carry_ref[...] = jnp.zeros_like(carry_ref)

    ri = lax.broadcasted_iota(I32, (tm, tm), 0)
    ci = lax.broadcasted_iota(I32, (tm, tm), 1)
    tri = jnp.where(ci <= ri, 1.0, 0.0).astype(BF16)
    p1, p2, p3 = _split3_bf16(logf)
    cum = (_dot(tri, p1) + _dot(tri, p2)) + _dot(tri, p3) + carry_ref[...]
    carry_ref[...] = cum[tm - 1:tm, :]

    lane = lax.broadcasted_iota(I32, (tm, LANES), 1)
    small = jnp.where(lane < SM_WI, zs,
                      jnp.where(lane < SM_F, zs * (N_HEADS ** -0.5), cum))
    small_ref[0] = small
    st = small.T
    smallt_ref[0] = st
    ki = st[0:HEAD_DIM, :]
    ki_hi = ki.astype(BF16)
    ki_lo = (ki - ki_hi.astype(F32)).astype(BF16)
    kit_ref[0] = jnp.concatenate([ki_hi, ki_hi, ki_lo], axis=0)


def _in_proj(x, mod, gain, w_main, w_small, b_small, q_gains, k_gains, *, tm=512):
    b, s, d = x.shape
    row = lambda i, j: (i, j, 0)
    col = lambda i, j: (i, 0, j)
    sd = jax.ShapeDtypeStruct
    out_shape = [sd((b, s, WIDTH), BF16), sd((b, WIDTH, s), BF16), sd((b, s, WIDTH), BF16),
                 sd((b, s, WIDTH), BF16), sd((b, WIDTH, s), BF16), sd((b, s, WIDTH), BF16),
                 sd((b, s, WIDTH), F32),
                 sd((b, s, LANES), F32), sd((b, LANES, s), F32), sd((b, 3 * HEAD_DIM, s), BF16)]
    rspec = pl.BlockSpec((1, tm, WIDTH), row)
    cspec = pl.BlockSpec((1, WIDTH, tm), col)
    out_specs = [rspec, cspec, rspec, rspec, cspec, rspec, rspec,
                 pl.BlockSpec((1, tm, LANES), row), pl.BlockSpec((1, LANES, tm), col),
                 pl.BlockSpec((1, 3 * HEAD_DIM, tm), col)]
    return pl.pallas_call(
        _proj_kernel,
        grid=(b, s // tm),
        in_specs=[pl.BlockSpec((1, tm, d), row),
                  pl.BlockSpec((1, N_MOD, d), lambda i, j: (i, 0, 0)),
                  _const_spec((1, d)),
                  _const_spec(w_main.shape), _const_spec(w_small.shape),
                  _const_spec((1, LANES)), _const_spec((2, WIDTH)), _const_spec((HEAD_DIM, 2))],
        out_specs=out_specs,
        out_shape=out_shape,
        scratch_shapes=[pltpu.VMEM((1, LANES), F32)],
        compiler_params=_cparams(2),
        name="in_proj",
    )(x, mod, gain.reshape(1, d), w_main, w_small, b_small, q_gains, k_gains)


FLASH_ROWS = 64
COUNT_ROWS = 128
LOG2E = 1.4426950408889634
SAFE_BOUND = 50.0


def _flash_scratch(tq, tk):
    return [pltpu.VMEM((N_HEADS, tq, LANES), F32), pltpu.VMEM((N_HEADS, tq, LANES), F32),
            pltpu.VMEM((N_HEADS, tq, LANES), F32), pltpu.VMEM((N_HEADS, tq, tk), BF16)]


def _flash_reset(m_ref, l_ref, acc_ref, online):
    if online:
        m_ref[...] = jnp.full(m_ref.shape, NEG, F32)
    l_ref[...] = jnp.zeros(l_ref.shape, F32)
    acc_ref[...] = jnp.zeros(acc_ref.shape, F32)


def _flash_step(q, kt_ref, v_ref, off, chunk_logits, m_ref, l_ref, acc_ref, p_ref, online=True):
    tq, tk = p_ref.shape[1:]
    n_sub = tk // LANES
    for h in range(N_HEADS):
        pair = h // 2
        q_h = q[:, h * HEAD_DIM:(h + 1) * HEAD_DIM]
        kt = kt_ref[0, h * HEAD_DIM:(h + 1) * HEAD_DIM, pl.ds(off, tk)]
        for r0 in range(0, tq, FLASH_ROWS):
            rows = pl.ds(r0, FLASH_ROWS)
            s = _dot(q_h[r0:r0 + FLASH_ROWS], kt)
            sc = [chunk_logits(h, r0, c, s[:, c * LANES:(c + 1) * LANES]) for c in range(n_sub)]
            m_row = m_ref[h, rows, :]
            if online:
                mx = sc[0]
                for x in sc[1:]:
                    mx = jnp.maximum(mx, x)
                m_old = m_row
                m_row = jnp.maximum(m_old, jnp.max(mx, axis=1, keepdims=True))
                alpha = jnp.exp2(m_old - m_row)
                m_ref[h, rows, :] = m_row
                acc_ref[h, rows, :] = alpha * acc_ref[h, rows, :]
            ps = [jnp.exp2(x - m_row) for x in sc]
            lsum = ps[0]
            for x in ps[1:]:
                lsum = lsum + x
            l_old = l_ref[h, rows, :]
            l_ref[h, rows, :] = (alpha * l_old if online else l_old) + lsum
            p_ref[h, rows, :] = jnp.concatenate(ps, axis=1).astype(BF16)
        v = v_ref[0, pl.ds(off, tk), pair * LANES:(pair + 1) * LANES]
        acc_ref[h] = acc_ref[h] + _dot(p_ref[h], v)


def _flash_finish(l_ref, acc_ref, o_ref):
    for pair in range(N_HEADS // 2):
        out = [acc_ref[h] * (1.0 / jnp.sum(l_ref[h], axis=1, keepdims=True))
               for h in (2 * pair, 2 * pair + 1)]
        lane = lax.broadcasted_iota(I32, out[0].shape, 1)
        o_ref[0, :, pair * LANES:(pair + 1) * LANES] = jnp.where(
            lane < HEAD_DIM, out[0], out[1]).astype(BF16)


def _dsa_kernel(bound_ref, qa_ref, qi_ref, small_ref, kat_ref, va_ref, kit_ref, o_ref, sc_ref,
                m_ref, l_ref, acc_ref, p_ref, *, tq, tk, topk):
    s_len = kat_ref.shape[2]
    t0 = pl.program_id(1) * tq
    nkb = (t0 + tq + tk - 1) // tk
    n_sub = tk // LANES
    row = lax.broadcasted_iota(I32, (tq, LANES), 0) + t0
    lane = lax.broadcasted_iota(I32, (COUNT_ROWS, LANES), 1)
    rep = lambda col: jnp.broadcast_to(col, (col.shape[0], LANES))

    wi = small_ref[0][:, SM_WI:SM_WI + N_HEADS]
    qi = qi_ref[0]
    lhs = []
    for h in range(N_HEADS):
        qh = qi[:, h * HEAD_DIM:(h + 1) * HEAD_DIM]
        hi = qh.astype(BF16)
        lo = (qh - hi.astype(F32)).astype(BF16)
        lhs.append(jnp.concatenate([hi, lo, hi], axis=1))
    lane_q = lax.broadcasted_iota(I32, (tq, LANES), 1)
    for h in range(N_HEADS):
        acc_ref[h] = jnp.broadcast_to(wi[:, h:h + 1], (tq, LANES))
    m_ref[0] = jnp.full((tq, LANES), jnp.inf, F32)
    m_ref[1] = jnp.full((tq, LANES), -jnp.inf, F32)
    m_ref[2] = jnp.full((tq, LANES), -jnp.inf, F32)

    def score_body(kb, _):
        off = pl.multiple_of(kb * tk, tk)
        kib = kit_ref[0, :, pl.ds(off, tk)]
        acc = [jnp.zeros((tq, LANES), F32) for _ in range(n_sub)]
        for h in range(N_HEADS):
            r = _dot(lhs[h], kib)
            w = acc_ref[h]
            for i in range(n_sub):
                acc[i] = acc[i] + jnp.maximum(r[:, i * LANES:(i + 1) * LANES], 0.0) * w
        mn, top1, top2 = m_ref[0], m_ref[1], m_ref[2]
        for i in range(n_sub):
            causal = lane_q + (off + i * LANES) <= row
            lowered = jnp.where(causal, acc[i], -jnp.inf)
            sc_ref[kb * n_sub + i] = lowered
            mn = jnp.minimum(mn, jnp.where(causal, acc[i], jnp.inf))
            top2 = jnp.maximum(top2, jnp.minimum(top1, lowered))
            top1 = jnp.maximum(top1, lowered)
        m_ref[0], m_ref[1], m_ref[2] = mn, top1, top2
        return 0

    lax.fori_loop(0, nkb, score_body, 0)
    mn, top2 = m_ref[0], m_ref[2]

    slabs = range(0, tq, COUNT_ROWS)

    def scan(step, init, *row_args, settled=None):
        outs = []
        for k, r0 in enumerate(slabs):
            args = [a[r0:r0 + COUNT_ROWS] for a in row_args]
            start = jnp.full((COUNT_ROWS, LANES), init, F32)

            def body(kb, acc, r0=r0, args=args):
                off = pl.multiple_of(kb * tk, tk)
                for i in range(n_sub):
                    blk = sc_ref[kb * n_sub + i, pl.ds(r0, COUNT_ROWS), :]
                    acc = step(acc, blk, lane + (off + i * LANES), *args)
                return acc

            run = functools.partial(lax.fori_loop, 0, nkb, body, start)
            outs.append(run() if settled is None
                        else lax.cond(settled[k] > 0.5, lambda start=start: start, run))
        return jnp.concatenate(outs, axis=0)

    def count(pred, *row_args, settled=None):
        c = scan(lambda acc, blk, pos, *a: acc + jnp.where(pred(blk, pos, *a), 1.0, 0.0),
                 0.0, *row_args, settled=settled)
        return rep(jnp.sum(c, axis=1, keepdims=True))

    def below_max(bound, settled):
        m = scan(lambda acc, blk, pos, b: jnp.maximum(acc, jnp.where(blk < b, blk, -jnp.inf)),
                 -jnp.inf, bound, settled=settled)
        return rep(jnp.max(m, axis=1, keepdims=True))

    def slab_settled(state):
        return tuple(jnp.min(state[r0:r0 + COUNT_ROWS]) for r0 in slabs)

    kf = float(topk)
    few = row < topk
    rmin = rep(jnp.min(mn, axis=1, keepdims=True))
    run_lo = rep(jnp.min(top2, axis=1, keepdims=True))
    run_hi = rep(jnp.max(top2, axis=1, keepdims=True))
    above = jnp.where(run_hi > 0.0, run_hi * (1.0 + 1e-6), run_hi * (1.0 - 1e-6)) + 1e-30
    lo0 = jnp.where(few, 0.0, jnp.maximum(run_lo, rmin))
    hi0 = jnp.where(few, 0.0, above)
    state0 = jnp.where(few, 1.0, 0.0)

    def search_body(_, st):
        lo, hi, state = st
        mid = 0.5 * lo + 0.5 * hi
        cnt = count(lambda blk, pos, m: blk >= m, mid)
        active = state == 0.0
        ge = cnt >= kf
        lo = jnp.where(active, jnp.where(ge, mid, lo), lo)
        hi = jnp.where(active, jnp.where(ge, hi, mid), hi)
        state = jnp.where(active, jnp.where(cnt == kf, 1.0, 0.0), state)
        return lo, hi, state

    lo, hi, state = lax.fori_loop(0, BISECT_STEPS, search_body, (lo0, hi0, state0))
    settled = slab_settled(state)
    thr = jnp.where(few, -jnp.inf, lo)

    def snap_body(st):
        settled, hi, thr, state = st
        cand = below_max(hi, settled)
        cnt = count(lambda blk, pos, c: blk >= c, cand, settled=settled)
        active = state == 0.0
        found = cnt >= kf
        thr = jnp.where(active, jnp.where(found, cand, thr), thr)
        hi = jnp.where(active, jnp.where(found, hi, cand), hi)
        state = jnp.where(active, jnp.where(found, jnp.where(cnt == kf, 1.0, 2.0), 0.0), state)
        return slab_settled(state), hi, thr, state

    _, _, thr, state = lax.while_loop(
        lambda st: functools.reduce(jnp.minimum, st[0]) < 0.5, snap_body, (settled, hi, thr, state))

    last = []
    for r0 in slabs:
        rows = pl.ds(r0, COUNT_ROWS)
        thr_r, row_r, state_r = (a[r0:r0 + COUNT_ROWS] for a in (thr, row, state))
        far0 = jnp.full((COUNT_ROWS, LANES), -1, I32)

        def mark(kb, i, keep, far, rows=rows, row_r=row_r):
            pos = lane + (kb * tk + i * LANES)
            keep = keep & (pos <= row_r)
            sc_ref[kb * n_sub + i, rows, :] = jnp.where(keep, 0.0, NEG)
            return jnp.maximum(far, jnp.where(keep, pos, -1))

        def plain(rows=rows, thr_r=thr_r, mark=mark, far0=far0):
            def body(kb, far):
                for i in range(n_sub):
                    far = mark(kb, i, sc_ref[kb * n_sub + i, rows, :] >= thr_r, far)
                return far
            return lax.fori_loop(0, nkb, body, far0)

        def tied(rows=rows, thr_r=thr_r, mark=mark, far0=far0):
            def above(kb, c):
                for i in range(n_sub):
                    c = c + jnp.where(sc_ref[kb * n_sub + i, rows, :] > thr_r, 1.0, 0.0)
                return c
            c = lax.fori_loop(0, nkb, above, jnp.zeros((COUNT_ROWS, LANES), F32))
            need = kf - rep(jnp.sum(c, axis=1, keepdims=True))
            tri = jnp.where(lax.broadcasted_iota(I32, (tk, tk), 0) <= lax.broadcasted_iota(I32, (tk, tk), 1),
                            1.0, 0.0).astype(BF16)

            def body(kb, carry):
                far, seen = carry
                blks = [sc_ref[kb * n_sub + i, rows, :] for i in range(n_sub)]
                is_tie = jnp.concatenate([jnp.where(blk == thr_r, 1.0, 0.0) for blk in blks], axis=1)
                rank = _dot(is_tie.astype(BF16), tri)
                for i, blk in enumerate(blks):
                    rank_i = seen + rank[:, i * LANES:(i + 1) * LANES]
                    far = mark(kb, i, (blk > thr_r) | ((blk == thr_r) & (rank_i <= need)), far)
                return far, seen + rep(rank[:, tk - 1:tk])
            return lax.fori_loop(0, nkb, body, (far0, jnp.zeros((COUNT_ROWS, LANES), F32)))[0]

        far = lax.cond(jnp.max(state_r) > 1.5, tied, plain)
        last.append(rep(jnp.max(far.astype(F32), axis=1, keepdims=True)))
    last = jnp.concatenate(last, axis=0)

    qa = qa_ref[0]
    kpos = lax.broadcasted_iota(I32, (1, tk), 1)
    slopes = [LOG2E * 2.0 ** (-8.0 * (h + 1) / N_HEADS) for h in range(N_HEADS)]

    def attend(online):
        def attn_body(kb, _):
            off = pl.multiple_of(kb * tk, tk)
            rel = (kpos + (off - t0)).astype(F32)

            def chunk_logits(h, r0, c, x):
                mask = sc_ref[kb * n_sub + c, pl.ds(r0, FLASH_ROWS), :]
                return x + (mask + slopes[h] * rel[:, c * LANES:(c + 1) * LANES])

            _flash_step(qa, kat_ref, va_ref, off, chunk_logits, m_ref, l_ref, acc_ref, p_ref, online)
            return 0

        lax.fori_loop(0, nkb, attn_body, 0)
        _flash_finish(l_ref, acc_ref, o_ref)

    bound = bound_ref[0, 0]

    @pl.when(bound <= SAFE_BOUND)
    def _():
        _flash_reset(m_ref, l_ref, acc_ref, online=False)
        for h in range(N_HEADS):
            m_ref[h] = bound + slopes[h] * (last - t0.astype(F32))
        attend(online=False)

    @pl.when(bound > SAFE_BOUND)
    def _():
        _flash_reset(m_ref, l_ref, acc_ref, online=True)
        attend(online=True)


def _dsa_attn(bound, qa, qi, small, kat, va, kit, *, tq=512, tk=512):
    b, s, _ = qa.shape
    row = lambda i, j: (i, j, 0)
    whole = lambda i, j: (i, 0, 0)
    once = pl.Buffered(1)
    topk = min(TOPK, s // 4)
    assert topk > LANES and s % tq == 0 and s % tk == 0, (s, tq, tk)
    kern = functools.partial(_dsa_kernel, tq=tq, tk=tk, topk=topk)
    return pl.pallas_call(
        kern,
        grid=(b, s // tq),
        in_specs=[pl.BlockSpec(memory_space=pltpu.SMEM),
                  pl.BlockSpec((1, tq, WIDTH), row), pl.BlockSpec((1, tq, WIDTH), row),
                  pl.BlockSpec((1, tq, LANES), row),
                  pl.BlockSpec((1, WIDTH, s), whole, pipeline_mode=once),
                  pl.BlockSpec((1, s, WIDTH), whole, pipeline_mode=once),
                  pl.BlockSpec((1, 3 * HEAD_DIM, s), whole, pipeline_mode=once)],
        out_specs=pl.BlockSpec((1, tq, WIDTH), row),
        out_shape=jax.ShapeDtypeStruct((b, s, WIDTH), BF16),
        scratch_shapes=[pltpu.VMEM((s // LANES, tq, LANES), F32)] + _flash_scratch(tq, tk),
        compiler_params=_cparams(2),
        name="dsa_attn",
    )(bound, qa, qi, small, kat, va, kit)


def _fox_kernel(bound_ref, qf_ref, small_ref, smallt_ref, kft_ref, vf_ref, o_ref,
                m_ref, l_ref, acc_ref, p_ref, *, tq, tk):
    t0 = pl.program_id(1) * tq
    n_full = t0 // tk
    nkb = (t0 + tq + tk - 1) // tk
    row = lax.broadcasted_iota(I32, (FLASH_ROWS, LANES), 0) + t0
    col = lax.broadcasted_iota(I32, (FLASH_ROWS, LANES), 1)
    qf = qf_ref[0]
    f_t0 = smallt_ref[0, SM_F:SM_F + N_HEADS, pl.ds(pl.multiple_of(t0, LANES), LANES)][:, 0:1]

    def attend(online):
        def attn_body(kb, _, causal):
            off = pl.multiple_of(kb * tk, tk)
            decay = LOG2E * (f_t0 - smallt_ref[0, SM_F:SM_F + N_HEADS, pl.ds(off, tk)])

            def chunk_logits(h, r0, c, x):
                x = x + decay[h:h + 1, c * LANES:(c + 1) * LANES]
                if causal:
                    x = jnp.where(col + (off + c * LANES) <= row + r0, x, NEG)
                return x

            _flash_step(qf, kft_ref, vf_ref, off, chunk_logits, m_ref, l_ref, acc_ref, p_ref, online)
            return 0

        lax.fori_loop(0, n_full, functools.partial(attn_body, causal=False), 0)
        lax.fori_loop(n_full, nkb, functools.partial(attn_body, causal=True), 0)
        _flash_finish(l_ref, acc_ref, o_ref)

    bound = bound_ref[0, 0]

    @pl.when(bound <= SAFE_BOUND)
    def _():
        _flash_reset(m_ref, l_ref, acc_ref, online=False)
        f_t = small_ref[0][:, SM_F:SM_F + N_HEADS]
        for h in range(N_HEADS):
            top = bound + LOG2E * (f_t0[h:h + 1, :] - f_t[:, h:h + 1])
            m_ref[h] = jnp.broadcast_to(top, (tq, LANES))
        attend(online=False)

    @pl.when(bound > SAFE_BOUND)
    def _():
        _flash_reset(m_ref, l_ref, acc_ref, online=True)
        attend(online=True)


def _fox_attn(bound, qf, small, smallt, kft, vf, *, tq=512, tk=512):
    b, s, _ = qf.shape
    row = lambda i, j: (i, j, 0)
    whole = lambda i, j: (i, 0, 0)
    once = pl.Buffered(1)
    kern = functools.partial(_fox_kernel, tq=tq, tk=tk)
    return pl.pallas_call(
        kern,
        grid=(b, s // tq),
        in_specs=[pl.BlockSpec(memory_space=pltpu.SMEM),
                  pl.BlockSpec((1, tq, WIDTH), row), pl.BlockSpec((1, tq, LANES), row),
                  pl.BlockSpec((1, LANES, s), whole, pipeline_mode=once),
                  pl.BlockSpec((1, WIDTH, s), whole, pipeline_mode=once),
                  pl.BlockSpec((1, s, WIDTH), whole, pipeline_mode=once)],
        out_specs=pl.BlockSpec((1, tq, WIDTH), row),
        out_shape=jax.ShapeDtypeStruct((b, s, WIDTH), BF16),
        scratch_shapes=_flash_scratch(tq, tk),
        compiler_params=_cparams(2),
        name="fox_attn",
    )(bound, qf, small, smallt, kft, vf)


def _mix_kernel(x_ref, mod_ref, g_ref, ad_ref, af_ref, wgate_ref, wbd_ref, wbf_ref, wo_ref, o_ref):
    d = x_ref.shape[2]
    x = x_ref[0]
    shift = mod_ref[0, 3:4, :]
    scale = mod_ref[0, 4:5, :]
    gate = mod_ref[0, 5:6, :]
    h = _rms_adaln(x, g_ref[...], scale, shift).astype(BF16)
    ga = _dot(h, wgate_ref[:, 0:d])
    gb = _dot(h, wgate_ref[:, d:2 * d])
    y_dsa = _dot(ad_ref[0], wbd_ref[...])
    y_fox = _dot(af_ref[0], wbf_ref[...])
    merged = jax.nn.sigmoid(ga) * y_dsa + jax.nn.sigmoid(gb) * y_fox
    o_ref[0] = x + gate * _dot(merged.astype(BF16), wo_ref[...])


def _mix_out(x, mod, gain, a_dsa, a_fox, w_gate, w_br_dsa, w_br_fox, w_out, *, tm=512):
    b, s, d = x.shape
    row = lambda i, j: (i, j, 0)
    return pl.pallas_call(
        _mix_kernel,
        grid=(b, s // tm),
        in_specs=[pl.BlockSpec((1, tm, d), row),
                  pl.BlockSpec((1, N_MOD, d), lambda i, j: (i, 0, 0)),
                  _const_spec((1, d)),
                  pl.BlockSpec((1, tm, WIDTH), row), pl.BlockSpec((1, tm, WIDTH), row),
                  _const_spec(w_gate.shape), _const_spec(w_br_dsa.shape),
                  _const_spec(w_br_fox.shape), _const_spec(w_out.shape)],
        out_specs=pl.BlockSpec((1, tm, d), row),
        out_shape=jax.ShapeDtypeStruct((b, s, d), F32),
        compiler_params=_cparams(2),
        name="mix_out",
    )(x, mod, gain.reshape(1, d), a_dsa, a_fox, w_gate, w_br_dsa, w_br_fox, w_out)


def _layer(x, mod, norm1_g, ffn1_wg, ffn1_wu, ffn1_wd, norm2_g, w_in, b_forget,
           qn_dsa, kn_dsa, qn_fox, kn_fox, w_br_dsa, w_br_fox, w_out, norm3_g,
           ffn2_wg, ffn2_wu, ffn2_wd):
    d = x.shape[2]
    bf = lambda w: w.astype(BF16)
    x = _ffn(x, mod, norm1_g, bf(ffn1_wg), bf(ffn1_wu), bf(ffn1_wd), mod_base=0)

    n_main = 7 * WIDTH
    n_small = HEAD_DIM + 2 * N_HEADS
    w_main = bf(w_in[:, :n_main])
    w_small = jnp.zeros((d, LANES), BF16).at[:, :n_small].set(bf(w_in[:, n_main:n_main + n_small]))
    w_gate = bf(w_in[:, n_main + n_small:])
    b_small = jnp.zeros((1, LANES), F32).at[0, SM_F:SM_F + N_HEADS].set(b_forget)
    q_scale = LOG2E * HEAD_DIM ** -0.5
    q_gains = jnp.stack([jnp.tile(qn_dsa, N_HEADS), jnp.tile(qn_fox, N_HEADS)]) * q_scale
    k_gains = jnp.stack([kn_dsa, kn_fox], axis=1)

    qa, kat, va, qf, kft, vf, qi, small, smallt, kit = _in_proj(
        x, mod, norm2_g, w_main, w_small, b_small, q_gains, k_gains)
    def logit_bound(qn, kn):
        b = 1.05 * HEAD_DIM * q_scale * jnp.max(jnp.abs(qn)) * jnp.max(jnp.abs(kn))
        return b.reshape(1, 1).astype(F32)

    a_dsa = _dsa_attn(logit_bound(qn_dsa, kn_dsa), qa, qi, small, kat, va, kit)
    a_fox = _fox_attn(logit_bound(qn_fox, kn_fox), qf, small, smallt, kft, vf)
    x = _mix_out(x, mod, norm2_g, a_dsa, a_fox, w_gate, bf(w_br_dsa), bf(w_br_fox), bf(w_out))
    return _ffn(x, mod, norm3_g, bf(ffn2_wg), bf(ffn2_wu), bf(ffn2_wd), mod_base=6)


def kernel(x, c, ada_w, ada_b, norm1_g, ffn1_wg, ffn1_wu, ffn1_wd, norm2_g, w_in, b_forget,
           qn_dsa, kn_dsa, qn_fox, kn_fox, w_br_dsa, w_br_fox, w_out, norm3_g,
           ffn2_wg, ffn2_wu, ffn2_wd):
    per_layer = (norm1_g, ffn1_wg, ffn1_wu, ffn1_wd, norm2_g, w_in, b_forget,
                 qn_dsa, kn_dsa, qn_fox, kn_fox, w_br_dsa, w_br_fox, w_out, norm3_g,
                 ffn2_wg, ffn2_wu, ffn2_wd)
    for l in range(ada_w.shape[0]):
        mod = _adaln_mod(c, ada_w[l], ada_b[l])
        x = _layer(x, mod, *(p[l] for p in per_layer))
    return x
```

```python
import functools

import jax
import jax.numpy as jnp
from jax import lax
from jax.experimental import pallas as pl
from jax.experimental.pallas import tpu as pltpu

F32 = jnp.float32
BF16 = jnp.bfloat16
I32 = jnp.int32

HEAD_DIM = 64
N_HEADS = 8
WIDTH = N_HEADS * HEAD_DIM
TOPK = 256
EPS = 1e-6
N_MOD = 9
LANES = 128
NEG = -1e30
BISECT_STEPS = 14
VMEM_LIMIT = 56 * 1024 * 1024

SM_WI = 64
SM_F = 72


def _cparams(n_axes):
    return pltpu.CompilerParams(
        dimension_semantics=("arbitrary",) * n_axes, vmem_limit_bytes=VMEM_LIMIT)


def _const_spec(shape):
    nd = len(shape)
    return pl.BlockSpec(shape, lambda *_: (0,) * nd)


def _dot(a, b):
    return jnp.dot(a, b, preferred_element_type=F32)


def _rms_adaln(x, gain, scale, shift):
    y = x * lax.rsqrt(jnp.mean(x * x, axis=-1, keepdims=True) + EPS)
    return (y * gain) * (1.0 + scale) + shift


def _mod_kernel(c_ref, w_ref, b_ref, o_ref):
    c = c_ref[...]
    a = c * jax.nn.sigmoid(c)
    o_ref[...] = jnp.dot(a, w_ref[0], preferred_element_type=F32,
                         precision=lax.Precision.HIGHEST) + b_ref[0]


def _adaln_mod(c, ada_w, ada_b, layer):
    b, d = c.shape
    n = ada_w.shape[2]
    rows = 8
    tn = n // 8
    c_pad = jnp.zeros((rows, d), F32).at[:b].set(c)
    out = pl.pallas_call(
        _mod_kernel,
        grid=(n // tn,),
        in_specs=[pl.BlockSpec((rows, d), lambda j: (0, 0)),
                  pl.BlockSpec((1, d, tn), lambda j: (layer, 0, j)),
                  pl.BlockSpec((1, 1, tn), lambda j: (layer, 0, j))],
        out_specs=pl.BlockSpec((rows, tn), lambda j: (0, j)),
        out_shape=jax.ShapeDtypeStruct((rows, n), F32),
        compiler_params=_cparams(1),
        name="adaln_mod",
    )(c_pad, ada_w, ada_b.reshape(ada_b.shape[0], 1, n))
    return out[:b].reshape(b, N_MOD, d)


def _ffn_kernel(x_ref, mod_ref, g_ref, wg_ref, wu_ref, wd_ref, o_ref, *, mod_base, n_chunks):
    x = x_ref[0]
    shift = mod_ref[0, mod_base:mod_base + 1, :]
    scale = mod_ref[0, mod_base + 1:mod_base + 2, :]
    gate = mod_ref[0, mod_base + 2:mod_base + 3, :]
    h = _rms_adaln(x, g_ref[...], scale, shift).astype(BF16)
    dff = wg_ref.shape[1]
    ck = dff // n_chunks
    acc = jnp.zeros(x.shape, F32)
    for i in range(n_chunks):
        g = _dot(h, wg_ref[:, i * ck:(i + 1) * ck])
        u = _dot(h, wu_ref[:, i * ck:(i + 1) * ck])
        a = (g * jax.nn.sigmoid(g) * u).astype(BF16)
        acc = acc + _dot(a, wd_ref[i * ck:(i + 1) * ck, :])
    o_ref[0] = x + (0.5 * gate) * acc


def _ffn(x, mod, gain, wg, wu, wd, *, mod_base, tm=512):
    b, s, d = x.shape
    dff = wg.shape[1]
    kern = functools.partial(_ffn_kernel, mod_base=mod_base, n_chunks=2)
    return pl.pallas_call(
        kern,
        grid=(b, s // tm),
        in_specs=[pl.BlockSpec((1, tm, d), lambda i, j: (i, j, 0)),
                  pl.BlockSpec((1, N_MOD, d), lambda i, j: (i, 0, 0)),
                  _const_spec((1, d)),
                  _const_spec((d, dff)), _const_spec((d, dff)), _const_spec((dff, d))],
        out_specs=pl.BlockSpec((1, tm, d), lambda i, j: (i, j, 0)),
        out_shape=jax.ShapeDtypeStruct((b, s, d), F32),
        compiler_params=_cparams(2),
        name="ffn",
    )(x, mod, gain.reshape(1, d), wg, wu, wd)


def _split3_bf16(v):
    p1 = v.astype(BF16)
    r1 = v - p1.astype(F32)
    p2 = r1.astype(BF16)
    r2 = r1 - p2.astype(F32)
    return p1, p2, r2.astype(BF16)


def _proj_kernel(x_ref, mod_ref, g_ref, wm_ref, ws_ref, bsm_ref, qg_ref, kg_ref,
                 qa_ref, kat_ref, va_ref, qf_ref, kft_ref, vf_ref, qi_ref,
                 small_ref, smallt_ref, kit_ref, carry_ref):
    tm = x_ref.shape[1]
    x = x_ref[0]
    shift = mod_ref[0, 3:4, :]
    scale = mod_ref[0, 4:5, :]
    h = _rms_adaln(x, g_ref[...], scale, shift).astype(BF16)

    def z(i):
        return _dot(h, wm_ref[:, i * WIDTH:(i + 1) * WIDTH])

    r = lax.broadcasted_iota(I32, (WIDTH, WIDTH), 0) // HEAD_DIM
    c = lax.broadcasted_iota(I32, (WIDTH, WIDTH), 1) // HEAD_DIM
    avg = jnp.where(r == c, 1.0 / HEAD_DIM, 0.0).astype(BF16)

    def norm_q(q, gain_row):
        ms = _dot((q * q).astype(BF16), avg)
        return (q * lax.rsqrt(ms + EPS) * gain_row).astype(BF16)

    def norm_kt(k, gain_col):
        kt = k.T.reshape(N_HEADS, HEAD_DIM, tm)
        ms = jnp.mean(kt * kt, axis=1, keepdims=True)
        kt = kt * lax.rsqrt(ms + EPS) * gain_col[None]
        return kt.reshape(WIDTH, tm).astype(BF16)

    qa_ref[0] = norm_q(z(0), qg_ref[0:1, :])
    kat_ref[0] = norm_kt(z(1), kg_ref[:, 0:1])
    va_ref[0] = z(2).astype(BF16)
    qf_ref[0] = norm_q(z(3), qg_ref[1:2, :])
    kft_ref[0] = norm_kt(z(4), kg_ref[:, 1:2])
    vf_ref[0] = z(5).astype(BF16)
    qi_ref[0] = z(6) * (HEAD_DIM ** -0.5)

    zs = _dot(h, ws_ref[...])
    pre = zs + bsm_ref[...]
    logf = jnp.minimum(pre, 0.0) - jnp.log(1.0 + jnp.exp(-jnp.abs(pre)))

    @pl.when(pl.program_id(1) == 0)
    def _():
        carry_ref[...] = jnp.zeros_like(carry_ref)

    ri = lax.broadcasted_iota(I32, (tm, tm), 0)
    ci = lax.broadcasted_iota(I32, (tm, tm), 1)
    tri = jnp.where(ci <= ri, 1.0, 0.0).astype(BF16)
    p1, p2, p3 = _split3_bf16(logf)
    cum = (_dot(tri, p1) + _dot(tri, p2)) + _dot(tri, p3) + carry_ref[...]
    carry_ref[...] = cum[tm - 1:tm, :]

    lane = lax.broadcasted_iota(I32, (tm, LANES), 1)
    small = jnp.where(lane < SM_WI, zs,
                      jnp.where(lane < SM_F, zs * (N_HEADS ** -0.5), cum))
    small_ref[0] = small
    st = small.T
    smallt_ref[0] = st
    ki = st[0:HEAD_DIM, :]
    ki_hi = ki.astype(BF16)
    ki_lo = (ki - ki_hi.astype(F32)).astype(BF16)
    kit_ref[0] = jnp.concatenate([ki_hi, ki_hi, ki_lo], axis=0)


def _in_proj(x, mod, gain, w_main, w_small, b_small, q_gains, k_gains, *, tm=512):
    b, s, d = x.shape
    row = lambda i, j: (i, j, 0)
    col = lambda i, j: (i, 0, j)
    sd = jax.ShapeDtypeStruct
    out_shape = [sd((b, s, WIDTH), BF16), sd((b, WIDTH, s), BF16), sd((b, s, WIDTH), BF16),
                 sd((b, s, WIDTH), BF16), sd((b, WIDTH, s), BF16), sd((b, s, WIDTH), BF16),
                 sd((b, s, WIDTH), F32),
                 sd((b, s, LANES), F32), sd((b, LANES, s), F32), sd((b, 3 * HEAD_DIM, s), BF16)]
    rspec = pl.BlockSpec((1, tm, WIDTH), row)
    cspec = pl.BlockSpec((1, WIDTH, tm), col)
    out_specs = [rspec, cspec, rspec, rspec, cspec, rspec, rspec,
                 pl.BlockSpec((1, tm, LANES), row), pl.BlockSpec((1, LANES, tm), col),
                 pl.BlockSpec((1, 3 * HEAD_DIM, tm), col)]
    return pl.pallas_call(
        _proj_kernel,
        grid=(b, s // tm),
        in_specs=[pl.BlockSpec((1, tm, d), row),
                  pl.BlockSpec((1, N_MOD, d), lambda i, j: (i, 0, 0)),
                  _const_spec((1, d)),
                  _const_spec(w_main.shape), _const_spec(w_small.shape),
                  _const_spec((1, LANES)), _const_spec((2, WIDTH)), _const_spec((HEAD_DIM, 2))],
        out_specs=out_specs,
        out_shape=out_shape,
        scratch_shapes=[pltpu.VMEM((1, LANES), F32)],
        compiler_params=_cparams(2),
        name="in_proj",
    )(x, mod, gain.reshape(1, d), w_main, w_small, b_small, q_gains, k_gains)


FLASH_ROWS = 64
COUNT_ROWS = 128
LOG2E = 1.4426950408889634
SAFE_BOUND = 50.0


def _flash_scratch(tq, tk):
    return [pltpu.VMEM((N_HEADS, tq, LANES), F32), pltpu.VMEM((N_HEADS, tq, LANES), F32),
            pltpu.VMEM((N_HEADS, tq, LANES), F32), pltpu.VMEM((N_HEADS, tq, tk), BF16)]


def _flash_reset(m_ref, l_ref, acc_ref, online):
    if online:
        m_ref[...] = jnp.full(m_ref.shape, NEG, F32)
    l_ref[...] = jnp.zeros(l_ref.shape, F32)
    acc_ref[...] = jnp.zeros(acc_ref.shape, F32)


def _flash_step(q, kt_ref, v_ref, off, chunk_logits, m_ref, l_ref, acc_ref, p_ref, online=True):
    tq, tk = p_ref.shape[1:]
    n_sub = tk // LANES
    for h in range(N_HEADS):
        pair = h // 2
        q_h = q[:, h * HEAD_DIM:(h + 1) * HEAD_DIM]
        kt = kt_ref[0, h * HEAD_DIM:(h + 1) * HEAD_DIM, pl.ds(off, tk)]
        for r0 in range(0, tq, FLASH_ROWS):
            rows = pl.ds(r0, FLASH_ROWS)
            s = _dot(q_h[r0:r0 + FLASH_ROWS], kt)
            sc = [chunk_logits(h, r0, c, s[:, c * LANES:(c + 1) * LANES]) for c in range(n_sub)]
            m_row = m_ref[h, rows, :]
            if online:
                mx = sc[0]
                for x in sc[1:]:
                    mx = jnp.maximum(mx, x)
                m_old = m_row
                m_row = jnp.maximum(m_old, jnp.max(mx, axis=1, keepdims=True))
                alpha = jnp.exp2(m_old - m_row)
                m_ref[h, rows, :] = m_row
                acc_ref[h, rows, :] = alpha * acc_ref[h, rows, :]
            ps = [jnp.exp2(x - m_row) for x in sc]
            lsum = ps[0]
            for x in ps[1:]:
                lsum = lsum + x
            l_old = l_ref[h, rows, :]
            l_ref[h, rows, :] = (alpha * l_old if online else l_old) + lsum
            p_ref[h, rows, :] = jnp.concatenate(ps, axis=1).astype(BF16)
        v = v_ref[0, pl.ds(off, tk), pair * LANES:(pair + 1) * LANES]
        acc_ref[h] = acc_ref[h] + _dot(p_ref[h], v)


def _flash_finish(l_ref, acc_ref, o_ref):
    for pair in range(N_HEADS // 2):
        out = [acc_ref[h] * (1.0 / jnp.sum(l_ref[h], axis=1, keepdims=True))
               for h in (2 * pair, 2 * pair + 1)]
        lane = lax.broadcasted_iota(I32, out[0].shape, 1)
        o_ref[0, :, pair * LANES:(pair + 1) * LANES] = jnp.where(
            lane < HEAD_DIM, out[0], out[1]).astype(BF16)


def _dsa_kernel(bound_ref, qa_ref, qi_ref, small_ref, kat_ref, va_ref, kit_ref, o_ref, sc_ref,
                m_ref, l_ref, acc_ref, p_ref, *, tq, tk, topk):
    s_len = kat_ref.shape[2]
    t0 = pl.program_id(1) * tq
    nkb = (t0 + tq + tk - 1) // tk
    n_sub = tk // LANES
    row = lax.broadcasted_iota(I32, (tq, LANES), 0) + t0
    lane = lax.broadcasted_iota(I32, (COUNT_ROWS, LANES), 1)
    rep = lambda col: jnp.broadcast_to(col, (col.shape[0], LANES))

    wi = small_ref[0][:, SM_WI:SM_WI + N_HEADS]
    qi = qi_ref[0]
    lhs = []
    for h in range(N_HEADS):
        qh = qi[:, h * HEAD_DIM:(h + 1) * HEAD_DIM]
        hi = qh.astype(BF16)
        lo = (qh - hi.astype(F32)).astype(BF16)
        lhs.append(jnp.concatenate([hi, lo, hi], axis=1))
    lane_q = lax.broadcasted_iota(I32, (tq, LANES), 1)
    for h in range(N_HEADS):
        acc_ref[h] = jnp.broadcast_to(wi[:, h:h + 1], (tq, LANES))
    m_ref[0] = jnp.full((tq, LANES), jnp.inf, F32)
    m_ref[1] = jnp.full((tq, LANES), -jnp.inf, F32)
    m_ref[2] = jnp.full((tq, LANES), -jnp.inf, F32)

    def score_body(kb, _):
        off = pl.multiple_of(kb * tk, tk)
        kib = kit_ref[0, :, pl.ds(off, tk)]
        acc = [jnp.zeros((tq, LANES), F32) for _ in range(n_sub)]
        for h in range(N_HEADS):
            r = _dot(lhs[h], kib)
            w = acc_ref[h]
            for i in range(n_sub):
                acc[i] = acc[i] + jnp.maximum(r[:, i * LANES:(i + 1) * LANES], 0.0) * w
        mn, top1, top2 = m_ref[0], m_ref[1], m_ref[2]
        for i in range(n_sub):
            causal = lane_q + (off + i * LANES) <= row
            lowered = jnp.where(causal, acc[i], -jnp.inf)
            sc_ref[kb * n_sub + i] = lowered
            mn = jnp.minimum(mn, jnp.where(causal, acc[i], jnp.inf))
            top2 = jnp.maximum(top2, jnp.minimum(top1, lowered))
            top1 = jnp.maximum(top1, lowered)
        m_ref[0], m_ref[1], m_ref[2] = mn, top1, top2
        return 0

    lax.fori_loop(0, nkb, score_body, 0)
    mn, top2 = m_ref[0], m_ref[2]

    slabs = range(0, tq, COUNT_ROWS)

    def scan(step, init, *row_args, settled=None):
        outs = []
        for k, r0 in enumerate(slabs):
            args = [a[r0:r0 + COUNT_ROWS] for a in row_args]
            start = jnp.full((COUNT_ROWS, LANES), init, F32)

            def body(kb, acc, r0=r0, args=args):
                off = pl.multiple_of(kb * tk, tk)
                for i in range(n_sub):
                    blk = sc_ref[kb * n_sub + i, pl.ds(r0, COUNT_ROWS), :]
                    acc = step(acc, blk, lane + (off + i * LANES), *args)
                return acc

            run = functools.partial(lax.fori_loop, 0, nkb, body, start)
            outs.append(run() if settled is None
                        else lax.cond(settled[k] > 0.5, lambda start=start: start, run))
        return jnp.concatenate(outs, axis=0)

    def count(pred, *row_args, settled=None):
        c = scan(lambda acc, blk, pos, *a: acc + jnp.where(pred(blk, pos, *a), 1.0, 0.0),
                 0.0, *row_args, settled=settled)
        return rep(jnp.sum(c, axis=1, keepdims=True))

    def below_max(bound, settled):
        m = scan(lambda acc, blk, pos, b: jnp.maximum(acc, jnp.where(blk < b, blk, -jnp.inf)),
                 -jnp.inf, bound, settled=settled)
        return rep(jnp.max(m, axis=1, keepdims=True))

    def slab_settled(state):
        return tuple(jnp.min(state[r0:r0 + COUNT_ROWS]) for r0 in slabs)

    kf = float(topk)
    few = row < topk
    rmin = rep(jnp.min(mn, axis=1, keepdims=True))
    run_lo = rep(jnp.min(top2, axis=1, keepdims=True))
    run_hi = rep(jnp.max(top2, axis=1, keepdims=True))
    above = jnp.where(run_hi > 0.0, run_hi * (1.0 + 1e-6), run_hi * (1.0 - 1e-6)) + 1e-30
    lo0 = jnp.where(few, 0.0, jnp.maximum(run_lo, rmin))
    hi0 = jnp.where(few, 0.0, above)
    state0 = jnp.where(few, 1.0, 0.0)

    def search_body(_, st):
        lo, hi, state = st
        mid = 0.5 * lo + 0.5 * hi
        cnt = count(lambda blk, pos, m: blk >= m, mid)
        active = state == 0.0
        ge = cnt >= kf
        lo = jnp.where(active, jnp.where(ge, mid, lo), lo)
        hi = jnp.where(active, jnp.where(ge, hi, mid), hi)
        state = jnp.where(active, jnp.where(cnt == kf, 1.0, 0.0), state)
        return lo, hi, state

    lo, hi, state = lax.fori_loop(0, BISECT_STEPS, search_body, (lo0, hi0, state0))
    settled = slab_settled(state)
    thr = jnp.where(few, -jnp.inf, lo)

    def snap_body(st):
        settled, hi, thr, state = st
        cand = below_max(hi, settled)
        cnt = count(lambda blk, pos, c: blk >= c, cand, settled=settled)
        active = state == 0.0
        found = cnt >= kf
        thr = jnp.where(active, jnp.where(found, cand, thr), thr)
        hi = jnp.where(active, jnp.where(found, hi, cand), hi)
        state = jnp.where(active, jnp.where(found, jnp.where(cnt == kf, 1.0, 2.0), 0.0), state)
        return slab_settled(state), hi, thr, state

    _, _, thr, state = lax.while_loop(
        lambda st: functools.reduce(jnp.minimum, st[0]) < 0.5, snap_body, (settled, hi, thr, state))

    last = []
    for r0 in slabs:
        rows = pl.ds(r0, COUNT_ROWS)
        thr_r, row_r, state_r = (a[r0:r0 + COUNT_ROWS] for a in (thr, row, state))
        far0 = jnp.full((COUNT_ROWS, LANES), -1, I32)

        def mark(kb, i, keep, far, rows=rows, row_r=row_r):
            pos = lane + (kb * tk + i * LANES)
            keep = keep & (pos <= row_r)
            sc_ref[kb * n_sub + i, rows, :] = jnp.where(keep, 0.0, NEG)
            return jnp.maximum(far, jnp.where(keep, pos, -1))

        def plain(rows=rows, thr_r=thr_r, mark=mark, far0=far0):
            def body(kb, far):
                for i in range(n_sub):
                    far = mark(kb, i, sc_ref[kb * n_sub + i, rows, :] >= thr_r, far)
                return far
            return lax.fori_loop(0, nkb, body, far0)

        def tied(rows=rows, thr_r=thr_r, mark=mark, far0=far0):
            def above(kb, c):
                for i in range(n_sub):
                    c = c + jnp.where(sc_ref[kb * n_sub + i, rows, :] > thr_r, 1.0, 0.0)
                return c
            c = lax.fori_loop(0, nkb, above, jnp.zeros((COUNT_ROWS, LANES), F32))
            need = kf - rep(jnp.sum(c, axis=1, keepdims=True))
            tri = jnp.where(lax.broadcasted_iota(I32, (tk, tk), 0) <= lax.broadcasted_iota(I32, (tk, tk), 1),
                            1.0, 0.0).astype(BF16)

            def body(kb, carry):
                far, seen = carry
                blks = [sc_ref[kb * n_sub + i, rows, :] for i in range(n_sub)]
                is_tie = jnp.concatenate([jnp.where(blk == thr_r, 1.0, 0.0) for blk in blks], axis=1)
                rank = _dot(is_tie.astype(BF16), tri)
                for i, blk in enumerate(blks):
                    rank_i = seen + rank[:, i * LANES:(i + 1) * LANES]
                    far = mark(kb, i, (blk > thr_r) | ((blk == thr_r) & (rank_i <= need)), far)
                return far, seen + rep(rank[:, tk - 1:tk])
            return lax.fori_loop(0, nkb, body, (far0, jnp.zeros((COUNT_ROWS, LANES), F32)))[0]

        far = lax.cond(jnp.max(state_r) > 1.5, tied, plain)
        last.append(rep(jnp.max(far.astype(F32), axis=1, keepdims=True)))
    last = jnp.concatenate(last, axis=0)

    qa = qa_ref[0]
    kpos = lax.broadcasted_iota(I32, (1, tk), 1)
    slopes = [LOG2E * 2.0 ** (-8.0 * (h + 1) / N_HEADS) for h in range(N_HEADS)]

    def attend(online):
        def attn_body(kb, _):
            off = pl.multiple_of(kb * tk, tk)
            rel = (kpos + (off - t0)).astype(F32)

            def chunk_logits(h, r0, c, x):
                mask = sc_ref[kb * n_sub + c, pl.ds(r0, FLASH_ROWS), :]
                return x + (mask + slopes[h] * rel[:, c * LANES:(c + 1) * LANES])

            _flash_step(qa, kat_ref, va_ref, off, chunk_logits, m_ref, l_ref, acc_ref, p_ref, online)
            return 0

        lax.fori_loop(0, nkb, attn_body, 0)
        _flash_finish(l_ref, acc_ref, o_ref)

    bound = bound_ref[0, 0]

    @pl.when(bound <= SAFE_BOUND)
    def _():
        _flash_reset(m_ref, l_ref, acc_ref, online=False)
        for h in range(N_HEADS):
            m_ref[h] = bound + slopes[h] * (last - t0.astype(F32))
        attend(online=False)

    @pl.when(bound > SAFE_BOUND)
    def _():
        _flash_reset(m_ref, l_ref, acc_ref, online=True)
        attend(online=True)


def _dsa_attn(bound, qa, qi, small, kat, va, kit, *, tq=512, tk=512):
    b, s, _ = qa.shape
    row = lambda i, j: (i, j, 0)
    whole = lambda i, j: (i, 0, 0)
    once = pl.Buffered(1)
    topk = min(TOPK, s // 4)
    assert topk > LANES and s % tq == 0 and s % tk == 0, (s, tq, tk)
    kern = functools.partial(_dsa_kernel, tq=tq, tk=tk, topk=topk)
    return pl.pallas_call(
        kern,
        grid=(b, s // tq),
        in_specs=[pl.BlockSpec(memory_space=pltpu.SMEM),
                  pl.BlockSpec((1, tq, WIDTH), row), pl.BlockSpec((1, tq, WIDTH), row),
                  pl.BlockSpec((1, tq, LANES), row),
                  pl.BlockSpec((1, WIDTH, s), whole, pipeline_mode=once),
                  pl.BlockSpec((1, s, WIDTH), whole, pipeline_mode=once),
                  pl.BlockSpec((1, 3 * HEAD_DIM, s), whole, pipeline_mode=once)],
        out_specs=pl.BlockSpec((1, tq, WIDTH), row),
        out_shape=jax.ShapeDtypeStruct((b, s, WIDTH), BF16),
        scratch_shapes=[pltpu.VMEM((s // LANES, tq, LANES), F32)] + _flash_scratch(tq, tk),
        compiler_params=_cparams(2),
        name="dsa_attn",
    )(bound, qa, qi, small, kat, va, kit)


def _fox_kernel(bound_ref, qf_ref, small_ref, smallt_ref, kft_ref, vf_ref, o_ref,
                m_ref, l_ref, acc_ref, p_ref, *, tq, tk):
    t0 = pl.program_id(1) * tq
    n_full = t0 // tk
    nkb = (t0 + tq + tk - 1) // tk
    row = lax.broadcasted_iota(I32, (FLASH_ROWS, LANES), 0) + t0
    col = lax.broadcasted_iota(I32, (FLASH_ROWS, LANES), 1)
    qf = qf_ref[0]
    f_t0 = smallt_ref[0, SM_F:SM_F + N_HEADS, pl.ds(pl.multiple_of(t0, LANES), LANES)][:, 0:1]

    def attend(online):
        def attn_body(kb, _, causal):
            off = pl.multiple_of(kb * tk, tk)
            decay = LOG2E * (f_t0 - smallt_ref[0, SM_F:SM_F + N_HEADS, pl.ds(off, tk)])

            def chunk_logits(h, r0, c, x):
                x = x + decay[h:h + 1, c * LANES:(c + 1) * LANES]
                if causal:
                    x = jnp.where(col + (off + c * LANES) <= row + r0, x, NEG)
                return x

            _flash_step(qf, kft_ref, vf_ref, off, chunk_logits, m_ref, l_ref, acc_ref, p_ref, online)
            return 0

        lax.fori_loop(0, n_full, functools.partial(attn_body, causal=False), 0)
        lax.fori_loop(n_full, nkb, functools.partial(attn_body, causal=True), 0)
        _flash_finish(l_ref, acc_ref, o_ref)

    bound = bound_ref[0, 0]

    @pl.when(bound <= SAFE_BOUND)
    def _():
        _flash_reset(m_ref, l_ref, acc_ref, online=False)
        f_t = small_ref[0][:, SM_F:SM_F + N_HEADS]
        for h in range(N_HEADS):
            top = bound + LOG2E * (f_t0[h:h + 1, :] - f_t[:, h:h + 1])
            m_ref[h] = jnp.broadcast_to(top, (tq, LANES))
        attend(online=False)

    @pl.when(bound > SAFE_BOUND)
    def _():
        _flash_reset(m_ref, l_ref, acc_ref, online=True)
        attend(online=True)


def _fox_attn(bound, qf, small, smallt, kft, vf, *, tq=512, tk=512):
    b, s, _ = qf.shape
    row = lambda i, j: (i, j, 0)
    whole = lambda i, j: (i, 0, 0)
    once = pl.Buffered(1)
    kern = functools.partial(_fox_kernel, tq=tq, tk=tk)
    return pl.pallas_call(
        kern,
        grid=(b, s // tq),
        in_specs=[pl.BlockSpec(memory_space=pltpu.SMEM),
                  pl.BlockSpec((1, tq, WIDTH), row), pl.BlockSpec((1, tq, LANES), row),
                  pl.BlockSpec((1, LANES, s), whole, pipeline_mode=once),
                  pl.BlockSpec((1, WIDTH, s), whole, pipeline_mode=once),
                  pl.BlockSpec((1, s, WIDTH), whole, pipeline_mode=once)],
        out_specs=pl.BlockSpec((1, tq, WIDTH), row),
        out_shape=jax.ShapeDtypeStruct((b, s, WIDTH), BF16),
        scratch_shapes=_flash_scratch(tq, tk),
        compiler_params=_cparams(2),
        name="fox_attn",
    )(bound, qf, small, smallt, kft, vf)


def _mix_kernel(x_ref, mod_ref, g_ref, ad_ref, af_ref, wgate_ref, wbd_ref, wbf_ref, wo_ref, o_ref):
    d = x_ref.shape[2]
    x = x_ref[0]
    shift = mod_ref[0, 3:4, :]
    scale = mod_ref[0, 4:5, :]
    gate = mod_ref[0, 5:6, :]
    h = _rms_adaln(x, g_ref[...], scale, shift).astype(BF16)
    ga = _dot(h, wgate_ref[:, 0:d])
    gb = _dot(h, wgate_ref[:, d:2 * d])
    y_dsa = _dot(ad_ref[0], wbd_ref[...])
    y_fox = _dot(af_ref[0], wbf_ref[...])
    merged = jax.nn.sigmoid(ga) * y_dsa + jax.nn.sigmoid(gb) * y_fox
    o_ref[0] = x + gate * _dot(merged.astype(BF16), wo_ref[...])


def _mix_out(x, mod, gain, a_dsa, a_fox, w_gate, w_br_dsa, w_br_fox, w_out, *, tm=512):
    b, s, d = x.shape
    row = lambda i, j: (i, j, 0)
    return pl.pallas_call(
        _mix_kernel,
        grid=(b, s // tm),
        in_specs=[pl.BlockSpec((1, tm, d), row),
                  pl.BlockSpec((1, N_MOD, d), lambda i, j: (i, 0, 0)),
                  _const_spec((1, d)),
                  pl.BlockSpec((1, tm, WIDTH), row), pl.BlockSpec((1, tm, WIDTH), row),
                  _const_spec(w_gate.shape), _const_spec(w_br_dsa.shape),
                  _const_spec(w_br_fox.shape), _const_spec(w_out.shape)],
        out_specs=pl.BlockSpec((1, tm, d), row),
        out_shape=jax.ShapeDtypeStruct((b, s, d), F32),
        compiler_params=_cparams(2),
        name="mix_out",
    )(x, mod, gain.reshape(1, d), a_dsa, a_fox, w_gate, w_br_dsa, w_br_fox, w_out)


def _layer(x, mod, norm1_g, ffn1_wg, ffn1_wu, ffn1_wd, norm2_g, w_in, b_forget,
           qn_dsa, kn_dsa, qn_fox, kn_fox, w_br_dsa, w_br_fox, w_out, norm3_g,
           ffn2_wg, ffn2_wu, ffn2_wd):
    d = x.shape[2]
    bf = lambda w: w.astype(BF16)
    x = _ffn(x, mod, norm1_g, bf(ffn1_wg), bf(ffn1_wu), bf(ffn1_wd), mod_base=0)

    n_main = 7 * WIDTH
    n_small = HEAD_DIM + 2 * N_HEADS
    w_main = bf(w_in[:, :n_main])
    w_small = jnp.zeros((d, LANES), BF16).at[:, :n_small].set(bf(w_in[:, n_main:n_main + n_small]))
    w_gate = bf(w_in[:, n_main + n_small:])
    b_small = jnp.zeros((1, LANES), F32).at[0, SM_F:SM_F + N_HEADS].set(b_forget)
    q_scale = LOG2E * HEAD_DIM ** -0.5
    q_gains = jnp.stack([jnp.tile(qn_dsa, N_HEADS), jnp.tile(qn_fox, N_HEADS)]) * q_scale
    k_gains = jnp.stack([kn_dsa, kn_fox], axis=1)

    qa, kat, va, qf, kft, vf, qi, small, smallt, kit = _in_proj(
        x, mod, norm2_g, w_main, w_small, b_small, q_gains, k_gains)
    def logit_bound(qn, kn):
        b = 1.05 * HEAD_DIM * q_scale * jnp.max(jnp.abs(qn)) * jnp.max(jnp.abs(kn))
        return b.reshape(1, 1).astype(F32)

    a_dsa = _dsa_attn(logit_bound(qn_dsa, kn_dsa), qa, qi, small, kat, va, kit)
    a_fox = _fox_attn(logit_bound(qn_fox, kn_fox), qf, small, smallt, kft, vf)
    x = _mix_out(x, mod, norm2_g, a_dsa, a_fox, w_gate, bf(w_br_dsa), bf(w_br_fox), bf(w_out))
    return _ffn(x, mod, norm3_g, bf(ffn2_wg), bf(ffn2_wu), bf(ffn2_wd), mod_base=6)


def kernel(x, c, ada_w, ada_b, norm1_g, ffn1_wg, ffn1_wu, ffn1_wd, norm2_g, w_in, b_forget,
           qn_dsa, kn_dsa, qn_fox, kn_fox, w_br_dsa, w_br_fox, w_out, norm3_g,
           ffn2_wg, ffn2_wu, ffn2_wd):
    per_layer = (norm1_g, ffn1_wg, ffn1_wu, ffn1_wd, norm2_g, w_in, b_forget,
                 qn_dsa, kn_dsa, qn_fox, kn_fox, w_br_dsa, w_br_fox, w_out, norm3_g,
                 ffn2_wg, ffn2_wu, ffn2_wd)
    for l in range(ada_w.shape[0]):
        mod = _adaln_mod(c, ada_w, ada_b, l)
        x = _layer(x, mod, *(p[l] for p in per_layer))
    return x
```

```python
import functools

import jax
import jax.numpy as jnp
from jax import lax
from jax.experimental import pallas as pl
from jax.experimental.pallas import tpu as pltpu

F32 = jnp.float32
BF16 = jnp.bfloat16
I32 = jnp.int32

HEAD_DIM = 64
N_HEADS = 8
WIDTH = N_HEADS * HEAD_DIM
TOPK = 256
EPS = 1e-6
N_MOD = 9
LANES = 128
NEG = -1e30
BISECT_STEPS = 14
VMEM_LIMIT = 56 * 1024 * 1024

SM_WI = 64
SM_F = 72


def _cparams(n_axes):
    return pltpu.CompilerParams(
        dimension_semantics=("arbitrary",) * n_axes, vmem_limit_bytes=VMEM_LIMIT)


def _const_spec(shape):
    nd = len(shape)
    return pl.BlockSpec(shape, lambda *_: (0,) * nd)


def _dot(a, b):
    return jnp.dot(a, b, preferred_element_type=F32)


def _rms_adaln(x, gain, scale, shift):
    y = x * lax.rsqrt(jnp.mean(x * x, axis=-1, keepdims=True) + EPS)
    return (y * gain) * (1.0 + scale) + shift


def _mod_kernel(c_ref, w_ref, b_ref, o_ref):
    c = c_ref[...]
    a = c * jax.nn.sigmoid(c)
    o_ref[...] = jnp.dot(a, w_ref[0], preferred_element_type=F32,
                         precision=lax.Precision.HIGHEST) + b_ref[0]


def _adaln_mod(c, ada_w, ada_b, layer):
    b, d = c.shape
    n = ada_w.shape[2]
    rows = 8
    tn = n // 8
    c_pad = jnp.zeros((rows, d), F32).at[:b].set(c)
    out = pl.pallas_call(
        _mod_kernel,
        grid=(n // tn,),
        in_specs=[pl.BlockSpec((rows, d), lambda j: (0, 0)),
                  pl.BlockSpec((1, d, tn), lambda j: (layer, 0, j)),
                  pl.BlockSpec((1, 1, tn), lambda j: (layer, 0, j))],
        out_specs=pl.BlockSpec((rows, tn), lambda j: (0, j)),
        out_shape=jax.ShapeDtypeStruct((rows, n), F32),
        compiler_params=_cparams(1),
        name="adaln_mod",
    )(c_pad, ada_w, ada_b.reshape(ada_b.shape[0], 1, n))
    return out[:b].reshape(b, N_MOD, d)


def _ffn_kernel(x_ref, mod_ref, g_ref, wg_ref, wu_ref, wd_ref, o_ref, *, mod_base, n_chunks):
    x = x_ref[0]
    shift = mod_ref[0, mod_base:mod_base + 1, :]
    scale = mod_ref[0, mod_base + 1:mod_base + 2, :]
    gate = mod_ref[0, mod_base + 2:mod_base + 3, :]
    h = _rms_adaln(x, g_ref[...], scale, shift).astype(BF16)
    dff = wg_ref.shape[1]
    ck = dff // n_chunks
    acc = jnp.zeros(x.shape, F32)
    for i in range(n_chunks):
        g = _dot(h, wg_ref[:, i * ck:(i + 1) * ck])
        u = _dot(h, wu_ref[:, i * ck:(i + 1) * ck])
        a = (g * jax.nn.sigmoid(g) * u).astype(BF16)
        acc = acc + _dot(a, wd_ref[i * ck:(i + 1) * ck, :])
    o_ref[0] = x + (0.5 * gate) * acc


def _ffn(x, mod, gain, wg, wu, wd, *, mod_base, tm=512):
    b, s, d = x.shape
    dff = wg.shape[1]
    kern = functools.partial(_ffn_kernel, mod_base=mod_base, n_chunks=2)
    return pl.pallas_call(
        kern,
        grid=(b, s // tm),
        in_specs=[pl.BlockSpec((1, tm, d), lambda i, j: (i, j, 0)),
                  pl.BlockSpec((1, N_MOD, d), lambda i, j: (i, 0, 0)),
                  _const_spec((1, d)),
                  _const_spec((d, dff)), _const_spec((d, dff)), _const_spec((dff, d))],
        out_specs=pl.BlockSpec((1, tm, d), lambda i, j: (i, j, 0)),
        out_shape=jax.ShapeDtypeStruct((b, s, d), F32),
        compiler_params=_cparams(2),
        name="ffn",
    )(x, mod, gain.reshape(1, d), wg, wu, wd)


def _split3_bf16(v):
    p1 = v.astype(BF16)
    r1 = v - p1.astype(F32)
    p2 = r1.astype(BF16)
    r2 = r1 - p2.astype(F32)
    return p1, p2, r2.astype(BF16)


def _proj_kernel(x_ref, mod_ref, g_ref, wm_ref, ws_ref, bsm_ref, qg_ref, kg_ref,
                 qa_ref, kat_ref, va_ref, qf_ref, kft_ref, vf_ref, qi_ref,
                 small_ref, smallt_ref, kit_ref, carry_ref):
    tm = x_ref.shape[1]
    x = x_ref[0]
    shift = mod_ref[0, 3:4, :]
    scale = mod_ref[0, 4:5, :]
    h = _rms_adaln(x, g_ref[...], scale, shift).astype(BF16)

    def z(i):
        return _dot(h, wm_ref[:, i * WIDTH:(i + 1) * WIDTH])

    r = lax.broadcasted_iota(I32, (WIDTH, WIDTH), 0) // HEAD_DIM
    c = lax.broadcasted_iota(I32, (WIDTH, WIDTH), 1) // HEAD_DIM
    avg = jnp.where(r == c, 1.0 / HEAD_DIM, 0.0).astype(BF16)

    def norm_q(q, gain_row):
        ms = _dot((q * q).astype(BF16), avg)
        return (q * lax.rsqrt(ms + EPS) * gain_row).astype(BF16)

    def norm_kt(k, gain_col):
        kt = k.T.reshape(N_HEADS, HEAD_DIM, tm)
        ms = jnp.mean(kt * kt, axis=1, keepdims=True)
        kt = kt * lax.rsqrt(ms + EPS) * gain_col[None]
        return kt.reshape(WIDTH, tm).astype(BF16)

    qa_ref[0] = norm_q(z(0), qg_ref[0:1, :])
    kat_ref[0] = norm_kt(z(1), kg_ref[:, 0:1])
    va_ref[0] = z(2).astype(BF16)
    qf_ref[0] = norm_q(z(3), qg_ref[1:2, :])
    kft_ref[0] = norm_kt(z(4), kg_ref[:, 1:2])
    vf_ref[0] = z(5).astype(BF16)
    qi_ref[0] = z(6) * (HEAD_DIM ** -0.5)

    zs = _dot(h, ws_ref[...])
    pre = zs + bsm_ref[...]
    logf = jnp.minimum(pre, 0.0) - jnp.log(1.0 + jnp.exp(-jnp.abs(pre)))

    @pl.when(pl.program_id(1) == 0)
    def _():
        carry_ref[...] = jnp.zeros_like(carry_ref)

    ri = lax.broadcasted_iota(I32, (tm, tm), 0)
    ci = lax.broadcasted_iota(I32, (tm, tm), 1)
    tri = jnp.where(ci <= ri, 1.0, 0.0).astype(BF16)
    p1, p2, p3 = _split3_bf16(logf)
    cum = (_dot(tri, p1) + _dot(tri, p2)) + _dot(tri, p3) + carry_ref[...]
    carry_ref[...] = cum[tm - 1:tm, :]

    lane = lax.broadcasted_iota(I32, (tm, LANES), 1)
    small = jnp.where(lane < SM_WI, zs,
                      jnp.where(lane < SM_F, zs * (N_HEADS ** -0.5), cum))
    small_ref[0] = small
    st = small.T
    smallt_ref[0] = st
    ki = st[0:HEAD_DIM, :]
    ki_hi = ki.astype(BF16)
    ki_lo = (ki - ki_hi.astype(F32)).astype(BF16)
    kit_ref[0] = jnp.concatenate([ki_hi, ki_hi, ki_lo], axis=0)


def _in_proj(x, mod, gain, w_main, w_small, b_small, q_gains, k_gains, *, tm=512):
    b, s, d = x.shape
    row = lambda i, j: (i, j, 0)
    col = lambda i, j: (i, 0, j)
    sd = jax.ShapeDtypeStruct
    out_shape = [sd((b, s, WIDTH), BF16), sd((b, WIDTH, s), BF16), sd((b, s, WIDTH), BF16),
                 sd((b, s, WIDTH), BF16), sd((b, WIDTH, s), BF16), sd((b, s, WIDTH), BF16),
                 sd((b, s, WIDTH), F32),
                 sd((b, s, LANES), F32), sd((b, LANES, s), F32), sd((b, 3 * HEAD_DIM, s), BF16)]
    rspec = pl.BlockSpec((1, tm, WIDTH), row)
    cspec = pl.BlockSpec((1, WIDTH, tm), col)
    out_specs = [rspec, cspec, rspec, rspec, cspec, rspec, rspec,
                 pl.BlockSpec((1, tm, LANES), row), pl.BlockSpec((1, LANES, tm), col),
                 pl.BlockSpec((1, 3 * HEAD_DIM, tm), col)]
    return pl.pallas_call(
        _proj_kernel,
        grid=(b, s // tm),
        in_specs=[pl.BlockSpec((1, tm, d), row),
                  pl.BlockSpec((1, N_MOD, d), lambda i, j: (i, 0, 0)),
                  _const_spec((1, d)),
                  _const_spec(w_main.shape), _const_spec(w_small.shape),
                  _const_spec((1, LANES)), _const_spec((2, WIDTH)), _const_spec((HEAD_DIM, 2))],
        out_specs=out_specs,
        out_shape=out_shape,
        scratch_shapes=[pltpu.VMEM((1, LANES), F32)],
        compiler_params=_cparams(2),
        name="in_proj",
    )(x, mod, gain.reshape(1, d), w_main, w_small, b_small, q_gains, k_gains)


FLASH_ROWS = 128
COUNT_ROWS = 128
LOG2E = 1.4426950408889634
SAFE_BOUND = 50.0


def _flash_scratch(tq, tk):
    return [pltpu.VMEM((N_HEADS, tq, LANES), F32), pltpu.VMEM((N_HEADS, tq, LANES), F32),
            pltpu.VMEM((N_HEADS, tq, LANES), F32), pltpu.VMEM((N_HEADS, tq, tk), BF16)]


def _flash_reset(m_ref, l_ref, acc_ref, online):
    if online:
        m_ref[...] = jnp.full(m_ref.shape, NEG, F32)
    l_ref[...] = jnp.zeros(l_ref.shape, F32)
    acc_ref[...] = jnp.zeros(acc_ref.shape, F32)


def _flash_step(q, kt_ref, v_ref, off, chunk_logits, m_ref, l_ref, acc_ref, p_ref, online=True):
    tq, tk = p_ref.shape[1:]
    n_sub = tk // LANES
    for h in range(N_HEADS):
        pair = h // 2
        q_h = q[:, h * HEAD_DIM:(h + 1) * HEAD_DIM]
        kt = kt_ref[0, h * HEAD_DIM:(h + 1) * HEAD_DIM, pl.ds(off, tk)]
        for r0 in range(0, tq, FLASH_ROWS):
            rows = pl.ds(r0, FLASH_ROWS)
            s = _dot(q_h[r0:r0 + FLASH_ROWS], kt)
            sc = [chunk_logits(h, r0, c, s[:, c * LANES:(c + 1) * LANES]) for c in range(n_sub)]
            m_row = m_ref[h, rows, :]
            if online:
                mx = sc[0]
                for x in sc[1:]:
                    mx = jnp.maximum(mx, x)
                m_old = m_row
                m_row = jnp.maximum(m_old, jnp.max(mx, axis=1, keepdims=True))
                alpha = jnp.exp2(m_old - m_row)
                m_ref[h, rows, :] = m_row
                acc_ref[h, rows, :] = alpha * acc_ref[h, rows, :]
            ps = [jnp.exp2(x - m_row) for x in sc]
            lsum = ps[0]
            for x in ps[1:]:
                lsum = lsum + x
            l_old = l_ref[h, rows, :]
            l_ref[h, rows, :] = (alpha * l_old if online else l_old) + lsum
            p_ref[h, rows, :] = jnp.concatenate(ps, axis=1).astype(BF16)
        v = v_ref[0, pl.ds(off, tk), pair * LANES:(pair + 1) * LANES]
        acc_ref[h] = acc_ref[h] + _dot(p_ref[h], v)


def _flash_finish(l_ref, acc_ref, o_ref):
    for pair in range(N_HEADS // 2):
        out = [acc_ref[h] * (1.0 / jnp.sum(l_ref[h], axis=1, keepdims=True))
               for h in (2 * pair, 2 * pair + 1)]
        lane = lax.broadcasted_iota(I32, out[0].shape, 1)
        o_ref[0, :, pair * LANES:(pair + 1) * LANES] = jnp.where(
            lane < HEAD_DIM, out[0], out[1]).astype(BF16)


def _dsa_kernel(bound_ref, qa_ref, qi_ref, small_ref, kat_ref, va_ref, kit_ref, o_ref, sc_ref,
                m_ref, l_ref, acc_ref, p_ref, *, tq, tk, topk):
    s_len = kat_ref.shape[2]
    t0 = pl.program_id(1) * tq
    nkb = (t0 + tq + tk - 1) // tk
    n_sub = tk // LANES
    row = lax.broadcasted_iota(I32, (tq, LANES), 0) + t0
    lane = lax.broadcasted_iota(I32, (COUNT_ROWS, LANES), 1)
    rep = lambda col: jnp.broadcast_to(col, (col.shape[0], LANES))

    wi = small_ref[0][:, SM_WI:SM_WI + N_HEADS]
    qi = qi_ref[0]
    lhs = []
    for h in range(N_HEADS):
        qh = qi[:, h * HEAD_DIM:(h + 1) * HEAD_DIM]
        hi = qh.astype(BF16)
        lo = (qh - hi.astype(F32)).astype(BF16)
        lhs.append(jnp.concatenate([hi, lo, hi], axis=1))
    lane_q = lax.broadcasted_iota(I32, (tq, LANES), 1)
    for h in range(N_HEADS):
        acc_ref[h] = jnp.broadcast_to(wi[:, h:h + 1], (tq, LANES))
    m_ref[0] = jnp.full((tq, LANES), jnp.inf, F32)
    m_ref[1] = jnp.full((tq, LANES), -jnp.inf, F32)
    m_ref[2] = jnp.full((tq, LANES), -jnp.inf, F32)

    def score_body(kb, _):
        off = pl.multiple_of(kb * tk, tk)
        kib = kit_ref[0, :, pl.ds(off, tk)]
        acc = [jnp.zeros((tq, LANES), F32) for _ in range(n_sub)]
        for h in range(N_HEADS):
            r = _dot(lhs[h], kib)
            w = acc_ref[h]
            for i in range(n_sub):
                acc[i] = acc[i] + jnp.maximum(r[:, i * LANES:(i + 1) * LANES], 0.0) * w
        mn, top1, top2 = m_ref[0], m_ref[1], m_ref[2]
        for i in range(n_sub):
            causal = lane_q + (off + i * LANES) <= row
            lowered = jnp.where(causal, acc[i], -jnp.inf)
            sc_ref[kb * n_sub + i] = lowered
            mn = jnp.minimum(mn, jnp.where(causal, acc[i], jnp.inf))
            top2 = jnp.maximum(top2, jnp.minimum(top1, lowered))
            top1 = jnp.maximum(top1, lowered)
        m_ref[0], m_ref[1], m_ref[2] = mn, top1, top2
        return 0

    lax.fori_loop(0, nkb, score_body, 0)
    mn, top2 = m_ref[0], m_ref[2]

    slabs = range(0, tq, COUNT_ROWS)

    def scan(step, init, *row_args, settled=None):
        outs = []
        for k, r0 in enumerate(slabs):
            args = [a[r0:r0 + COUNT_ROWS] for a in row_args]
            start = jnp.full((COUNT_ROWS, LANES), init, F32)

            def body(kb, acc, r0=r0, args=args):
                off = pl.multiple_of(kb * tk, tk)
                for i in range(n_sub):
                    blk = sc_ref[kb * n_sub + i, pl.ds(r0, COUNT_ROWS), :]
                    acc = step(acc, blk, lane + (off + i * LANES), *args)
                return acc

            run = functools.partial(lax.fori_loop, 0, nkb, body, start)
            outs.append(run() if settled is None
                        else lax.cond(settled[k] > 0.5, lambda start=start: start, run))
        return jnp.concatenate(outs, axis=0)

    def count(pred, *row_args, settled=None):
        c = scan(lambda acc, blk, pos, *a: acc + jnp.where(pred(blk, pos, *a), 1.0, 0.0),
                 0.0, *row_args, settled=settled)
        return rep(jnp.sum(c, axis=1, keepdims=True))

    def below_max(bound, settled):
        m = scan(lambda acc, blk, pos, b: jnp.maximum(acc, jnp.where(blk < b, blk, -jnp.inf)),
                 -jnp.inf, bound, settled=settled)
        return rep(jnp.max(m, axis=1, keepdims=True))

    def slab_settled(state):
        return tuple(jnp.min(state[r0:r0 + COUNT_ROWS]) for r0 in slabs)

    kf = float(topk)
    few = row < topk
    rmin = rep(jnp.min(mn, axis=1, keepdims=True))
    run_lo = rep(jnp.min(top2, axis=1, keepdims=True))
    run_hi = rep(jnp.max(top2, axis=1, keepdims=True))
    above = jnp.where(run_hi > 0.0, run_hi * (1.0 + 1e-6), run_hi * (1.0 - 1e-6)) + 1e-30
    lo0 = jnp.where(few, 0.0, jnp.maximum(run_lo, rmin))
    hi0 = jnp.where(few, 0.0, above)
    state0 = jnp.where(few, 1.0, 0.0)

    def search_body(_, st):
        lo, hi, state = st
        mid = 0.5 * lo + 0.5 * hi
        cnt = count(lambda blk, pos, m: blk >= m, mid)
        active = state == 0.0
        ge = cnt >= kf
        lo = jnp.where(active, jnp.where(ge, mid, lo), lo)
        hi = jnp.where(active, jnp.where(ge, hi, mid), hi)
        state = jnp.where(active, jnp.where(cnt == kf, 1.0, 0.0), state)
        return lo, hi, state

    lo, hi, state = lax.fori_loop(0, BISECT_STEPS, search_body, (lo0, hi0, state0))
    settled = slab_settled(state)
    thr = jnp.where(few, -jnp.inf, lo)

    def snap_body(st):
        settled, hi, thr, state = st
        cand = below_max(hi, settled)
        cnt = count(lambda blk, pos, c: blk >= c, cand, settled=settled)
        active = state == 0.0
        found = cnt >= kf
        thr = jnp.where(active, jnp.where(found, cand, thr), thr)
        hi = jnp.where(active, jnp.where(found, hi, cand), hi)
        state = jnp.where(active, jnp.where(found, jnp.where(cnt == kf, 1.0, 2.0), 0.0), state)
        return slab_settled(state), hi, thr, state

    _, _, thr, state = lax.while_loop(
        lambda st: functools.reduce(jnp.minimum, st[0]) < 0.5, snap_body, (settled, hi, thr, state))

    last = []
    for r0 in slabs:
        rows = pl.ds(r0, COUNT_ROWS)
        thr_r, row_r, state_r = (a[r0:r0 + COUNT_ROWS] for a in (thr, row, state))
        far0 = jnp.full((COUNT_ROWS, LANES), -1, I32)

        def mark(kb, i, keep, far, rows=rows, row_r=row_r):
            pos = lane + (kb * tk + i * LANES)
            keep = keep & (pos <= row_r)
            sc_ref[kb * n_sub + i, rows, :] = jnp.where(keep, 0.0, NEG)
            return jnp.maximum(far, jnp.where(keep, pos, -1))

        def plain(rows=rows, thr_r=thr_r, mark=mark, far0=far0):
            def body(kb, far):
                for i in range(n_sub):
                    far = mark(kb, i, sc_ref[kb * n_sub + i, rows, :] >= thr_r, far)
                return far
            return lax.fori_loop(0, nkb, body, far0)

        def tied(rows=rows, thr_r=thr_r, mark=mark, far0=far0):
            def above(kb, c):
                for i in range(n_sub):
                    c = c + jnp.where(sc_ref[kb * n_sub + i, rows, :] > thr_r, 1.0, 0.0)
                return c
            c = lax.fori_loop(0, nkb, above, jnp.zeros((COUNT_ROWS, LANES), F32))
            need = kf - rep(jnp.sum(c, axis=1, keepdims=True))
            tri = jnp.where(lax.broadcasted_iota(I32, (tk, tk), 0) <= lax.broadcasted_iota(I32, (tk, tk), 1),
                            1.0, 0.0).astype(BF16)

            def body(kb, carry):
                far, seen = carry
                blks = [sc_ref[kb * n_sub + i, rows, :] for i in range(n_sub)]
                is_tie = jnp.concatenate([jnp.where(blk == thr_r, 1.0, 0.0) for blk in blks], axis=1)
                rank = _dot(is_tie.astype(BF16), tri)
                for i, blk in enumerate(blks):
                    rank_i = seen + rank[:, i * LANES:(i + 1) * LANES]
                    far = mark(kb, i, (blk > thr_r) | ((blk == thr_r) & (rank_i <= need)), far)
                return far, seen + rep(rank[:, tk - 1:tk])
            return lax.fori_loop(0, nkb, body, (far0, jnp.zeros((COUNT_ROWS, LANES), F32)))[0]

        far = lax.cond(jnp.max(state_r) > 1.5, tied, plain)
        last.append(rep(jnp.max(far.astype(F32), axis=1, keepdims=True)))
    last = jnp.concatenate(last, axis=0)

    qa = qa_ref[0]
    kpos = lax.broadcasted_iota(I32, (1, tk), 1)
    slopes = [LOG2E * 2.0 ** (-8.0 * (h + 1) / N_HEADS) for h in range(N_HEADS)]

    def attend(online):
        def attn_body(kb, _):
            off = pl.multiple_of(kb * tk, tk)
            rel = (kpos + (off - t0)).astype(F32)

            def chunk_logits(h, r0, c, x):
                mask = sc_ref[kb * n_sub + c, pl.ds(r0, FLASH_ROWS), :]
                return x + (mask + slopes[h] * rel[:, c * LANES:(c + 1) * LANES])

            _flash_step(qa, kat_ref, va_ref, off, chunk_logits, m_ref, l_ref, acc_ref, p_ref, online)
            return 0

        lax.fori_loop(0, nkb, attn_body, 0)
        _flash_finish(l_ref, acc_ref, o_ref)

    bound = bound_ref[0, 0]

    @pl.when(bound <= SAFE_BOUND)
    def _():
        _flash_reset(m_ref, l_ref, acc_ref, online=False)
        for h in range(N_HEADS):
            m_ref[h] = bound + slopes[h] * (last - t0.astype(F32))
        attend(online=False)

    @pl.when(bound > SAFE_BOUND)
    def _():
        _flash_reset(m_ref, l_ref, acc_ref, online=True)
        attend(online=True)


def _dsa_attn(bound, qa, qi, small, kat, va, kit, *, tq=512, tk=512):
    b, s, _ = qa.shape
    row = lambda i, j: (i, j, 0)
    whole = lambda i, j: (i, 0, 0)
    once = pl.Buffered(1)
    topk = min(TOPK, s // 4)
    assert topk > LANES and s % tq == 0 and s % tk == 0, (s, tq, tk)
    kern = functools.partial(_dsa_kernel, tq=tq, tk=tk, topk=topk)
    return pl.pallas_call(
        kern,
        grid=(b, s // tq),
        in_specs=[pl.BlockSpec(memory_space=pltpu.SMEM),
                  pl.BlockSpec((1, tq, WIDTH), row), pl.BlockSpec((1, tq, WIDTH), row),
                  pl.BlockSpec((1, tq, LANES), row),
                  pl.BlockSpec((1, WIDTH, s), whole, pipeline_mode=once),
                  pl.BlockSpec((1, s, WIDTH), whole, pipeline_mode=once),
                  pl.BlockSpec((1, 3 * HEAD_DIM, s), whole, pipeline_mode=once)],
        out_specs=pl.BlockSpec((1, tq, WIDTH), row),
        out_shape=jax.ShapeDtypeStruct((b, s, WIDTH), BF16),
        scratch_shapes=[pltpu.VMEM((s // LANES, tq, LANES), F32)] + _flash_scratch(tq, tk),
        compiler_params=_cparams(2),
        name="dsa_attn",
    )(bound, qa, qi, small, kat, va, kit)


def _fox_kernel(bound_ref, qf_ref, small_ref, smallt_ref, kft_ref, vf_ref, o_ref,
                m_ref, l_ref, acc_ref, p_ref, *, tq, tk):
    t0 = pl.program_id(1) * tq
    n_full = t0 // tk
    nkb = (t0 + tq + tk - 1) // tk
    row = lax.broadcasted_iota(I32, (FLASH_ROWS, LANES), 0) + t0
    col = lax.broadcasted_iota(I32, (FLASH_ROWS, LANES), 1)
    qf = qf_ref[0]
    f_t0 = smallt_ref[0, SM_F:SM_F + N_HEADS, pl.ds(pl.multiple_of(t0, LANES), LANES)][:, 0:1]

    def attend(online):
        def attn_body(kb, _, causal):
            off = pl.multiple_of(kb * tk, tk)
            decay = LOG2E * (f_t0 - smallt_ref[0, SM_F:SM_F + N_HEADS, pl.ds(off, tk)])

            def chunk_logits(h, r0, c, x):
                x = x + decay[h:h + 1, c * LANES:(c + 1) * LANES]
                if causal:
                    x = jnp.where(col + (off + c * LANES) <= row + r0, x, NEG)
                return x

            _flash_step(qf, kft_ref, vf_ref, off, chunk_logits, m_ref, l_ref, acc_ref, p_ref, online)
            return 0

        lax.fori_loop(0, n_full, functools.partial(attn_body, causal=False), 0)
        lax.fori_loop(n_full, nkb, functools.partial(attn_body, causal=True), 0)
        _flash_finish(l_ref, acc_ref, o_ref)

    bound = bound_ref[0, 0]

    @pl.when(bound <= SAFE_BOUND)
    def _():
        _flash_reset(m_ref, l_ref, acc_ref, online=False)
        f_t = small_ref[0][:, SM_F:SM_F + N_HEADS]
        for h in range(N_HEADS):
            top = bound + LOG2E * (f_t0[h:h + 1, :] - f_t[:, h:h + 1])
            m_ref[h] = jnp.broadcast_to(top, (tq, LANES))
        attend(online=False)

    @pl.when(bound > SAFE_BOUND)
    def _():
        _flash_reset(m_ref, l_ref, acc_ref, online=True)
        attend(online=True)


def _fox_attn(bound, qf, small, smallt, kft, vf, *, tq=512, tk=512):
    b, s, _ = qf.shape
    row = lambda i, j: (i, j, 0)
    whole = lambda i, j: (i, 0, 0)
    once = pl.Buffered(1)
    kern = functools.partial(_fox_kernel, tq=tq, tk=tk)
    return pl.pallas_call(
        kern,
        grid=(b, s // tq),
        in_specs=[pl.BlockSpec(memory_space=pltpu.SMEM),
                  pl.BlockSpec((1, tq, WIDTH), row), pl.BlockSpec((1, tq, LANES), row),
                  pl.BlockSpec((1, LANES, s), whole, pipeline_mode=once),
                  pl.BlockSpec((1, WIDTH, s), whole, pipeline_mode=once),
                  pl.BlockSpec((1, s, WIDTH), whole, pipeline_mode=once)],
        out_specs=pl.BlockSpec((1, tq, WIDTH), row),
        out_shape=jax.ShapeDtypeStruct((b, s, WIDTH), BF16),
        scratch_shapes=_flash_scratch(tq, tk),
        compiler_params=_cparams(2),
        name="fox_attn",
    )(bound, qf, small, smallt, kft, vf)


def _mix_kernel(x_ref, mod_ref, g_ref, ad_ref, af_ref, wgate_ref, wbd_ref, wbf_ref, wo_ref, o_ref):
    d = x_ref.shape[2]
    x = x_ref[0]
    shift = mod_ref[0, 3:4, :]
    scale = mod_ref[0, 4:5, :]
    gate = mod_ref[0, 5:6, :]
    h = _rms_adaln(x, g_ref[...], scale, shift).astype(BF16)
    ga = _dot(h, wgate_ref[:, 0:d])
    gb = _dot(h, wgate_ref[:, d:2 * d])
    y_dsa = _dot(ad_ref[0], wbd_ref[...])
    y_fox = _dot(af_ref[0], wbf_ref[...])
    merged = jax.nn.sigmoid(ga) * y_dsa + jax.nn.sigmoid(gb) * y_fox
    o_ref[0] = x + gate * _dot(merged.astype(BF16), wo_ref[...])


def _mix_out(x, mod, gain, a_dsa, a_fox, w_gate, w_br_dsa, w_br_fox, w_out, *, tm=512):
    b, s, d = x.shape
    row = lambda i, j: (i, j, 0)
    return pl.pallas_call(
        _mix_kernel,
        grid=(b, s // tm),
        in_specs=[pl.BlockSpec((1, tm, d), row),
                  pl.BlockSpec((1, N_MOD, d), lambda i, j: (i, 0, 0)),
                  _const_spec((1, d)),
                  pl.BlockSpec((1, tm, WIDTH), row), pl.BlockSpec((1, tm, WIDTH), row),
                  _const_spec(w_gate.shape), _const_spec(w_br_dsa.shape),
                  _const_spec(w_br_fox.shape), _const_spec(w_out.shape)],
        out_specs=pl.BlockSpec((1, tm, d), row),
        out_shape=jax.ShapeDtypeStruct((b, s, d), F32),
        compiler_params=_cparams(2),
        name="mix_out",
    )(x, mod, gain.reshape(1, d), a_dsa, a_fox, w_gate, w_br_dsa, w_br_fox, w_out)


def _layer(x, mod, norm1_g, ffn1_wg, ffn1_wu, ffn1_wd, norm2_g, w_in, b_forget,
           qn_dsa, kn_dsa, qn_fox, kn_fox, w_br_dsa, w_br_fox, w_out, norm3_g,
           ffn2_wg, ffn2_wu, ffn2_wd):
    d = x.shape[2]
    bf = lambda w: w.astype(BF16)
    x = _ffn(x, mod, norm1_g, bf(ffn1_wg), bf(ffn1_wu), bf(ffn1_wd), mod_base=0)

    n_main = 7 * WIDTH
    n_small = HEAD_DIM + 2 * N_HEADS
    w_main = bf(w_in[:, :n_main])
    w_small = jnp.zeros((d, LANES), BF16).at[:, :n_small].set(bf(w_in[:, n_main:n_main + n_small]))
    w_gate = bf(w_in[:, n_main + n_small:])
    b_small = jnp.zeros((1, LANES), F32).at[0, SM_F:SM_F + N_HEADS].set(b_forget)
    q_scale = LOG2E * HEAD_DIM ** -0.5
    q_gains = jnp.stack([jnp.tile(qn_dsa, N_HEADS), jnp.tile(qn_fox, N_HEADS)]) * q_scale
    k_gains = jnp.stack([kn_dsa, kn_fox], axis=1)

    qa, kat, va, qf, kft, vf, qi, small, smallt, kit = _in_proj(
        x, mod, norm2_g, w_main, w_small, b_small, q_gains, k_gains)
    def logit_bound(qn, kn):
        b = 1.05 * HEAD_DIM * q_scale * jnp.max(jnp.abs(qn)) * jnp.max(jnp.abs(kn))
        return b.reshape(1, 1).astype(F32)

    a_dsa = _dsa_attn(logit_bound(qn_dsa, kn_dsa), qa, qi, small, kat, va, kit)
    a_fox = _fox_attn(logit_bound(qn_fox, kn_fox), qf, small, smallt, kft, vf)
    x = _mix_out(x, mod, norm2_g, a_dsa, a_fox, w_gate, bf(w_br_dsa), bf(w_br_fox), bf(w_out))
    return _ffn(x, mod, norm3_g, bf(ffn2_wg), bf(ffn2_wu), bf(ffn2_wd), mod_base=6)


def kernel(x, c, ada_w, ada_b, norm1_g, ffn1_wg, ffn1_wu, ffn1_wd, norm2_g, w_in, b_forget,
           qn_dsa, kn_dsa, qn_fox, kn_fox, w_br_dsa, w_br_fox, w_out, norm3_g,
           ffn2_wg, ffn2_wu, ffn2_wd):
    per_layer = (norm1_g, ffn1_wg, ffn1_wu, ffn1_wd, norm2_g, w_in, b_forget,
                 qn_dsa, kn_dsa, qn_fox, kn_fox, w_br_dsa, w_br_fox, w_out, norm3_g,
                 ffn2_wg, ffn2_wu, ffn2_wd)
    for l in range(ada_w.shape[0]):
        mod = _adaln_mod(c, ada_w, ada_b, l)
        x = _layer(x, mod, *(p[l] for p in per_layer))
    return x
```

```python
import functools

import jax
import jax.numpy as jnp
from jax import lax
from jax.experimental import pallas as pl
from jax.experimental.pallas import tpu as pltpu

F32 = jnp.float32
BF16 = jnp.bfloat16
I32 = jnp.int32

HEAD_DIM = 64
N_HEADS = 8
WIDTH = N_HEADS * HEAD_DIM
TOPK = 256
EPS = 1e-6
N_MOD = 9
LANES = 128
NEG = -1e30
BISECT_STEPS = 14
VMEM_LIMIT = 56 * 1024 * 1024

SM_WI = 64
SM_F = 72


def _cparams(n_axes):
    return pltpu.CompilerParams(
        dimension_semantics=("arbitrary",) * n_axes, vmem_limit_bytes=VMEM_LIMIT)


def _const_spec(shape):
    nd = len(shape)
    return pl.BlockSpec(shape, lambda *_: (0,) * nd)


def _dot(a, b):
    return jnp.dot(a, b, preferred_element_type=F32)


def _rms_adaln(x, gain, scale, shift):
    y = x * lax.rsqrt(jnp.mean(x * x, axis=-1, keepdims=True) + EPS)
    return (y * gain) * (1.0 + scale) + shift


def _mod_kernel(c_ref, w_ref, b_ref, o_ref):
    c = c_ref[...]
    a = c * jax.nn.sigmoid(c)
    o_ref[...] = jnp.dot(a, w_ref[0], preferred_element_type=F32,
                         precision=lax.Precision.HIGHEST) + b_ref[0]


def _adaln_mod(c, ada_w, ada_b, layer):
    b, d = c.shape
    n = ada_w.shape[2]
    rows = 8
    tn = n // 8
    c_pad = jnp.zeros((rows, d), F32).at[:b].set(c)
    out = pl.pallas_call(
        _mod_kernel,
        grid=(n // tn,),
        in_specs=[pl.BlockSpec((rows, d), lambda j: (0, 0)),
                  pl.BlockSpec((1, d, tn), lambda j: (layer, 0, j)),
                  pl.BlockSpec((1, 1, tn), lambda j: (layer, 0, j))],
        out_specs=pl.BlockSpec((rows, tn), lambda j: (0, j)),
        out_shape=jax.ShapeDtypeStruct((rows, n), F32),
        compiler_params=_cparams(1),
        name="adaln_mod",
    )(c_pad, ada_w, ada_b.reshape(ada_b.shape[0], 1, n))
    return out[:b].reshape(b, N_MOD, d)


def _ffn_kernel(x_ref, mod_ref, g_ref, wg_ref, wu_ref, wd_ref, o_ref, *, mod_base, n_chunks):
    x = x_ref[0]
    shift = mod_ref[0, mod_base:mod_base + 1, :]
    scale = mod_ref[0, mod_base + 1:mod_base + 2, :]
    gate = mod_ref[0, mod_base + 2:mod_base + 3, :]
    h = _rms_adaln(x, g_ref[...], scale, shift).astype(BF16)
    dff = wg_ref.shape[1]
    ck = dff // n_chunks
    acc = jnp.zeros(x.shape, F32)
    for i in range(n_chunks):
        g = _dot(h, wg_ref[:, i * ck:(i + 1) * ck])
        u = _dot(h, wu_ref[:, i * ck:(i + 1) * ck])
        a = (g * jax.nn.sigmoid(g) * u).astype(BF16)
        acc = acc + _dot(a, wd_ref[i * ck:(i + 1) * ck, :])
    o_ref[0] = x + (0.5 * gate) * acc


def _ffn(x, mod, gain, wg, wu, wd, *, mod_base, tm=512):
    b, s, d = x.shape
    dff = wg.shape[1]
    kern = functools.partial(_ffn_kernel, mod_base=mod_base, n_chunks=2)
    return pl.pallas_call(
        kern,
        grid=(b, s // tm),
        in_specs=[pl.BlockSpec((1, tm, d), lambda i, j: (i, j, 0)),
                  pl.BlockSpec((1, N_MOD, d), lambda i, j: (i, 0, 0)),
                  _const_spec((1, d)),
                  _const_spec((d, dff)), _const_spec((d, dff)), _const_spec((dff, d))],
        out_specs=pl.BlockSpec((1, tm, d), lambda i, j: (i, j, 0)),
        out_shape=jax.ShapeDtypeStruct((b, s, d), F32),
        compiler_params=_cparams(2),
        name="ffn",
    )(x, mod, gain.reshape(1, d), wg, wu, wd)


def _split3_bf16(v):
    p1 = v.astype(BF16)
    r1 = v - p1.astype(F32)
    p2 = r1.astype(BF16)
    r2 = r1 - p2.astype(F32)
    return p1, p2, r2.astype(BF16)


def _proj_kernel(x_ref, mod_ref, g_ref, wm_ref, ws_ref, bsm_ref, qg_ref, kg_ref,
                 qa_ref, kat_ref, va_ref, qf_ref, kft_ref, vf_ref, qi_ref,
                 small_ref, smallt_ref, kit_ref, carry_ref):
    tm = x_ref.shape[1]
    x = x_ref[0]
    shift = mod_ref[0, 3:4, :]
    scale = mod_ref[0, 4:5, :]
    h = _rms_adaln(x, g_ref[...], scale, shift).astype(BF16)

    def z(i):
        return _dot(h, wm_ref[:, i * WIDTH:(i + 1) * WIDTH])

    r = lax.broadcasted_iota(I32, (WIDTH, WIDTH), 0) // HEAD_DIM
    c = lax.broadcasted_iota(I32, (WIDTH, WIDTH), 1) // HEAD_DIM
    avg = jnp.where(r == c, 1.0 / HEAD_DIM, 0.0).astype(BF16)

    def norm_q(q, gain_row):
        ms = _dot((q * q).astype(BF16), avg)
        return (q * lax.rsqrt(ms + EPS) * gain_row).astype(BF16)

    def norm_kt(k, gain_col):
        kt = k.T.reshape(N_HEADS, HEAD_DIM, tm)
        ms = jnp.mean(kt * kt, axis=1, keepdims=True)
        kt = kt * lax.rsqrt(ms + EPS) * gain_col[None]
        return kt.reshape(WIDTH, tm).astype(BF16)

    qa_ref[0] = norm_q(z(0), qg_ref[0:1, :])
    kat_ref[0] = norm_kt(z(1), kg_ref[:, 0:1])
    va_ref[0] = z(2).astype(BF16)
    qf_ref[0] = norm_q(z(3), qg_ref[1:2, :])
    kft_ref[0] = norm_kt(z(4), kg_ref[:, 1:2])
    vf_ref[0] = z(5).astype(BF16)
    qi_ref[0] = z(6) * (HEAD_DIM ** -0.5)

    zs = _dot(h, ws_ref[...])
    pre = zs + bsm_ref[...]
    logf = jnp.minimum(pre, 0.0) - jnp.log(1.0 + jnp.exp(-jnp.abs(pre)))

    @pl.when(pl.program_id(1) == 0)
    def _():
        carry_ref[...] = jnp.zeros_like(carry_ref)

    ri = lax.broadcasted_iota(I32, (tm, tm), 0)
    ci = lax.broadcasted_iota(I32, (tm, tm), 1)
    tri = jnp.where(ci <= ri, 1.0, 0.0).astype(BF16)
    p1, p2, p3 = _split3_bf16(logf)
    cum = (_dot(tri, p1) + _dot(tri, p2)) + _dot(tri, p3) + carry_ref[...]
    carry_ref[...] = cum[tm - 1:tm, :]

    lane = lax.broadcasted_iota(I32, (tm, LANES), 1)
    small = jnp.where(lane < SM_WI, zs,
                      jnp.where(lane < SM_F, zs * (N_HEADS ** -0.5), cum))
    small_ref[0] = small
    st = small.T
    smallt_ref[0] = st
    ki = st[0:HEAD_DIM, :]
    ki_hi = ki.astype(BF16)
    ki_lo = (ki - ki_hi.astype(F32)).astype(BF16)
    kit_ref[0] = jnp.concatenate([ki_hi, ki_hi, ki_lo], axis=0)


def _in_proj(x, mod, gain, w_main, w_small, b_small, q_gains, k_gains, *, tm=512):
    b, s, d = x.shape
    row = lambda i, j: (i, j, 0)
    col = lambda i, j: (i, 0, j)
    sd = jax.ShapeDtypeStruct
    out_shape = [sd((b, s, WIDTH), BF16), sd((b, WIDTH, s), BF16), sd((b, s, WIDTH), BF16),
                 sd((b, s, WIDTH), BF16), sd((b, WIDTH, s), BF16), sd((b, s, WIDTH), BF16),
                 sd((b, s, WIDTH), F32),
                 sd((b, s, LANES), F32), sd((b, LANES, s), F32), sd((b, 3 * HEAD_DIM, s), BF16)]
    rspec = pl.BlockSpec((1, tm, WIDTH), row)
    cspec = pl.BlockSpec((1, WIDTH, tm), col)
    out_specs = [rspec, cspec, rspec, rspec, cspec, rspec, rspec,
                 pl.BlockSpec((1, tm, LANES), row), pl.BlockSpec((1, LANES, tm), col),
                 pl.BlockSpec((1, 3 * HEAD_DIM, tm), col)]
    return pl.pallas_call(
        _proj_kernel,
        grid=(b, s // tm),
        in_specs=[pl.BlockSpec((1, tm, d), row),
                  pl.BlockSpec((1, N_MOD, d), lambda i, j: (i, 0, 0)),
                  _const_spec((1, d)),
                  _const_spec(w_main.shape), _const_spec(w_small.shape),
                  _const_spec((1, LANES)), _const_spec((2, WIDTH)), _const_spec((HEAD_DIM, 2))],
        out_specs=out_specs,
        out_shape=out_shape,
        scratch_shapes=[pltpu.VMEM((1, LANES), F32)],
        compiler_params=_cparams(2),
        name="in_proj",
    )(x, mod, gain.reshape(1, d), w_main, w_small, b_small, q_gains, k_gains)


FLASH_ROWS = 256
COUNT_ROWS = 128
LOG2E = 1.4426950408889634
SAFE_BOUND = 50.0


def _flash_scratch(tq, tk):
    return [pltpu.VMEM((N_HEADS, tq, LANES), F32), pltpu.VMEM((N_HEADS, tq, LANES), F32),
            pltpu.VMEM((N_HEADS, tq, LANES), F32), pltpu.VMEM((N_HEADS, tq, tk), BF16)]


def _flash_reset(m_ref, l_ref, acc_ref, online):
    if online:
        m_ref[...] = jnp.full(m_ref.shape, NEG, F32)
    l_ref[...] = jnp.zeros(l_ref.shape, F32)
    acc_ref[...] = jnp.zeros(acc_ref.shape, F32)


def _flash_step(q, kt_ref, v_ref, off, chunk_logits, m_ref, l_ref, acc_ref, p_ref, online=True):
    tq, tk = p_ref.shape[1:]
    n_sub = tk // LANES
    for h in range(N_HEADS):
        pair = h // 2
        q_h = q[:, h * HEAD_DIM:(h + 1) * HEAD_DIM]
        kt = kt_ref[0, h * HEAD_DIM:(h + 1) * HEAD_DIM, pl.ds(off, tk)]
        for r0 in range(0, tq, FLASH_ROWS):
            rows = pl.ds(r0, FLASH_ROWS)
            s = _dot(q_h[r0:r0 + FLASH_ROWS], kt)
            sc = [chunk_logits(h, r0, c, s[:, c * LANES:(c + 1) * LANES]) for c in range(n_sub)]
            m_row = m_ref[h, rows, :]
            if online:
                mx = sc[0]
                for x in sc[1:]:
                    mx = jnp.maximum(mx, x)
                m_old = m_row
                m_row = jnp.maximum(m_old, jnp.max(mx, axis=1, keepdims=True))
                alpha = jnp.exp2(m_old - m_row)
                m_ref[h, rows, :] = m_row
                acc_ref[h, rows, :] = alpha * acc_ref[h, rows, :]
            ps = [jnp.exp2(x - m_row) for x in sc]
            lsum = ps[0]
            for x in ps[1:]:
                lsum = lsum + x
            l_old = l_ref[h, rows, :]
            l_ref[h, rows, :] = (alpha * l_old if online else l_old) + lsum
            p_ref[h, rows, :] = jnp.concatenate(ps, axis=1).astype(BF16)
        v = v_ref[0, pl.ds(off, tk), pair * LANES:(pair + 1) * LANES]
        acc_ref[h] = acc_ref[h] + _dot(p_ref[h], v)


def _flash_finish(l_ref, acc_ref, o_ref):
    for pair in range(N_HEADS // 2):
        out = [acc_ref[h] * (1.0 / jnp.sum(l_ref[h], axis=1, keepdims=True))
               for h in (2 * pair, 2 * pair + 1)]
        lane = lax.broadcasted_iota(I32, out[0].shape, 1)
        o_ref[0, :, pair * LANES:(pair + 1) * LANES] = jnp.where(
            lane < HEAD_DIM, out[0], out[1]).astype(BF16)


def _dsa_kernel(bound_ref, qa_ref, qi_ref, small_ref, kat_ref, va_ref, kit_ref, o_ref, sc_ref,
                m_ref, l_ref, acc_ref, p_ref, *, tq, tk, topk):
    s_len = kat_ref.shape[2]
    t0 = pl.program_id(1) * tq
    nkb = (t0 + tq + tk - 1) // tk
    n_sub = tk // LANES
    row = lax.broadcasted_iota(I32, (tq, LANES), 0) + t0
    lane = lax.broadcasted_iota(I32, (COUNT_ROWS, LANES), 1)
    rep = lambda col: jnp.broadcast_to(col, (col.shape[0], LANES))

    wi = small_ref[0][:, SM_WI:SM_WI + N_HEADS]
    qi = qi_ref[0]
    lhs = []
    for h in range(N_HEADS):
        qh = qi[:, h * HEAD_DIM:(h + 1) * HEAD_DIM]
        hi = qh.astype(BF16)
        lo = (qh - hi.astype(F32)).astype(BF16)
        lhs.append(jnp.concatenate([hi, lo, hi], axis=1))
    lane_q = lax.broadcasted_iota(I32, (tq, LANES), 1)
    for h in range(N_HEADS):
        acc_ref[h] = jnp.broadcast_to(wi[:, h:h + 1], (tq, LANES))
    m_ref[0] = jnp.full((tq, LANES), jnp.inf, F32)
    m_ref[1] = jnp.full((tq, LANES), -jnp.inf, F32)
    m_ref[2] = jnp.full((tq, LANES), -jnp.inf, F32)

    def score_body(kb, _):
        off = pl.multiple_of(kb * tk, tk)
        kib = kit_ref[0, :, pl.ds(off, tk)]
        acc = [jnp.zeros((tq, LANES), F32) for _ in range(n_sub)]
        for h in range(N_HEADS):
            r = _dot(lhs[h], kib)
            w = acc_ref[h]
            for i in range(n_sub):
                acc[i] = acc[i] + jnp.maximum(r[:, i * LANES:(i + 1) * LANES], 0.0) * w
        mn, top1, top2 = m_ref[0], m_ref[1], m_ref[2]
        for i in range(n_sub):
            causal = lane_q + (off + i * LANES) <= row
            lowered = jnp.where(causal, acc[i], -jnp.inf)
            sc_ref[kb * n_sub + i] = lowered
            mn = jnp.minimum(mn, jnp.where(causal, acc[i], jnp.inf))
            top2 = jnp.maximum(top2, jnp.minimum(top1, lowered))
            top1 = jnp.maximum(top1, lowered)
        m_ref[0], m_ref[1], m_ref[2] = mn, top1, top2
        return 0

    lax.fori_loop(0, nkb, score_body, 0)
    mn, top2 = m_ref[0], m_ref[2]

    slabs = range(0, tq, COUNT_ROWS)

    def scan(step, init, *row_args, settled=None):
        outs = []
        for k, r0 in enumerate(slabs):
            args = [a[r0:r0 + COUNT_ROWS] for a in row_args]
            start = jnp.full((COUNT_ROWS, LANES), init, F32)

            def body(kb, acc, r0=r0, args=args):
                off = pl.multiple_of(kb * tk, tk)
                for i in range(n_sub):
                    blk = sc_ref[kb * n_sub + i, pl.ds(r0, COUNT_ROWS), :]
                    acc = step(acc, blk, lane + (off + i * LANES), *args)
                return acc

            run = functools.partial(lax.fori_loop, 0, nkb, body, start)
            outs.append(run() if settled is None
                        else lax.cond(settled[k] > 0.5, lambda start=start: start, run))
        return jnp.concatenate(outs, axis=0)

    def count(pred, *row_args, settled=None):
        c = scan(lambda acc, blk, pos, *a: acc + jnp.where(pred(blk, pos, *a), 1.0, 0.0),
                 0.0, *row_args, settled=settled)
        return rep(jnp.sum(c, axis=1, keepdims=True))

    def below_max(bound, settled):
        m = scan(lambda acc, blk, pos, b: jnp.maximum(acc, jnp.where(blk < b, blk, -jnp.inf)),
                 -jnp.inf, bound, settled=settled)
        return rep(jnp.max(m, axis=1, keepdims=True))

    def slab_settled(state):
        return tuple(jnp.min(state[r0:r0 + COUNT_ROWS]) for r0 in slabs)

    kf = float(topk)
    few = row < topk
    rmin = rep(jnp.min(mn, axis=1, keepdims=True))
    run_lo = rep(jnp.min(top2, axis=1, keepdims=True))
    run_hi = rep(jnp.max(top2, axis=1, keepdims=True))
    above = jnp.where(run_hi > 0.0, run_hi * (1.0 + 1e-6), run_hi * (1.0 - 1e-6)) + 1e-30
    lo0 = jnp.where(few, 0.0, jnp.maximum(run_lo, rmin))
    hi0 = jnp.where(few, 0.0, above)
    state0 = jnp.where(few, 1.0, 0.0)

    def search_body(_, st):
        lo, hi, state = st
        mid = 0.5 * lo + 0.5 * hi
        cnt = count(lambda blk, pos, m: blk >= m, mid)
        active = state == 0.0
        ge = cnt >= kf
        lo = jnp.where(active, jnp.where(ge, mid, lo), lo)
        hi = jnp.where(active, jnp.where(ge, hi, mid), hi)
        state = jnp.where(active, jnp.where(cnt == kf, 1.0, 0.0), state)
        return lo, hi, state

    lo, hi, state = lax.fori_loop(0, BISECT_STEPS, search_body, (lo0, hi0, state0))
    settled = slab_settled(state)
    thr = jnp.where(few, -jnp.inf, lo)

    def snap_body(st):
        settled, hi, thr, state = st
        cand = below_max(hi, settled)
        cnt = count(lambda blk, pos, c: blk >= c, cand, settled=settled)
        active = state == 0.0
        found = cnt >= kf
        thr = jnp.where(active, jnp.where(found, cand, thr), thr)
        hi = jnp.where(active, jnp.where(found, hi, cand), hi)
        state = jnp.where(active, jnp.where(found, jnp.where(cnt == kf, 1.0, 2.0), 0.0), state)
        return slab_settled(state), hi, thr, state

    _, _, thr, state = lax.while_loop(
        lambda st: functools.reduce(jnp.minimum, st[0]) < 0.5, snap_body, (settled, hi, thr, state))

    last = []
    for r0 in slabs:
        rows = pl.ds(r0, COUNT_ROWS)
        thr_r, row_r, state_r = (a[r0:r0 + COUNT_ROWS] for a in (thr, row, state))
        far0 = jnp.full((COUNT_ROWS, LANES), -1, I32)

        def mark(kb, i, keep, far, rows=rows, row_r=row_r):
            pos = lane + (kb * tk + i * LANES)
            keep = keep & (pos <= row_r)
            sc_ref[kb * n_sub + i, rows, :] = jnp.where(keep, 0.0, NEG)
            return jnp.maximum(far, jnp.where(keep, pos, -1))

        def plain(rows=rows, thr_r=thr_r, mark=mark, far0=far0):
            def body(kb, far):
                for i in range(n_sub):
                    far = mark(kb, i, sc_ref[kb * n_sub + i, rows, :] >= thr_r, far)
                return far
            return lax.fori_loop(0, nkb, body, far0)

        def tied(rows=rows, thr_r=thr_r, mark=mark, far0=far0):
            def above(kb, c):
                for i in range(n_sub):
                    c = c + jnp.where(sc_ref[kb * n_sub + i, rows, :] > thr_r, 1.0, 0.0)
                return c
            c = lax.fori_loop(0, nkb, above, jnp.zeros((COUNT_ROWS, LANES), F32))
            need = kf - rep(jnp.sum(c, axis=1, keepdims=True))
            tri = jnp.where(lax.broadcasted_iota(I32, (tk, tk), 0) <= lax.broadcasted_iota(I32, (tk, tk), 1),
                            1.0, 0.0).astype(BF16)

            def body(kb, carry):
                far, seen = carry
                blks = [sc_ref[kb * n_sub + i, rows, :] for i in range(n_sub)]
                is_tie = jnp.concatenate([jnp.where(blk == thr_r, 1.0, 0.0) for blk in blks], axis=1)
                rank = _dot(is_tie.astype(BF16), tri)
                for i, blk in enumerate(blks):
                    rank_i = seen + rank[:, i * LANES:(i + 1) * LANES]
                    far = mark(kb, i, (blk > thr_r) | ((blk == thr_r) & (rank_i <= need)), far)
                return far, seen + rep(rank[:, tk - 1:tk])
            return lax.fori_loop(0, nkb, body, (far0, jnp.zeros((COUNT_ROWS, LANES), F32)))[0]

        far = lax.cond(jnp.max(state_r) > 1.5, tied, plain)
        last.append(rep(jnp.max(far.astype(F32), axis=1, keepdims=True)))
    last = jnp.concatenate(last, axis=0)

    qa = qa_ref[0]
    kpos = lax.broadcasted_iota(I32, (1, tk), 1)
    slopes = [LOG2E * 2.0 ** (-8.0 * (h + 1) / N_HEADS) for h in range(N_HEADS)]

    def attend(online):
        def attn_body(kb, _):
            off = pl.multiple_of(kb * tk, tk)
            rel = (kpos + (off - t0)).astype(F32)

            def chunk_logits(h, r0, c, x):
                mask = sc_ref[kb * n_sub + c, pl.ds(r0, FLASH_ROWS), :]
                return x + (mask + slopes[h] * rel[:, c * LANES:(c + 1) * LANES])

            _flash_step(qa, kat_ref, va_ref, off, chunk_logits, m_ref, l_ref, acc_ref, p_ref, online)
            return 0

        lax.fori_loop(0, nkb, attn_body, 0)
        _flash_finish(l_ref, acc_ref, o_ref)

    bound = bound_ref[0, 0]

    @pl.when(bound <= SAFE_BOUND)
    def _():
        _flash_reset(m_ref, l_ref, acc_ref, online=False)
        for h in range(N_HEADS):
            m_ref[h] = bound + slopes[h] * (last - t0.astype(F32))
        attend(online=False)

    @pl.when(bound > SAFE_BOUND)
    def _():
        _flash_reset(m_ref, l_ref, acc_ref, online=True)
        attend(online=True)


def _dsa_attn(bound, qa, qi, small, kat, va, kit, *, tq=512, tk=512):
    b, s, _ = qa.shape
    row = lambda i, j: (i, j, 0)
    whole = lambda i, j: (i, 0, 0)
    once = pl.Buffered(1)
    topk = min(TOPK, s // 4)
    assert topk > LANES and s % tq == 0 and s % tk == 0, (s, tq, tk)
    kern = functools.partial(_dsa_kernel, tq=tq, tk=tk, topk=topk)
    return pl.pallas_call(
        kern,
        grid=(b, s // tq),
        in_specs=[pl.BlockSpec(memory_space=pltpu.SMEM),
                  pl.BlockSpec((1, tq, WIDTH), row), pl.BlockSpec((1, tq, WIDTH), row),
                  pl.BlockSpec((1, tq, LANES), row),
                  pl.BlockSpec((1, WIDTH, s), whole, pipeline_mode=once),
                  pl.BlockSpec((1, s, WIDTH), whole, pipeline_mode=once),
                  pl.BlockSpec((1, 3 * HEAD_DIM, s), whole, pipeline_mode=once)],
        out_specs=pl.BlockSpec((1, tq, WIDTH), row),
        out_shape=jax.ShapeDtypeStruct((b, s, WIDTH), BF16),
        scratch_shapes=[pltpu.VMEM((s // LANES, tq, LANES), F32)] + _flash_scratch(tq, tk),
        compiler_params=_cparams(2),
        name="dsa_attn",
    )(bound, qa, qi, small, kat, va, kit)


def _fox_kernel(bound_ref, qf_ref, small_ref, smallt_ref, kft_ref, vf_ref, o_ref,
                m_ref, l_ref, acc_ref, p_ref, *, tq, tk):
    t0 = pl.program_id(1) * tq
    n_full = t0 // tk
    nkb = (t0 + tq + tk - 1) // tk
    row = lax.broadcasted_iota(I32, (FLASH_ROWS, LANES), 0) + t0
    col = lax.broadcasted_iota(I32, (FLASH_ROWS, LANES), 1)
    qf = qf_ref[0]
    f_t0 = smallt_ref[0, SM_F:SM_F + N_HEADS, pl.ds(pl.multiple_of(t0, LANES), LANES)][:, 0:1]

    def attend(online):
        def attn_body(kb, _, causal):
            off = pl.multiple_of(kb * tk, tk)
            decay = LOG2E * (f_t0 - smallt_ref[0, SM_F:SM_F + N_HEADS, pl.ds(off, tk)])

            def chunk_logits(h, r0, c, x):
                x = x + decay[h:h + 1, c * LANES:(c + 1) * LANES]
                if causal:
                    x = jnp.where(col + (off + c * LANES) <= row + r0, x, NEG)
                return x

            _flash_step(qf, kft_ref, vf_ref, off, chunk_logits, m_ref, l_ref, acc_ref, p_ref, online)
            return 0

        lax.fori_loop(0, n_full, functools.partial(attn_body, causal=False), 0)
        lax.fori_loop(n_full, nkb, functools.partial(attn_body, causal=True), 0)
        _flash_finish(l_ref, acc_ref, o_ref)

    bound = bound_ref[0, 0]

    @pl.when(bound <= SAFE_BOUND)
    def _():
        _flash_reset(m_ref, l_ref, acc_ref, online=False)
        f_t = small_ref[0][:, SM_F:SM_F + N_HEADS]
        for h in range(N_HEADS):
            top = bound + LOG2E * (f_t0[h:h + 1, :] - f_t[:, h:h + 1])
            m_ref[h] = jnp.broadcast_to(top, (tq, LANES))
        attend(online=False)

    @pl.when(bound > SAFE_BOUND)
    def _():
        _flash_reset(m_ref, l_ref, acc_ref, online=True)
        attend(online=True)


def _fox_attn(bound, qf, small, smallt, kft, vf, *, tq=512, tk=512):
    b, s, _ = qf.shape
    row = lambda i, j: (i, j, 0)
    whole = lambda i, j: (i, 0, 0)
    once = pl.Buffered(1)
    kern = functools.partial(_fox_kernel, tq=tq, tk=tk)
    return pl.pallas_call(
        kern,
        grid=(b, s // tq),
        in_specs=[pl.BlockSpec(memory_space=pltpu.SMEM),
                  pl.BlockSpec((1, tq, WIDTH), row), pl.BlockSpec((1, tq, LANES), row),
                  pl.BlockSpec((1, LANES, s), whole, pipeline_mode=once),
                  pl.BlockSpec((1, WIDTH, s), whole, pipeline_mode=once),
                  pl.BlockSpec((1, s, WIDTH), whole, pipeline_mode=once)],
        out_specs=pl.BlockSpec((1, tq, WIDTH), row),
        out_shape=jax.ShapeDtypeStruct((b, s, WIDTH), BF16),
        scratch_shapes=_flash_scratch(tq, tk),
        compiler_params=_cparams(2),
        name="fox_attn",
    )(bound, qf, small, smallt, kft, vf)


def _mix_kernel(x_ref, mod_ref, g_ref, ad_ref, af_ref, wgate_ref, wbd_ref, wbf_ref, wo_ref, o_ref):
    d = x_ref.shape[2]
    x = x_ref[0]
    shift = mod_ref[0, 3:4, :]
    scale = mod_ref[0, 4:5, :]
    gate = mod_ref[0, 5:6, :]
    h = _rms_adaln(x, g_ref[...], scale, shift).astype(BF16)
    ga = _dot(h, wgate_ref[:, 0:d])
    gb = _dot(h, wgate_ref[:, d:2 * d])
    y_dsa = _dot(ad_ref[0], wbd_ref[...])
    y_fox = _dot(af_ref[0], wbf_ref[...])
    merged = jax.nn.sigmoid(ga) * y_dsa + jax.nn.sigmoid(gb) * y_fox
    o_ref[0] = x + gate * _dot(merged.astype(BF16), wo_ref[...])


def _mix_out(x, mod, gain, a_dsa, a_fox, w_gate, w_br_dsa, w_br_fox, w_out, *, tm=512):
    b, s, d = x.shape
    row = lambda i, j: (i, j, 0)
    return pl.pallas_call(
        _mix_kernel,
        grid=(b, s // tm),
        in_specs=[pl.BlockSpec((1, tm, d), row),
                  pl.BlockSpec((1, N_MOD, d), lambda i, j: (i, 0, 0)),
                  _const_spec((1, d)),
                  pl.BlockSpec((1, tm, WIDTH), row), pl.BlockSpec((1, tm, WIDTH), row),
                  _const_spec(w_gate.shape), _const_spec(w_br_dsa.shape),
                  _const_spec(w_br_fox.shape), _const_spec(w_out.shape)],
        out_specs=pl.BlockSpec((1, tm, d), row),
        out_shape=jax.ShapeDtypeStruct((b, s, d), F32),
        compiler_params=_cparams(2),
        name="mix_out",
    )(x, mod, gain.reshape(1, d), a_dsa, a_fox, w_gate, w_br_dsa, w_br_fox, w_out)


def _layer(x, mod, norm1_g, ffn1_wg, ffn1_wu, ffn1_wd, norm2_g, w_in, b_forget,
           qn_dsa, kn_dsa, qn_fox, kn_fox, w_br_dsa, w_br_fox, w_out, norm3_g,
           ffn2_wg, ffn2_wu, ffn2_wd):
    d = x.shape[2]
    bf = lambda w: w.astype(BF16)
    x = _ffn(x, mod, norm1_g, bf(ffn1_wg), bf(ffn1_wu), bf(ffn1_wd), mod_base=0)

    n_main = 7 * WIDTH
    n_small = HEAD_DIM + 2 * N_HEADS
    w_main = bf(w_in[:, :n_main])
    w_small = jnp.zeros((d, LANES), BF16).at[:, :n_small].set(bf(w_in[:, n_main:n_main + n_small]))
    w_gate = bf(w_in[:, n_main + n_small:])
    b_small = jnp.zeros((1, LANES), F32).at[0, SM_F:SM_F + N_HEADS].set(b_forget)
    q_scale = LOG2E * HEAD_DIM ** -0.5
    q_gains = jnp.stack([jnp.tile(qn_dsa, N_HEADS), jnp.tile(qn_fox, N_HEADS)]) * q_scale
    k_gains = jnp.stack([kn_dsa, kn_fox], axis=1)

    qa, kat, va, qf, kft, vf, qi, small, smallt, kit = _in_proj(
        x, mod, norm2_g, w_main, w_small, b_small, q_gains, k_gains)
    def logit_bound(qn, kn):
        b = 1.05 * HEAD_DIM * q_scale * jnp.max(jnp.abs(qn)) * jnp.max(jnp.abs(kn))
        return b.reshape(1, 1).astype(F32)

    a_dsa = _dsa_attn(logit_bound(qn_dsa, kn_dsa), qa, qi, small, kat, va, kit)
    a_fox = _fox_attn(logit_bound(qn_fox, kn_fox), qf, small, smallt, kft, vf)
    x = _mix_out(x, mod, norm2_g, a_dsa, a_fox, w_gate, bf(w_br_dsa), bf(w_br_fox), bf(w_out))
    return _ffn(x, mod, norm3_g, bf(ffn2_wg), bf(ffn2_wu), bf(ffn2_wd), mod_base=6)


def kernel(x, c, ada_w, ada_b, norm1_g, ffn1_wg, ffn1_wu, ffn1_wd, norm2_g, w_in, b_forget,
           qn_dsa, kn_dsa, qn_fox, kn_fox, w_br_dsa, w_br_fox, w_out, norm3_g,
           ffn2_wg, ffn2_wu, ffn2_wd):
    per_layer = (norm1_g, ffn1_wg, ffn1_wu, ffn1_wd, norm2_g, w_in, b_forget,
                 qn_dsa, kn_dsa, qn_fox, kn_fox, w_br_dsa, w_br_fox, w_out, norm3_g,
                 ffn2_wg, ffn2_wu, ffn2_wd)
    for l in range(ada_w.shape[0]):
        mod = _adaln_mod(c, ada_w, ada_b, l)
        x = _layer(x, mod, *(p[l] for p in per_layer))
    return x
```

```python
import functools

import jax
import jax.numpy as jnp
from jax import lax
from jax.experimental import pallas as pl
from jax.experimental.pallas import tpu as pltpu

F32 = jnp.float32
BF16 = jnp.bfloat16
I32 = jnp.int32

HEAD_DIM = 64
N_HEADS = 8
WIDTH = N_HEADS * HEAD_DIM
TOPK = 256
EPS = 1e-6
N_MOD = 9
LANES = 128
NEG = -1e30
BISECT_STEPS = 14
VMEM_LIMIT = 56 * 1024 * 1024

SM_WI = 64
SM_F = 72


def _cparams(n_axes):
    return pltpu.CompilerParams(
        dimension_semantics=("arbitrary",) * n_axes, vmem_limit_bytes=VMEM_LIMIT)


def _const_spec(shape):
    nd = len(shape)
    return pl.BlockSpec(shape, lambda *_: (0,) * nd)


def _dot(a, b):
    return jnp.dot(a, b, preferred_element_type=F32)


def _rms_adaln(x, gain, scale, shift):
    y = x * lax.rsqrt(jnp.mean(x * x, axis=-1, keepdims=True) + EPS)
    return (y * gain) * (1.0 + scale) + shift


def _mod_kernel(c_ref, w_ref, b_ref, o_ref):
    c = c_ref[...]
    a = c * jax.nn.sigmoid(c)
    o_ref[...] = jnp.dot(a, w_ref[0], preferred_element_type=F32,
                         precision=lax.Precision.HIGHEST) + b_ref[0]


def _adaln_mod(c, ada_w, ada_b, layer):
    b, d = c.shape
    n = ada_w.shape[2]
    rows = 8
    tn = n // 8
    c_pad = jnp.zeros((rows, d), F32).at[:b].set(c)
    out = pl.pallas_call(
        _mod_kernel,
        grid=(n // tn,),
        in_specs=[pl.BlockSpec((rows, d), lambda j: (0, 0)),
                  pl.BlockSpec((1, d, tn), lambda j: (layer, 0, j)),
                  pl.BlockSpec((1, 1, tn), lambda j: (layer, 0, j))],
        out_specs=pl.BlockSpec((rows, tn), lambda j: (0, j)),
        out_shape=jax.ShapeDtypeStruct((rows, n), F32),
        compiler_params=_cparams(1),
        name="adaln_mod",
    )(c_pad, ada_w, ada_b.reshape(ada_b.shape[0], 1, n))
    return out[:b].reshape(b, N_MOD, d)


def _ffn_kernel(x_ref, mod_ref, g_ref, wg_ref, wu_ref, wd_ref, o_ref, *, mod_base, n_chunks):
    x = x_ref[0]
    shift = mod_ref[0, mod_base:mod_base + 1, :]
    scale = mod_ref[0, mod_base + 1:mod_base + 2, :]
    gate = mod_ref[0, mod_base + 2:mod_base + 3, :]
    h = _rms_adaln(x, g_ref[...], scale, shift).astype(BF16)
    dff = wg_ref.shape[1]
    ck = dff // n_chunks
    acc = jnp.zeros(x.shape, F32)
    for i in range(n_chunks):
        g = _dot(h, wg_ref[:, i * ck:(i + 1) * ck])
        u = _dot(h, wu_ref[:, i * ck:(i + 1) * ck])
        a = (g * jax.nn.sigmoid(g) * u).astype(BF16)
        acc = acc + _dot(a, wd_ref[i * ck:(i + 1) * ck, :])
    o_ref[0] = x + (0.5 * gate) * acc


def _ffn(x, mod, gain, wg, wu, wd, *, mod_base, tm=512):
    b, s, d = x.shape
    dff = wg.shape[1]
    kern = functools.partial(_ffn_kernel, mod_base=mod_base, n_chunks=2)
    return pl.pallas_call(
        kern,
        grid=(b, s // tm),
        in_specs=[pl.BlockSpec((1, tm, d), lambda i, j: (i, j, 0)),
                  pl.BlockSpec((1, N_MOD, d), lambda i, j: (i, 0, 0)),
                  _const_spec((1, d)),
                  _const_spec((d, dff)), _const_spec((d, dff)), _const_spec((dff, d))],
        out_specs=pl.BlockSpec((1, tm, d), lambda i, j: (i, j, 0)),
        out_shape=jax.ShapeDtypeStruct((b, s, d), F32),
        compiler_params=_cparams(2),
        name="ffn",
    )(x, mod, gain.reshape(1, d), wg, wu, wd)


def _split3_bf16(v):
    p1 = v.astype(BF16)
    r1 = v - p1.astype(F32)
    p2 = r1.astype(BF16)
    r2 = r1 - p2.astype(F32)
    return p1, p2, r2.astype(BF16)


def _proj_kernel(x_ref, mod_ref, g_ref, wm_ref, ws_ref, bsm_ref, qg_ref, kg_ref,
                 qa_ref, kat_ref, va_ref, qf_ref, kft_ref, vf_ref, qi_ref,
                 small_ref, smallt_ref, kit_ref, carry_ref):
    tm = x_ref.shape[1]
    x = x_ref[0]
    shift = mod_ref[0, 3:4, :]
    scale = mod_ref[0, 4:5, :]
    h = _rms_adaln(x, g_ref[...], scale, shift).astype(BF16)

    def z(i):
        return _dot(h, wm_ref[:, i * WIDTH:(i + 1) * WIDTH])

    r = lax.broadcasted_iota(I32, (WIDTH, WIDTH), 0) // HEAD_DIM
    c = lax.broadcasted_iota(I32, (WIDTH, WIDTH), 1) // HEAD_DIM
    avg = jnp.where(r == c, 1.0 / HEAD_DIM, 0.0).astype(BF16)

    def norm_q(q, gain_row):
        ms = _dot((q * q).astype(BF16), avg)
        return (q * lax.rsqrt(ms + EPS) * gain_row).astype(BF16)

    def norm_kt(k, gain_col):
        kt = k.T.reshape(N_HEADS, HEAD_DIM, tm)
        ms = jnp.mean(kt * kt, axis=1, keepdims=True)
        kt = kt * lax.rsqrt(ms + EPS) * gain_col[None]
        return kt.reshape(WIDTH, tm).astype(BF16)

    qa_ref[0] = norm_q(z(0), qg_ref[0:1, :])
    kat_ref[0] = norm_kt(z(1), kg_ref[:, 0:1])
    va_ref[0] = z(2).astype(BF16)
    qf_ref[0] = norm_q(z(3), qg_ref[1:2, :])
    kft_ref[0] = norm_kt(z(4), kg_ref[:, 1:2])
    vf_ref[0] = z(5).astype(BF16)
    qi_ref[0] = z(6) * (HEAD_DIM ** -0.5)

    zs = _dot(h, ws_ref[...])
    pre = zs + bsm_ref[...]
    logf = jnp.minimum(pre, 0.0) - jnp.log(1.0 + jnp.exp(-jnp.abs(pre)))

    @pl.when(pl.program_id(1) == 0)
    def _():
        carry_ref[...] = jnp.zeros_like(carry_ref)

    ri = lax.broadcasted_iota(I32, (tm, tm), 0)
    ci = lax.broadcasted_iota(I32, (tm, tm), 1)
    tri = jnp.where(ci <= ri, 1.0, 0.0).astype(BF16)
    p1, p2, p3 = _split3_bf16(logf)
    cum = (_dot(tri, p1) + _dot(tri, p2)) + _dot(tri, p3) + carry_ref[...]
    carry_ref[...] = cum[tm - 1:tm, :]

    lane = lax.broadcasted_iota(I32, (tm, LANES), 1)
    small = jnp.where(lane < SM_WI, zs,
                      jnp.where(lane < SM_F, zs * (N_HEADS ** -0.5), cum))
    small_ref[0] = small
    st = small.T
    smallt_ref[0] = st
    ki = st[0:HEAD_DIM, :]
    ki_hi = ki.astype(BF16)
    ki_lo = (ki - ki_hi.astype(F32)).astype(BF16)
    kit_ref[0] = jnp.concatenate([ki_hi, ki_hi, ki_lo], axis=0)


def _in_proj(x, mod, gain, w_main, w_small, b_small, q_gains, k_gains, *, tm=512):
    b, s, d = x.shape
    row = lambda i, j: (i, j, 0)
    col = lambda i, j: (i, 0, j)
    sd = jax.ShapeDtypeStruct
    out_shape = [sd((b, s, WIDTH), BF16), sd((b, WIDTH, s), BF16), sd((b, s, WIDTH), BF16),
                 sd((b, s, WIDTH), BF16), sd((b, WIDTH, s), BF16), sd((b, s, WIDTH), BF16),
                 sd((b, s, WIDTH), F32),
                 sd((b, s, LANES), F32), sd((b, LANES, s), F32), sd((b, 3 * HEAD_DIM, s), BF16)]
    rspec = pl.BlockSpec((1, tm, WIDTH), row)
    cspec = pl.BlockSpec((1, WIDTH, tm), col)
    out_specs = [rspec, cspec, rspec, rspec, cspec, rspec, rspec,
                 pl.BlockSpec((1, tm, LANES), row), pl.BlockSpec((1, LANES, tm), col),
                 pl.BlockSpec((1, 3 * HEAD_DIM, tm), col)]
    return pl.pallas_call(
        _proj_kernel,
        grid=(b, s // tm),
        in_specs=[pl.BlockSpec((1, tm, d), row),
                  pl.BlockSpec((1, N_MOD, d), lambda i, j: (i, 0, 0)),
                  _const_spec((1, d)),
                  _const_spec(w_main.shape), _const_spec(w_small.shape),
                  _const_spec((1, LANES)), _const_spec((2, WIDTH)), _const_spec((HEAD_DIM, 2))],
        out_specs=out_specs,
        out_shape=out_shape,
        scratch_shapes=[pltpu.VMEM((1, LANES), F32)],
        compiler_params=_cparams(2),
        name="in_proj",
    )(x, mod, gain.reshape(1, d), w_main, w_small, b_small, q_gains, k_gains)


FLASH_ROWS = 128
COUNT_ROWS = 128
LOG2E = 1.4426950408889634
SAFE_BOUND = 50.0


def _flash_scratch(tq, tk):
    return [pltpu.VMEM((N_HEADS, tq, LANES), F32), pltpu.VMEM((N_HEADS, tq, LANES), F32),
            pltpu.VMEM((N_HEADS, tq, LANES), F32), pltpu.VMEM((N_HEADS, tq, tk), BF16)]


def _flash_reset(m_ref, l_ref, acc_ref, online):
    if online:
        m_ref[...] = jnp.full(m_ref.shape, NEG, F32)
    l_ref[...] = jnp.zeros(l_ref.shape, F32)
    acc_ref[...] = jnp.zeros(acc_ref.shape, F32)


def _flash_step(q, kt_ref, v_ref, off, chunk_logits, m_ref, l_ref, acc_ref, p_ref, online=True):
    tq, tk = p_ref.shape[1:]
    n_sub = tk // LANES
    for h in range(N_HEADS):
        pair = h // 2
        q_h = q[:, h * HEAD_DIM:(h + 1) * HEAD_DIM]
        kt = kt_ref[0, h * HEAD_DIM:(h + 1) * HEAD_DIM, pl.ds(off, tk)]
        for r0 in range(0, tq, FLASH_ROWS):
            rows = pl.ds(r0, FLASH_ROWS)
            s = _dot(q_h[r0:r0 + FLASH_ROWS], kt)
            sc = [chunk_logits(h, r0, c, s[:, c * LANES:(c + 1) * LANES]) for c in range(n_sub)]
            m_row = m_ref[h, rows, :]
            if online:
                mx = sc[0]
                for x in sc[1:]:
                    mx = jnp.maximum(mx, x)
                m_old = m_row
                m_row = jnp.maximum(m_old, jnp.max(mx, axis=1, keepdims=True))
                alpha = jnp.exp2(m_old - m_row)
                m_ref[h, rows, :] = m_row
                acc_ref[h, rows, :] = alpha * acc_ref[h, rows, :]
            ps = [jnp.exp2(x - m_row) for x in sc]
            lsum = ps[0]
            for x in ps[1:]:
                lsum = lsum + x
            l_old = l_ref[h, rows, :]
            l_ref[h, rows, :] = (alpha * l_old if online else l_old) + lsum
            p_ref[h, rows, :] = jnp.concatenate(ps, axis=1).astype(BF16)
        v = v_ref[0, pl.ds(off, tk), pair * LANES:(pair + 1) * LANES]
        acc_ref[h] = acc_ref[h] + _dot(p_ref[h], v)


def _flash_finish(l_ref, acc_ref, o_ref):
    for pair in range(N_HEADS // 2):
        out = [acc_ref[h] * (1.0 / jnp.sum(l_ref[h], axis=1, keepdims=True))
               for h in (2 * pair, 2 * pair + 1)]
        lane = lax.broadcasted_iota(I32, out[0].shape, 1)
        o_ref[0, :, pair * LANES:(pair + 1) * LANES] = jnp.where(
            lane < HEAD_DIM, out[0], out[1]).astype(BF16)


def _dsa_kernel(bound_ref, qa_ref, qi_ref, small_ref, kat_ref, va_ref, kit_ref, o_ref, sc_ref,
                m_ref, l_ref, acc_ref, p_ref, *, tq, tk, topk):
    s_len = kat_ref.shape[2]
    t0 = pl.program_id(1) * tq
    nkb = (t0 + tq + tk - 1) // tk
    n_sub = tk // LANES
    row = lax.broadcasted_iota(I32, (tq, LANES), 0) + t0
    lane = lax.broadcasted_iota(I32, (COUNT_ROWS, LANES), 1)
    rep = lambda col: jnp.broadcast_to(col, (col.shape[0], LANES))

    wi = small_ref[0][:, SM_WI:SM_WI + N_HEADS]
    qi = qi_ref[0]
    lhs = []
    for h in range(N_HEADS):
        qh = qi[:, h * HEAD_DIM:(h + 1) * HEAD_DIM]
        hi = qh.astype(BF16)
        lo = (qh - hi.astype(F32)).astype(BF16)
        lhs.append(jnp.concatenate([hi, lo, hi], axis=1))
    lane_q = lax.broadcasted_iota(I32, (tq, LANES), 1)
    for h in range(N_HEADS):
        acc_ref[h] = jnp.broadcast_to(wi[:, h:h + 1], (tq, LANES))
    m_ref[0] = jnp.full((tq, LANES), jnp.inf, F32)
    m_ref[1] = jnp.full((tq, LANES), -jnp.inf, F32)
    m_ref[2] = jnp.full((tq, LANES), -jnp.inf, F32)

    def score_body(kb, _):
        off = pl.multiple_of(kb * tk, tk)
        kib = kit_ref[0, :, pl.ds(off, tk)]
        acc = [jnp.zeros((tq, LANES), F32) for _ in range(n_sub)]
        for h in range(N_HEADS):
            r = _dot(lhs[h], kib)
            w = acc_ref[h]
            for i in range(n_sub):
                acc[i] = acc[i] + jnp.maximum(r[:, i * LANES:(i + 1) * LANES], 0.0) * w
        mn, top1, top2 = m_ref[0], m_ref[1], m_ref[2]
        for i in range(n_sub):
            causal = lane_q + (off + i * LANES) <= row
            lowered = jnp.where(causal, acc[i], -jnp.inf)
            sc_ref[kb * n_sub + i] = lowered
            mn = jnp.minimum(mn, jnp.where(causal, acc[i], jnp.inf))
            top2 = jnp.maximum(top2, jnp.minimum(top1, lowered))
            top1 = jnp.maximum(top1, lowered)
        m_ref[0], m_ref[1], m_ref[2] = mn, top1, top2
        return 0

    lax.fori_loop(0, nkb, score_body, 0)
    mn, top2 = m_ref[0], m_ref[2]

    slabs = range(0, tq, COUNT_ROWS)

    def scan(step, init, *row_args, settled=None):
        outs = []
        for k, r0 in enumerate(slabs):
            args = [a[r0:r0 + COUNT_ROWS] for a in row_args]
            start = jnp.full((COUNT_ROWS, LANES), init, F32)

            def body(kb, acc, r0=r0, args=args):
                off = pl.multiple_of(kb * tk, tk)
                for i in range(n_sub):
                    blk = sc_ref[kb * n_sub + i, pl.ds(r0, COUNT_ROWS), :]
                    acc = step(acc, blk, lane + (off + i * LANES), *args)
                return acc

            run = functools.partial(lax.fori_loop, 0, nkb, body, start)
            outs.append(run() if settled is None
                        else lax.cond(settled[k] > 0.5, lambda start=start: start, run))
        return jnp.concatenate(outs, axis=0)

    def count(pred, *row_args, settled=None):
        c = scan(lambda acc, blk, pos, *a: acc + jnp.where(pred(blk, pos, *a), 1.0, 0.0),
                 0.0, *row_args, settled=settled)
        return rep(jnp.sum(c, axis=1, keepdims=True))

    def below_max(bound, settled):
        m = scan(lambda acc, blk, pos, b: jnp.maximum(acc, jnp.where(blk < b, blk, -jnp.inf)),
                 -jnp.inf, bound, settled=settled)
        return rep(jnp.max(m, axis=1, keepdims=True))

    def slab_settled(state):
        return tuple(jnp.min(state[r0:r0 + COUNT_ROWS]) for r0 in slabs)

    kf = float(topk)
    few = row < topk
    rmin = rep(jnp.min(mn, axis=1, keepdims=True))
    run_lo = rep(jnp.min(top2, axis=1, keepdims=True))
    run_hi = rep(jnp.max(top2, axis=1, keepdims=True))
    above = jnp.where(run_hi > 0.0, run_hi * (1.0 + 1e-6), run_hi * (1.0 - 1e-6)) + 1e-30
    lo0 = jnp.where(few, 0.0, jnp.maximum(run_lo, rmin))
    hi0 = jnp.where(few, 0.0, above)
    state0 = jnp.where(few, 1.0, 0.0)

    def search_body(_, st):
        lo, hi, state = st
        mid = 0.5 * lo + 0.5 * hi
        cnt = count(lambda blk, pos, m: blk >= m, mid)
        active = state == 0.0
        ge = cnt >= kf
        lo = jnp.where(active, jnp.where(ge, mid, lo), lo)
        hi = jnp.where(active, jnp.where(ge, hi, mid), hi)
        state = jnp.where(active, jnp.where(cnt == kf, 1.0, 0.0), state)
        return lo, hi, state

    lo, hi, state = lax.fori_loop(0, BISECT_STEPS, search_body, (lo0, hi0, state0))
    settled = slab_settled(state)
    thr = jnp.where(few, -jnp.inf, lo)

    def snap_body(st):
        settled, hi, thr, state = st
        cand = below_max(hi, settled)
        cnt = count(lambda blk, pos, c: blk >= c, cand, settled=settled)
        active = state == 0.0
        found = cnt >= kf
        thr = jnp.where(active, jnp.where(found, cand, thr), thr)
        hi = jnp.where(active, jnp.where(found, hi, cand), hi)
        state = jnp.where(active, jnp.where(found, jnp.where(cnt == kf, 1.0, 2.0), 0.0), state)
        return slab_settled(state), hi, thr, state

    _, _, thr, state = lax.while_loop(
        lambda st: functools.reduce(jnp.minimum, st[0]) < 0.5, snap_body, (settled, hi, thr, state))

    last = []
    for r0 in slabs:
        rows = pl.ds(r0, COUNT_ROWS)
        thr_r, row_r, state_r = (a[r0:r0 + COUNT_ROWS] for a in (thr, row, state))
        far0 = jnp.full((COUNT_ROWS, LANES), -1, I32)

        def mark(kb, i, keep, far, rows=rows, row_r=row_r):
            pos = lane + (kb * tk + i * LANES)
            keep = keep & (pos <= row_r)
            sc_ref[kb * n_sub + i, rows, :] = jnp.where(keep, 0.0, NEG)
            return jnp.maximum(far, jnp.where(keep, pos, -1))

        def plain(rows=rows, thr_r=thr_r, mark=mark, far0=far0):
            def body(kb, far):
                for i in range(n_sub):
                    far = mark(kb, i, sc_ref[kb * n_sub + i, rows, :] >= thr_r, far)
                return far
            return lax.fori_loop(0, nkb, body, far0)

        def tied(rows=rows, thr_r=thr_r, mark=mark, far0=far0):
            def above(kb, c):
                for i in range(n_sub):
                    c = c + jnp.where(sc_ref[kb * n_sub + i, rows, :] > thr_r, 1.0, 0.0)
                return c
            c = lax.fori_loop(0, nkb, above, jnp.zeros((COUNT_ROWS, LANES), F32))
            need = kf - rep(jnp.sum(c, axis=1, keepdims=True))
            tri = jnp.where(lax.broadcasted_iota(I32, (tk, tk), 0) <= lax.broadcasted_iota(I32, (tk, tk), 1),
                            1.0, 0.0).astype(BF16)

            def body(kb, carry):
                far, seen = carry
                blks = [sc_ref[kb * n_sub + i, rows, :] for i in range(n_sub)]
                is_tie = jnp.concatenate([jnp.where(blk == thr_r, 1.0, 0.0) for blk in blks], axis=1)
                rank = _dot(is_tie.astype(BF16), tri)
                for i, blk in enumerate(blks):
                    rank_i = seen + rank[:, i * LANES:(i + 1) * LANES]
                    far = mark(kb, i, (blk > thr_r) | ((blk == thr_r) & (rank_i <= need)), far)
                return far, seen + rep(rank[:, tk - 1:tk])
            return lax.fori_loop(0, nkb, body, (far0, jnp.zeros((COUNT_ROWS, LANES), F32)))[0]

        far = lax.cond(jnp.max(state_r) > 1.5, tied, plain)
        last.append(rep(jnp.max(far.astype(F32), axis=1, keepdims=True)))
    last = jnp.concatenate(last, axis=0)

    qa = qa_ref[0]
    kpos = lax.broadcasted_iota(I32, (1, tk), 1)
    slopes = [LOG2E * 2.0 ** (-8.0 * (h + 1) / N_HEADS) for h in range(N_HEADS)]

    def attend(online):
        def attn_body(kb, _):
            off = pl.multiple_of(kb * tk, tk)
            rel = (kpos + (off - t0)).astype(F32)

            def chunk_logits(h, r0, c, x):
                mask = sc_ref[kb * n_sub + c, pl.ds(r0, FLASH_ROWS), :]
                return x + (mask + slopes[h] * rel[:, c * LANES:(c + 1) * LANES])

            _flash_step(qa, kat_ref, va_ref, off, chunk_logits, m_ref, l_ref, acc_ref, p_ref, online)
            return 0

        lax.fori_loop(0, nkb, attn_body, 0)
        _flash_finish(l_ref, acc_ref, o_ref)

    bound = bound_ref[0, 0]

    @pl.when(bound <= SAFE_BOUND)
    def _():
        _flash_reset(m_ref, l_ref, acc_ref, online=False)
        for h in range(N_HEADS):
            m_ref[h] = bound + slopes[h] * (last - t0.astype(F32))
        attend(online=False)

    @pl.when(jnp.logical_not(bound <= SAFE_BOUND))
    def _():
        _flash_reset(m_ref, l_ref, acc_ref, online=True)
        attend(online=True)


def _dsa_attn(bound, qa, qi, small, kat, va, kit, *, tq=512, tk=512):
    b, s, _ = qa.shape
    row = lambda i, j: (i, j, 0)
    whole = lambda i, j: (i, 0, 0)
    once = pl.Buffered(1)
    topk = min(TOPK, s // 4)
    assert topk > LANES and s % tq == 0 and s % tk == 0, (s, tq, tk)
    kern = functools.partial(_dsa_kernel, tq=tq, tk=tk, topk=topk)
    return pl.pallas_call(
        kern,
        grid=(b, s // tq),
        in_specs=[pl.BlockSpec(memory_space=pltpu.SMEM),
                  pl.BlockSpec((1, tq, WIDTH), row), pl.BlockSpec((1, tq, WIDTH), row),
                  pl.BlockSpec((1, tq, LANES), row),
                  pl.BlockSpec((1, WIDTH, s), whole, pipeline_mode=once),
                  pl.BlockSpec((1, s, WIDTH), whole, pipeline_mode=once),
                  pl.BlockSpec((1, 3 * HEAD_DIM, s), whole, pipeline_mode=once)],
        out_specs=pl.BlockSpec((1, tq, WIDTH), row),
        out_shape=jax.ShapeDtypeStruct((b, s, WIDTH), BF16),
        scratch_shapes=[pltpu.VMEM((s // LANES, tq, LANES), F32)] + _flash_scratch(tq, tk),
        compiler_params=_cparams(2),
        name="dsa_attn",
    )(bound, qa, qi, small, kat, va, kit)


def _fox_kernel(bound_ref, qf_ref, small_ref, smallt_ref, kft_ref, vf_ref, o_ref,
                m_ref, l_ref, acc_ref, p_ref, *, tq, tk):
    t0 = pl.program_id(1) * tq
    n_full = t0 // tk
    nkb = (t0 + tq + tk - 1) // tk
    row = lax.broadcasted_iota(I32, (FLASH_ROWS, LANES), 0) + t0
    col = lax.broadcasted_iota(I32, (FLASH_ROWS, LANES), 1)
    qf = qf_ref[0]
    f_t0 = smallt_ref[0, SM_F:SM_F + N_HEADS, pl.ds(pl.multiple_of(t0, LANES), LANES)][:, 0:1]

    def attend(online):
        def attn_body(kb, _, causal):
            off = pl.multiple_of(kb * tk, tk)
            decay = LOG2E * (f_t0 - smallt_ref[0, SM_F:SM_F + N_HEADS, pl.ds(off, tk)])

            def chunk_logits(h, r0, c, x):
                x = x + decay[h:h + 1, c * LANES:(c + 1) * LANES]
                if causal:
                    x = jnp.where(col + (off + c * LANES) <= row + r0, x, NEG)
                return x

            _flash_step(qf, kft_ref, vf_ref, off, chunk_logits, m_ref, l_ref, acc_ref, p_ref, online)
            return 0

        lax.fori_loop(0, n_full, functools.partial(attn_body, causal=False), 0)
        lax.fori_loop(n_full, nkb, functools.partial(attn_body, causal=True), 0)
        _flash_finish(l_ref, acc_ref, o_ref)

    bound = bound_ref[0, 0]

    @pl.when(bound <= SAFE_BOUND)
    def _():
        _flash_reset(m_ref, l_ref, acc_ref, online=False)
        f_t = small_ref[0][:, SM_F:SM_F + N_HEADS]
        for h in range(N_HEADS):
            top = bound + LOG2E * (f_t0[h:h + 1, :] - f_t[:, h:h + 1])
            m_ref[h] = jnp.broadcast_to(top, (tq, LANES))
        attend(online=False)

    @pl.when(jnp.logical_not(bound <= SAFE_BOUND))
    def _():
        _flash_reset(m_ref, l_ref, acc_ref, online=True)
        attend(online=True)


def _fox_attn(bound, qf, small, smallt, kft, vf, *, tq=512, tk=512):
    b, s, _ = qf.shape
    row = lambda i, j: (i, j, 0)
    whole = lambda i, j: (i, 0, 0)
    once = pl.Buffered(1)
    kern = functools.partial(_fox_kernel, tq=tq, tk=tk)
    return pl.pallas_call(
        kern,
        grid=(b, s // tq),
        in_specs=[pl.BlockSpec(memory_space=pltpu.SMEM),
                  pl.BlockSpec((1, tq, WIDTH), row), pl.BlockSpec((1, tq, LANES), row),
                  pl.BlockSpec((1, LANES, s), whole, pipeline_mode=once),
                  pl.BlockSpec((1, WIDTH, s), whole, pipeline_mode=once),
                  pl.BlockSpec((1, s, WIDTH), whole, pipeline_mode=once)],
        out_specs=pl.BlockSpec((1, tq, WIDTH), row),
        out_shape=jax.ShapeDtypeStruct((b, s, WIDTH), BF16),
        scratch_shapes=_flash_scratch(tq, tk),
        compiler_params=_cparams(2),
        name="fox_attn",
    )(bound, qf, small, smallt, kft, vf)


def _mix_kernel(x_ref, mod_ref, g_ref, ad_ref, af_ref, wgate_ref, wbd_ref, wbf_ref, wo_ref, o_ref):
    d = x_ref.shape[2]
    x = x_ref[0]
    shift = mod_ref[0, 3:4, :]
    scale = mod_ref[0, 4:5, :]
    gate = mod_ref[0, 5:6, :]
    h = _rms_adaln(x, g_ref[...], scale, shift).astype(BF16)
    ga = _dot(h, wgate_ref[:, 0:d])
    gb = _dot(h, wgate_ref[:, d:2 * d])
    y_dsa = _dot(ad_ref[0], wbd_ref[...])
    y_fox = _dot(af_ref[0], wbf_ref[...])
    merged = jax.nn.sigmoid(ga) * y_dsa + jax.nn.sigmoid(gb) * y_fox
    o_ref[0] = x + gate * _dot(merged.astype(BF16), wo_ref[...])


def _mix_out(x, mod, gain, a_dsa, a_fox, w_gate, w_br_dsa, w_br_fox, w_out, *, tm=512):
    b, s, d = x.shape
    row = lambda i, j: (i, j, 0)
    return pl.pallas_call(
        _mix_kernel,
        grid=(b, s // tm),
        in_specs=[pl.BlockSpec((1, tm, d), row),
                  pl.BlockSpec((1, N_MOD, d), lambda i, j: (i, 0, 0)),
                  _const_spec((1, d)),
                  pl.BlockSpec((1, tm, WIDTH), row), pl.BlockSpec((1, tm, WIDTH), row),
                  _const_spec(w_gate.shape), _const_spec(w_br_dsa.shape),
                  _const_spec(w_br_fox.shape), _const_spec(w_out.shape)],
        out_specs=pl.BlockSpec((1, tm, d), row),
        out_shape=jax.ShapeDtypeStruct((b, s, d), F32),
        compiler_params=_cparams(2),
        name="mix_out",
    )(x, mod, gain.reshape(1, d), a_dsa, a_fox, w_gate, w_br_dsa, w_br_fox, w_out)


def _layer(x, mod, norm1_g, ffn1_wg, ffn1_wu, ffn1_wd, norm2_g, w_in, b_forget,
           qn_dsa, kn_dsa, qn_fox, kn_fox, w_br_dsa, w_br_fox, w_out, norm3_g,
           ffn2_wg, ffn2_wu, ffn2_wd):
    d = x.shape[2]
    bf = lambda w: w.astype(BF16)
    x = _ffn(x, mod, norm1_g, bf(ffn1_wg), bf(ffn1_wu), bf(ffn1_wd), mod_base=0)

    n_main = 7 * WIDTH
    n_small = HEAD_DIM + 2 * N_HEADS
    w_main = bf(w_in[:, :n_main])
    w_small = jnp.zeros((d, LANES), BF16).at[:, :n_small].set(bf(w_in[:, n_main:n_main + n_small]))
    w_gate = bf(w_in[:, n_main + n_small:])
    b_small = jnp.zeros((1, LANES), F32).at[0, SM_F:SM_F + N_HEADS].set(b_forget)
    q_scale = LOG2E * HEAD_DIM ** -0.5
    q_gains = jnp.stack([jnp.tile(qn_dsa, N_HEADS), jnp.tile(qn_fox, N_HEADS)]) * q_scale
    k_gains = jnp.stack([kn_dsa, kn_fox], axis=1)

    qa, kat, va, qf, kft, vf, qi, small, smallt, kit = _in_proj(
        x, mod, norm2_g, w_main, w_small, b_small, q_gains, k_gains)
    def logit_bound(qn, kn):
        b = 1.05 * HEAD_DIM * q_scale * jnp.max(jnp.abs(qn)) * jnp.max(jnp.abs(kn))
        return b.reshape(1, 1).astype(F32)

    a_dsa = _dsa_attn(logit_bound(qn_dsa, kn_dsa), qa, qi, small, kat, va, kit)
    a_fox = _fox_attn(logit_bound(qn_fox, kn_fox), qf, small, smallt, kft, vf)
    x = _mix_out(x, mod, norm2_g, a_dsa, a_fox, w_gate, bf(w_br_dsa), bf(w_br_fox), bf(w_out))
    return _ffn(x, mod, norm3_g, bf(ffn2_wg), bf(ffn2_wu), bf(ffn2_wd), mod_base=6)


def kernel(x, c, ada_w, ada_b, norm1_g, ffn1_wg, ffn1_wu, ffn1_wd, norm2_g, w_in, b_forget,
           qn_dsa, kn_dsa, qn_fox, kn_fox, w_br_dsa, w_br_fox, w_out, norm3_g,
           ffn2_wg, ffn2_wu, ffn2_wd):
    per_layer = (norm1_g, ffn1_wg, ffn1_wu, ffn1_wd, norm2_g, w_in, b_forget,
                 qn_dsa, kn_dsa, qn_fox, kn_fox, w_br_dsa, w_br_fox, w_out, norm3_g,
                 ffn2_wg, ffn2_wu, ffn2_wd)
    for l in range(ada_w.shape[0]):
        mod = _adaln_mod(c, ada_w, ada_b, l)
        x = _layer(x, mod, *(p[l] for p in per_layer))
    return x
```
